```python
import jax
import jax.numpy as jnp
from jax import lax
import numpy as np

D_MODEL = 2048
BATCH = 32
SEQ = 256
DEPTH = 4
DEC_BATCH = 4
DEC_SEQ = 4096
PAST_LEN = 512

GRID_W = 64
N_DIR = 2
D_MIX = D_MODEL
GROUP_W = D_MIX // 4
S5_CH = 16
S5_GROUPS = GROUP_W // S5_CH
S5_P = 64
MLSTM_HEADS = 4
MLSTM_DH = GROUP_W // MLSTM_HEADS
RET_HEADS = 4
RET_DH = GROUP_W // RET_HEADS
SSD_HEADS = 8
SSD_P = GROUP_W // SSD_HEADS
SSD_GROUPS = 2
SSD_HPG = SSD_HEADS // SSD_GROUPS
SSD_N = 128
SSD_CONV = 5
SSD_XBC = GROUP_W + 2 * SSD_GROUPS * SSD_N
D_FF = 4 * D_MODEL
CHUNK = 128
EPS = 1e-6
ROPE_BASE = 10000.0
IN_SIZES = (GROUP_W,
            GROUP_W, GROUP_W, GROUP_W, GROUP_W, N_DIR * MLSTM_HEADS, N_DIR * MLSTM_HEADS,
            GROUP_W, GROUP_W, GROUP_W, GROUP_W,
            GROUP_W, SSD_XBC, N_DIR * SSD_HEADS)
D_IN = 10 * GROUP_W + 2 * N_DIR * MLSTM_HEADS + SSD_XBC + N_DIR * SSD_HEADS

kernel_name = 'hymba_parallel_ssm_flow_backbone'


def _f32(a):
    return a.astype(jnp.float32)


def _rmsnorm(x, g):
    xf = _f32(x)
    y = xf * lax.rsqrt(jnp.mean(xf * xf, axis=-1, keepdims=True) + EPS)
    return (y * _f32(g)).astype(x.dtype)


def _head_rmsnorm(y, g):
    y = y * lax.rsqrt(jnp.mean(y * y, axis=-1, keepdims=True) + EPS)
    return y.reshape(y.shape[0], y.shape[1], -1) * _f32(g)


def _flip(a):
    return a[:, ::-1]


def _chunks(a):
    b, l = a.shape[0], a.shape[1]
    return jnp.moveaxis(a.reshape(b, l // CHUNK, CHUNK, *a.shape[2:]), 1, 0)


def _unchunks(a):
    a = jnp.moveaxis(a, 0, 1)
    return a.reshape(a.shape[0], -1, *a.shape[3:])


def _causal():
    return jnp.tril(jnp.ones((CHUNK, CHUNK), dtype=bool))


def _split_cols(proj):
    out, start = [], 0
    for size in IN_SIZES:
        out.append(proj[..., start:start + size])
        start += size
    return out


def _rope_2d(a, n_rows):
    rows = jnp.repeat(jnp.arange(n_rows, dtype=jnp.float32), GRID_W)
    cols = jnp.tile(jnp.arange(GRID_W, dtype=jnp.float32), n_rows)
    quarter = RET_DH // 4
    freqs = ROPE_BASE ** (-jnp.arange(quarter, dtype=jnp.float32) / quarter)
    ang = jnp.concatenate([rows[:, None] * freqs, cols[:, None] * freqs], axis=-1)
    cos = jnp.cos(ang)[None, :, None, :]
    sin = jnp.sin(ang)[None, :, None, :]
    a1, a2 = a[..., :RET_DH // 2], a[..., RET_DH // 2:]
    return jnp.concatenate([a1 * cos - a2 * sin, a1 * sin + a2 * cos], axis=-1)


def _dwconv_centred(x, w, b):
    pad = (w.shape[0] - 1) // 2
    out = lax.conv_general_dilated(x, w[:, None, :], window_strides=(1,), padding=[(pad, pad)],
                                   dimension_numbers=('NWC', 'WIO', 'NWC'),
                                   feature_group_count=x.shape[-1])
    return out + b


def _s5_scan(bu, lam_bar, h0):
    bu = bu.at[:, 0].add(lam_bar * h0)
    a = jnp.broadcast_to(lam_bar, (1,) + bu.shape[1:])

    def combine(e1, e2):
        a1, b1 = e1
        a2, b2 = e2
        return a1 * a2, a2 * b1 + b2

    _, h = lax.associative_scan(combine, (a, bu), axis=1)
    return h


def _s5_mixer(u, st_re, st_im, p):
    bsz, seq = u.shape[0], u.shape[1]
    uc = u.reshape(bsz, seq, S5_GROUPS, S5_CH).astype(jnp.complex64)
    h0 = lax.complex(st_re, st_im)
    y = _f32(p['s5_d']) * u
    finals = []
    for di in range(N_DIR):
        lam = lax.complex(_f32(p['s5_lambda_re'][di]), _f32(p['s5_lambda_im'][di]))
        dt = jnp.exp(_f32(p['s5_log_dt'][di]))[:, None]
        lam_bar = jnp.exp(lam * dt)
        b_bar = ((lam_bar - 1) / lam)[..., None] * lax.complex(_f32(p['s5_b_re'][di]), _f32(p['s5_b_im'][di]))
        c_mat = lax.complex(_f32(p['s5_c_re'][di]), _f32(p['s5_c_im'][di]))
        src = uc if di == 0 else _flip(uc)
        h = _s5_scan(jnp.einsum('blgc,gpc->blgp', src, b_bar), lam_bar, h0[:, di])
        finals.append(h[:, -1])
        yd = jnp.einsum('blgp,gcp->blgc', h, c_mat).real.reshape(bsz, seq, GROUP_W)
        y = y + (yd if di == 0 else _flip(yd))
    y = jax.nn.gelu(y)
    y = y * jax.nn.sigmoid(y @ _f32(p['s5_w_glu']))
    fin = jnp.stack(finals, axis=1)
    return y, (fin.real, fin.imag)


def _mlstm_scan(q, k, v, ig, lf, c0, n0, m0):
    mask = _causal()[None, :, :, None]

    def step(carry, inp):
        cm, nm, mm = carry
        qc, kc, vc, ic, fc = inp
        b = jnp.cumsum(fc, axis=1)
        dmat = jnp.where(mask, b[:, :, None] - b[:, None, :] + ic[:, None, :], -jnp.inf)
        m_inter = b + mm[:, None]
        m_t = jnp.maximum(m_inter, dmat.max(axis=2))
        s = jnp.einsum('bthd,bshd->btsh', qc, kc) * jnp.exp(dmat - m_t[:, :, None])
        w_inter = jnp.exp(m_inter - m_t)
        num = jnp.einsum('btsh,bshe->bthe', s, vc) + w_inter[..., None] * jnp.einsum('bthd,bhde->bthe', qc, cm)
        den = s.sum(axis=2) + w_inter * jnp.einsum('bthd,bhd->bth', qc, nm)
        h = num / jnp.maximum(jnp.abs(den), jnp.exp(-m_t))[..., None]
        g = b[:, -1:] - b + ic
        m_new = jnp.maximum(b[:, -1] + mm, g.max(axis=1))
        w_s = jnp.exp(g - m_new[:, None])
        decay_old = jnp.exp(b[:, -1] + mm - m_new)
        c_new = decay_old[..., None, None] * cm + jnp.einsum('bsh,bshd,bshe->bhde', w_s, kc, vc)
        n_new = decay_old[..., None] * nm + jnp.einsum('bsh,bshd->bhd', w_s, kc)
        return (c_new, n_new, m_new), h

    carry, h = lax.scan(step, (c0, n0, m0), tuple(_chunks(a) for a in (q, k, v, ig, lf)))
    return _unchunks(h), carry


def _mlstm_mixer(q, k, v, o, i_pre, f_pre, c0, n0, m0, p):
    bsz, seq = q.shape[0], q.shape[1]
    q, k, v = (a.reshape(bsz, seq, MLSTM_HEADS, MLSTM_DH) for a in (q, k, v))
    k = k * MLSTM_DH ** -0.5
    ig = i_pre.reshape(bsz, seq, N_DIR, MLSTM_HEADS) + _f32(p['mlstm_i_bias'])
    lf = jax.nn.log_sigmoid(f_pre.reshape(bsz, seq, N_DIR, MLSTM_HEADS) + _f32(p['mlstm_f_bias']))
    h_sum = 0.0
    cs, ns, ms = [], [], []
    for di in range(N_DIR):
        seqs = (q, k, v, ig[:, :, di], lf[:, :, di])
        if di == 1:
            seqs = tuple(_flip(a) for a in seqs)
        h, (c1, n1, m1) = _mlstm_scan(*seqs, c0[:, di], n0[:, di], m0[:, di])
        h_sum = h_sum + (h if di == 0 else _flip(h))
        cs.append(c1)
        ns.append(n1)
        ms.append(m1)
    y = jax.nn.sigmoid(o) * _head_rmsnorm(h_sum, p['mlstm_norm'])
    return y, (jnp.stack(cs, axis=1), jnp.stack(ns, axis=1), jnp.stack(ms, axis=1))


def _ret_scan(q, k, v, log_gamma, s0):
    idx = jnp.arange(CHUNK, dtype=jnp.float32)
    mask = _causal()
    diff = jnp.where(mask, idx[:, None] - idx[None, :], 0.0)
    dmat = jnp.where(mask[..., None], jnp.exp(diff[..., None] * log_gamma), 0.0)
    q_decay = jnp.exp((idx[:, None] + 1.0) * log_gamma)
    k_decay = jnp.exp((CHUNK - 1.0 - idx[:, None]) * log_gamma)
    chunk_decay = jnp.exp(CHUNK * log_gamma)

    def step(s, inp):
        qc, kc, vc = inp
        att = jnp.einsum('bthd,bshd->btsh', qc, kc) * dmat
        y = jnp.einsum('btsh,bshe->bthe', att, vc) + q_decay[:, :, None] * jnp.einsum('bthd,bhde->bthe', qc, s)
        s = chunk_decay[:, None, None] * s + jnp.einsum('bshd,sh,bshe->bhde', kc, k_decay, vc)
        return s, y

    s1, y = lax.scan(step, s0, tuple(_chunks(a) for a in (q, k, v)))
    return _unchunks(y), s1


def _ret_mixer(q, k, v, g, s0, p, n_rows):
    bsz, seq = q.shape[0], q.shape[1]
    q, k, v = (a.reshape(bsz, seq, RET_HEADS, RET_DH) for a in (q, k, v))
    if n_rows is not None:
        q = _rope_2d(q, n_rows)
        k = _rope_2d(k, n_rows)
    k = k * RET_DH ** -0.5
    log_gamma = -jnp.exp(_f32(p['ret_log_rate']))
    y_sum = 0.0
    finals = []
    for di in range(N_DIR):
        seqs = (q, k, v) if di == 0 else (_flip(q), _flip(k), _flip(v))
        y, s1 = _ret_scan(*seqs, log_gamma[di], s0[:, di])
        y_sum = y_sum + (y if di == 0 else _flip(y))
        finals.append(s1)
    y = _head_rmsnorm(y_sum, p['ret_norm']) * jax.nn.silu(g)
    return y, jnp.stack(finals, axis=1)


def _ssd_scan(xs, bm, cm, dt, a, h0):
    mask = _causal()[None, :, :, None, None]

    def step(h, inp):
        xc, bc, cc, dtc, ac = inp
        cum = jnp.cumsum(ac, axis=1)
        decay = jnp.exp(jnp.where(mask, cum[:, :, None] - cum[:, None, :], -jnp.inf))
        cb = jnp.einsum('btgn,bsgn->btsg', cc, bc)
        y = jnp.einsum('btsg,btsgr,bsgrp->btgrp', cb, decay, dtc[..., None] * xc)
        y = y + jnp.exp(cum)[..., None] * jnp.einsum('btgn,bgrpn->btgrp', cc, h)
        w = jnp.exp(cum[:, -1:] - cum) * dtc
        h = jnp.exp(cum[:, -1])[..., None, None] * h + jnp.einsum('bsgn,bsgr,bsgrp->bgrpn', bc, w, xc)
        return h, y

    h1, y = lax.scan(step, h0, tuple(_chunks(t) for t in (xs, bm, cm, dt, a)))
    return _unchunks(y), h1


def _ssd_mixer(z, xbc, dt_pre, h0, p):
    bsz, seq = z.shape[0], z.shape[1]
    xbc = jax.nn.silu(_dwconv_centred(xbc, _f32(p['ssd_conv_w']), _f32(p['ssd_conv_b'])))
    xs = xbc[..., :GROUP_W].reshape(bsz, seq, SSD_GROUPS, SSD_HPG, SSD_P)
    bm = xbc[..., GROUP_W:GROUP_W + SSD_GROUPS * SSD_N].reshape(bsz, seq, SSD_GROUPS, SSD_N)
    cm = xbc[..., GROUP_W + SSD_GROUPS * SSD_N:].reshape(bsz, seq, SSD_GROUPS, SSD_N)
    dt = jax.nn.softplus(dt_pre.reshape(bsz, seq, N_DIR, SSD_GROUPS, SSD_HPG)
                         + _f32(p['ssd_dt_bias']).reshape(N_DIR, SSD_GROUPS, SSD_HPG))
    a_rate = -jnp.exp(_f32(p['ssd_a_log'])).reshape(N_DIR, SSD_GROUPS, SSD_HPG)
    h0 = h0.reshape(bsz, N_DIR, SSD_GROUPS, SSD_HPG, SSD_P, SSD_N)
    y = _f32(p['ssd_d']).reshape(SSD_GROUPS, SSD_HPG, 1) * xs
    finals = []
    for di in range(N_DIR):
        seqs = (xs, bm, cm, dt[:, :, di], dt[:, :, di] * a_rate[di])
        if di == 1:
            seqs = tuple(_flip(t) for t in seqs)
        yd, h1 = _ssd_scan(*seqs, h0[:, di])
        y = y + (yd if di == 0 else _flip(yd))
        finals.append(h1.reshape(bsz, SSD_HEADS, SSD_P, SSD_N))
    y = y.reshape(bsz, seq, GROUP_W) * jax.nn.silu(z)
    y = y * lax.rsqrt(jnp.mean(y * y, axis=-1, keepdims=True) + EPS) * _f32(p['ssd_norm'])
    return y, jnp.stack(finals, axis=1)


def _layer(x, mod, state, p, n_rows):
    sh1, sc1, g1, sh2, sc2, g2 = jnp.split(mod[:, None, :], 6, axis=-1)
    h = _rmsnorm(x, p['norm1']) * (1 + sc1) + sh1
    proj = _f32(h @ p['w_in'])
    (s5_u, m_q, m_k, m_v, m_o, m_i, m_f, r_q, r_k, r_v, r_g, d_z, d_xbc, d_dt) = _split_cols(proj)
    s5_re, s5_im, m_c, m_n, m_m, r_s, d_h = (_f32(s) for s in state)
    y_a, (s5_re, s5_im) = _s5_mixer(s5_u, s5_re, s5_im, p)
    y_b, (m_c, m_n, m_m) = _mlstm_mixer(m_q, m_k, m_v, m_o, m_i, m_f, m_c, m_n, m_m, p)
    y_c, r_s = _ret_mixer(r_q, r_k, r_v, r_g, r_s, p, n_rows)
    y_d, d_h = _ssd_mixer(d_z, d_xbc, d_dt, d_h, p)
    y = jnp.concatenate([y_a, y_b, y_c, y_d], axis=-1).astype(x.dtype) @ p['w_out']
    x = x + g1 * y
    h2 = _rmsnorm(x, p['norm2']) * (1 + sc2) + sh2
    x = x + g2 * (jnp.square(jax.nn.relu(h2 @ p['w_mlp1'])) @ p['w_mlp2'])
    return x, (s5_re, s5_im, m_c, m_n, m_m, r_s, d_h)


def setup_inputs(seed: int = 0) -> dict:
    key = jax.random.key(seed)
    it = iter(jax.random.split(key, 48))

    def nrm(shape, scale):
        return jax.random.normal(next(it), shape, jnp.float32) * scale

    def unif(shape, lo, hi):
        return jax.random.uniform(next(it), shape, jnp.float32, lo, hi)

    L2 = (DEPTH, N_DIR)
    s5_im0 = jnp.pi * jnp.arange(S5_P, dtype=jnp.float32)
    ret_gamma = 1.0 - 2.0 ** (-5.0 - jnp.arange(RET_HEADS, dtype=jnp.float32))
    ssd_dt = jnp.exp(unif(L2 + (SSD_HEADS,), float(np.log(1e-3)), float(np.log(1e-1))))
    return {
        'x_prompt': nrm((BATCH, SEQ, D_MODEL), 1.0),
        'x_sample': nrm((DEC_BATCH, DEC_SEQ, D_MODEL), 1.0),
        'c': nrm((DEC_BATCH, D_MODEL), 1.0),
        'state_s5_re': nrm((DEC_BATCH, DEPTH, N_DIR, S5_GROUPS, S5_P), 0.1),
        'state_s5_im': nrm((DEC_BATCH, DEPTH, N_DIR, S5_GROUPS, S5_P), 0.1),
        'state_mlstm_c': nrm((DEC_BATCH, DEPTH, N_DIR, MLSTM_HEADS, MLSTM_DH, MLSTM_DH), 0.1),
        'state_mlstm_n': nrm((DEC_BATCH, DEPTH, N_DIR, MLSTM_HEADS, MLSTM_DH), 0.1),
        'state_mlstm_m': nrm((DEC_BATCH, DEPTH, N_DIR, MLSTM_HEADS), 1.0),
        'state_ret': nrm((DEC_BATCH, DEPTH, N_DIR, RET_HEADS, RET_DH, RET_DH), 0.1),
        'state_ssd': nrm((DEC_BATCH, DEPTH, N_DIR, SSD_HEADS, SSD_P, SSD_N), 0.1),
        'c_ctx': nrm((D_MODEL,), 1.0),
        'norm1': 1.0 + nrm((DEPTH, D_MODEL), 0.01),
        'w_ada': nrm((DEPTH, D_MODEL, 6 * D_MODEL), 0.5 * D_MODEL ** -0.5),
        'b_ada': nrm((DEPTH, 6 * D_MODEL), 0.01),
        'w_in': nrm((DEPTH, D_MODEL, D_IN), D_MODEL ** -0.5),
        's5_lambda_re': -0.5 + nrm(L2 + (S5_GROUPS, S5_P), 0.01),
        's5_lambda_im': s5_im0 + nrm(L2 + (S5_GROUPS, S5_P), 0.01),
        's5_log_dt': unif(L2 + (S5_GROUPS,), float(np.log(1e-3)), float(np.log(1e-1))),
        's5_b_re': nrm(L2 + (S5_GROUPS, S5_P, S5_CH), (2 * S5_CH) ** -0.5),
        's5_b_im': nrm(L2 + (S5_GROUPS, S5_P, S5_CH), (2 * S5_CH) ** -0.5),
        's5_c_re': nrm(L2 + (S5_GROUPS, S5_CH, S5_P), (2 * S5_P) ** -0.5),
        's5_c_im': nrm(L2 + (S5_GROUPS, S5_CH, S5_P), (2 * S5_P) ** -0.5),
        's5_d': nrm((DEPTH, GROUP_W), 1.0),
        's5_w_glu': nrm((DEPTH, GROUP_W, GROUP_W), GROUP_W ** -0.5),
        'mlstm_i_bias': nrm(L2 + (MLSTM_HEADS,), 0.1),
        'mlstm_f_bias': jnp.linspace(3.0, 6.0, MLSTM_HEADS) + nrm(L2 + (MLSTM_HEADS,), 0.1),
        'mlstm_norm': 1.0 + nrm((DEPTH, GROUP_W), 0.01),
        'ret_log_rate': jnp.log(-jnp.log(ret_gamma)) + nrm(L2 + (RET_HEADS,), 0.01),
        'ret_norm': 1.0 + nrm((DEPTH, GROUP_W), 0.01),
        'ssd_conv_w': nrm((DEPTH, SSD_CONV, SSD_XBC), SSD_CONV ** -0.5),
        'ssd_conv_b': nrm((DEPTH, SSD_XBC), 0.01),
        'ssd_dt_bias': ssd_dt + jnp.log(-jnp.expm1(-ssd_dt)),
        'ssd_a_log': jnp.log(unif(L2 + (SSD_HEADS,), 1.0, 16.0)),
        'ssd_d': 1.0 + nrm((DEPTH, SSD_HEADS), 0.01),
        'ssd_norm': 1.0 + nrm((DEPTH, GROUP_W), 0.01),
        'w_out': nrm((DEPTH, D_MIX, D_MODEL), D_MIX ** -0.5),
        'norm2': 1.0 + nrm((DEPTH, D_MODEL), 0.01),
        'w_mlp1': nrm((DEPTH, D_MODEL, D_FF), D_MODEL ** -0.5),
        'w_mlp2': nrm((DEPTH, D_FF, D_MODEL), D_FF ** -0.5),
        'final_norm': 1.0 + nrm((D_MODEL,), 0.01),
    }


def reference(x_prompt, x_sample, c, state_s5_re, state_s5_im, state_mlstm_c, state_mlstm_n,
              state_mlstm_m, state_ret, state_ssd, c_ctx, norm1, w_ada, b_ada, w_in,
              s5_lambda_re, s5_lambda_im, s5_log_dt, s5_b_re, s5_b_im, s5_c_re, s5_c_im, s5_d,
              s5_w_glu, mlstm_i_bias, mlstm_f_bias, mlstm_norm, ret_log_rate, ret_norm,
              ssd_conv_w, ssd_conv_b, ssd_dt_bias, ssd_a_log, ssd_d, ssd_norm, w_out, norm2,
              w_mlp1, w_mlp2, final_norm):
    n_rows = x_sample.shape[1] // GRID_W
    bp = x_prompt.shape[0]
    f32 = jnp.float32
    zero_state = (jnp.zeros((bp, N_DIR, S5_GROUPS, S5_P), f32),
                  jnp.zeros((bp, N_DIR, S5_GROUPS, S5_P), f32),
                  jnp.zeros((bp, N_DIR, MLSTM_HEADS, MLSTM_DH, MLSTM_DH), f32),
                  jnp.zeros((bp, N_DIR, MLSTM_HEADS, MLSTM_DH), f32),
                  jnp.zeros((bp, N_DIR, MLSTM_HEADS), f32),
                  jnp.zeros((bp, N_DIR, RET_HEADS, RET_DH, RET_DH), f32),
                  jnp.zeros((bp, N_DIR, SSD_HEADS, SSD_P, SSD_N), f32))
    x_ctx, x_lat = x_prompt, x_sample
    ctx_states = []
    for l in range(DEPTH):
        p = {'norm1': norm1[l], 'w_in': w_in[l],
             's5_lambda_re': s5_lambda_re[l], 's5_lambda_im': s5_lambda_im[l], 's5_log_dt': s5_log_dt[l],
             's5_b_re': s5_b_re[l], 's5_b_im': s5_b_im[l], 's5_c_re': s5_c_re[l], 's5_c_im': s5_c_im[l],
             's5_d': s5_d[l], 's5_w_glu': s5_w_glu[l],
             'mlstm_i_bias': mlstm_i_bias[l], 'mlstm_f_bias': mlstm_f_bias[l], 'mlstm_norm': mlstm_norm[l],
             'ret_log_rate': ret_log_rate[l], 'ret_norm': ret_norm[l],
             'ssd_conv_w': ssd_conv_w[l], 'ssd_conv_b': ssd_conv_b[l], 'ssd_dt_bias': ssd_dt_bias[l],
             'ssd_a_log': ssd_a_log[l], 'ssd_d': ssd_d[l], 'ssd_norm': ssd_norm[l],
             'w_out': w_out[l], 'norm2': norm2[l], 'w_mlp1': w_mlp1[l], 'w_mlp2': w_mlp2[l]}
        mod_ctx = jax.nn.silu(c_ctx)[None] @ w_ada[l] + b_ada[l]
        mod_lat = jax.nn.silu(c) @ w_ada[l] + b_ada[l]
        x_ctx, st = _layer(x_ctx, mod_ctx, zero_state, p, None)
        ctx_states.append(st)
        cache_l = (state_s5_re[:, l], state_s5_im[:, l], state_mlstm_c[:, l], state_mlstm_n[:, l],
                   state_mlstm_m[:, l], state_ret[:, l], state_ssd[:, l])
        x_lat, _ = _layer(x_lat, mod_lat, cache_l, p, n_rows)
    y_prompt = _rmsnorm(x_ctx, final_norm)
    y_sample = _rmsnorm(x_lat, final_norm)
    new_s5_re = jnp.stack([s[0] for s in ctx_states], axis=1)
    new_s5_im = jnp.stack([s[1] for s in ctx_states], axis=1)
    new_mlstm_c = jnp.stack([s[2] for s in ctx_states], axis=1)
    new_mlstm_n = jnp.stack([s[3] for s in ctx_states], axis=1)
    new_mlstm_m = jnp.stack([s[4] for s in ctx_states], axis=1)
    new_ret = jnp.stack([s[5] for s in ctx_states], axis=1)
    new_ssd = jnp.stack([s[6] for s in ctx_states], axis=1)
    return (y_prompt, y_sample, new_s5_re, new_s5_im, new_mlstm_c, new_mlstm_n, new_mlstm_m, new_ret, new_ssd)
```

```python
import functools

import jax
import jax.numpy as jnp
from jax import lax
from jax.experimental import pallas as pl
from jax.experimental.pallas import tpu as pltpu

D_MODEL = 2048
DEPTH = 4
GRID_W = 64
N_DIR = 2
GROUP_W = D_MODEL // 4
S5_CH = 16
S5_GROUPS = GROUP_W // S5_CH
S5_P = 64
S5_ST = S5_GROUPS * S5_P
S5_CB = 4
MLSTM_HEADS = 4
MLSTM_DH = GROUP_W // MLSTM_HEADS
RET_HEADS = 4
RET_DH = GROUP_W // RET_HEADS
SSD_HEADS = 8
SSD_P = GROUP_W // SSD_HEADS
SSD_GROUPS = 2
SSD_HPG = SSD_HEADS // SSD_GROUPS
SSD_N = 128
SSD_CONV = 5
SSD_XBC = GROUP_W + 2 * SSD_GROUPS * SSD_N
D_FF = 4 * D_MODEL
CHUNK = 128
EPS = 1e-6
ROPE_BASE = 10000.0
D_IN = 10 * GROUP_W + 2 * N_DIR * MLSTM_HEADS + SSD_XBC + N_DIR * SSD_HEADS
LANES = 128
HALO = 8
D_IN_P = 10 * GROUP_W + SSD_XBC + N_DIR * LANES
COL_S5, COL_MQ, COL_MK, COL_MV, COL_MO, COL_RQ, COL_RK, COL_RV, COL_RG, COL_DZ = range(10)
COL_XBC = 10 * GROUP_W // SSD_XBC
COL_GATE = (10 * GROUP_W + SSD_XBC) // LANES
VMEM_LIMIT = 56 * 1024 * 1024

F32 = jnp.float32
BF16 = jnp.bfloat16


def _bdot(a, b):
    return jnp.dot(a.astype(BF16), b.astype(BF16), preferred_element_type=F32)


def _bdot_nt(a, b):
    return lax.dot_general(a.astype(BF16), b.astype(BF16), (((1,), (1,)), ((), ())),
                           preferred_element_type=F32)


def _split3(x):
    x1 = x.astype(BF16)
    r1 = x - x1.astype(F32)
    x2 = r1.astype(BF16)
    x3 = (r1 - x2.astype(F32)).astype(BF16)
    return x1, x2, x3


def _dot01_left(m01, x):
    return sum(jnp.dot(m01, p, preferred_element_type=F32) for p in _split3(x))


def _dot01_right(x, m01):
    return sum(jnp.dot(p, m01, preferred_element_type=F32) for p in _split3(x))


def _dir_masks(d):
    row = lax.broadcasted_iota(jnp.int32, (CHUNK, CHUNK), 0)
    col = lax.broadcasted_iota(jnp.int32, (CHUNK, CHUNK), 1)
    sgn = 2 * d - 1
    mask = (row - col) * sgn >= 0
    mask_t = (col - row) * sgn >= 0
    return mask, mask_t, row


def _to01(mask):
    return jnp.where(mask, 1.0, 0.0).astype(BF16)


def _chunk_index(d, n, nc):
    return jnp.where(d == 1, n, nc - 1 - n)


def _cparams(n_axes):
    return pltpu.CompilerParams(dimension_semantics=("arbitrary",) * n_axes,
                                vmem_limit_bytes=VMEM_LIMIT)


def _ada_kernel(c_ref, w_ref, b_ref, o_ref):
    c = c_ref[...]
    a = c * jax.nn.sigmoid(c)
    o_ref[...] = _bdot(a, w_ref[...]) + b_ref[...]


def _ada(cvec, w_ada, b_ada, tn=1024):
    n = w_ada.shape[-1]
    return pl.pallas_call(
        _ada_kernel,
        grid=(DEPTH, n // tn),
        in_specs=[pl.BlockSpec((8, D_MODEL), lambda l, j: (0, 0)),
                  pl.BlockSpec((None, D_MODEL, tn), lambda l, j: (l, 0, j)),
                  pl.BlockSpec((None, 1, tn), lambda l, j: (l, 0, j))],
        out_specs=pl.BlockSpec((None, 8, tn), lambda l, j: (l, 0, j)),
        out_shape=jax.ShapeDtypeStruct((DEPTH, 8, n), F32),
        compiler_params=_cparams(2),
        name="ada",
    )(cvec, w_ada, b_ada.reshape(DEPTH, 1, n))


def _mod_spec(layer, path, seq_len, tm, which, n_grid_axes):
    def index(i, *_):
        r = path.mod_base + path.mod_per_seq * ((i * tm) // seq_len)
        return ((layer * 8 + r) * 6 + which, 0, 0)
    return pl.BlockSpec((None, 1, D_MODEL), index)


class _Path:
    def __init__(self, n_seq, seq_len, mod_base, mod_per_seq, rope):
        self.n_seq, self.seq_len = n_seq, seq_len
        self.mod_base, self.mod_per_seq, self.rope = mod_base, mod_per_seq, rope
        self.rows = n_seq * seq_len
        self.nc = seq_len // CHUNK


def _rms(x, g):
    return x * lax.rsqrt(jnp.mean(x * x, axis=-1, keepdims=True) + EPS) * g


def _in_proj_kernel(x_ref, g_ref, sc_ref, sh_ref, w_ref, o_ref, h_scr):
    @pl.when(pl.program_id(1) == 0)
    def _():
        h = _rms(x_ref[...], g_ref[...]) * (1.0 + sc_ref[...]) + sh_ref[...]
        h_scr[...] = h.astype(BF16)
    o_ref[...] = jnp.dot(h_scr[...], w_ref[...], preferred_element_type=F32)


def _in_proj(x, mod3, norm1, w_in_p, layer, path, tm=512, tn=1280):
    rows = x.shape[0]
    return pl.pallas_call(
        _in_proj_kernel,
        grid=(rows // tm, D_IN_P // tn),
        in_specs=[pl.BlockSpec((tm, D_MODEL), lambda i, j: (i, 0)),
                  pl.BlockSpec((None, 1, D_MODEL), lambda i, j: (layer, 0, 0)),
                  _mod_spec(layer, path, path.seq_len, tm, 1, 2),
                  _mod_spec(layer, path, path.seq_len, tm, 0, 2),
                  pl.BlockSpec((None, D_MODEL, tn), lambda i, j: (layer, 0, j))],
        out_specs=pl.BlockSpec((tm, tn), lambda i, j: (i, j)),
        out_shape=jax.ShapeDtypeStruct((rows, D_IN_P), F32),
        scratch_shapes=[pltpu.VMEM((tm, D_MODEL), BF16)],
        compiler_params=_cparams(2),
        name="in_proj",
    )(x, norm1.reshape(DEPTH, 1, D_MODEL), mod3, mod3, w_in_p)


def _row_spec(path, width, col_block):
    nc = path.nc
    return pl.BlockSpec((CHUNK, width),
                        lambda s, d, n: (s * nc + _chunk_index(d, n, nc), col_block))


def _gate_spec(path):
    nc = path.nc
    return pl.BlockSpec((CHUNK, LANES),
                        lambda s, d, n: (s * nc + _chunk_index(d, n, nc), COL_GATE + 1 - d))


def _out_row_spec(path):
    nc = path.nc
    return pl.BlockSpec((CHUNK, GROUP_W), lambda s, d, n: (s * nc + n * d, 0))


def _dir_spec(shape):
    zeros = (0,) * len(shape)
    return pl.BlockSpec((None,) + shape, lambda s, d, n: (1 - d,) + zeros)


def _const_spec(shape):
    zeros = (0,) * len(shape)
    return pl.BlockSpec(shape, lambda s, d, n: zeros)


def _state_spec(shape):
    zeros = (0,) * len(shape)
    return pl.BlockSpec((None, None) + shape, lambda s, d, n: (s, 1 - d) + zeros)


def _head_norm(y, heads, dh):
    parts = []
    for h in range(heads):
        seg = y[:, h * dh:(h + 1) * dh]
        parts.append(seg * lax.rsqrt(jnp.mean(seg * seg, axis=-1, keepdims=True) + EPS))
    return jnp.concatenate(parts, axis=1)


def _s5_kernel(u_ref, bre_ref, bim_ref, cre_ref, cim_ref, lr_ref, li_ref, dskip_ref, wglu_ref,
               h0r_ref, h0i_ref, y_ref, hnr_ref, hni_ref,
               bur, bui, hre, him, st_r, st_i, ybwd, *, nc):
    d = pl.program_id(1)
    n = pl.program_id(2)
    chunk = _chunk_index(d, n, nc)

    @pl.when(n == 0)
    def _():
        st_r[...] = h0r_ref[...]
        st_i[...] = h0i_ref[...]

    u = u_ref[...]
    w = S5_ST // S5_CB
    for cb in range(S5_CB):
        ucb = u[:, cb * LANES:(cb + 1) * LANES]
        bur[:, cb * w:(cb + 1) * w] = _bdot(ucb, bre_ref[cb])
        bui[:, cb * w:(cb + 1) * w] = _bdot(ucb, bim_ref[cb])
    lr = lr_ref[...]
    li = li_ref[...]

    def step(i, carry):
        hr, hi = carry
        t = jnp.where(d == 1, i, CHUNK - 1 - i)
        nr = lr * hr - li * hi + bur[pl.ds(t, 1), :]
        ni = lr * hi + li * hr + bui[pl.ds(t, 1), :]
        hre[pl.ds(t, 1), :] = nr
        him[pl.ds(t, 1), :] = ni
        return nr, ni

    hr, hi = lax.fori_loop(0, CHUNK, step, (st_r[...], st_i[...]))
    st_r[...] = hr
    st_i[...] = hi

    y = jnp.concatenate(
        [_bdot(hre[:, cb * w:(cb + 1) * w], cre_ref[cb]) + _bdot(him[:, cb * w:(cb + 1) * w], cim_ref[cb])
         for cb in range(S5_CB)], axis=1)
    off = pl.multiple_of(chunk * CHUNK, CHUNK)

    @pl.when(d == 0)
    def _():
        ybwd[pl.ds(off, CHUNK), :] = y

    @pl.when(d == 1)
    def _():
        t = jax.nn.gelu(dskip_ref[...] * u + y + ybwd[pl.ds(off, CHUNK), :])
        y_ref[...] = (t * jax.nn.sigmoid(_bdot(t, wglu_ref[...]))).astype(BF16)

    @pl.when(n == nc - 1)
    def _():
        hnr_ref[...] = hr
        hni_ref[...] = hi


def _s5_params(lam_re, lam_im, log_dt, b_re, b_im, c_re, c_im):
    lam = lax.complex(lam_re, lam_im)
    dt = jnp.exp(log_dt)[..., None]
    lam_bar = jnp.exp(lam * dt)
    b_bar = ((lam_bar - 1) / lam)[..., None] * lax.complex(b_re, b_im)
    gpb = S5_GROUPS // S5_CB
    eye = jnp.eye(gpb, dtype=F32)

    def pack_b(b):
        b = b.reshape(N_DIR, S5_CB, gpb, S5_P, S5_CH)
        m = jnp.einsum('dbgpc,gh->dbgchp', b, eye)
        return m.reshape(N_DIR, S5_CB, gpb * S5_CH, gpb * S5_P).astype(BF16)

    def pack_c(c):
        c = c.reshape(N_DIR, S5_CB, gpb, S5_CH, S5_P)
        m = jnp.einsum('dbgcp,gh->dbgphc', c, eye)
        return m.reshape(N_DIR, S5_CB, gpb * S5_P, gpb * S5_CH).astype(BF16)

    return (pack_b(b_bar.real), pack_b(b_bar.imag), pack_c(c_re), pack_c(-c_im),
            lam_bar.real.reshape(N_DIR, 1, S5_ST), lam_bar.imag.reshape(N_DIR, 1, S5_ST))


def _s5_mixer(proj, prm, h0r, h0i, path):
    nc = path.nc
    bre, bim, cre, cim, lr, li, dskip, wglu = prm
    gpb = S5_GROUPS // S5_CB
    out_shapes = (jax.ShapeDtypeStruct((path.rows, GROUP_W), BF16),
                  jax.ShapeDtypeStruct((path.n_seq, N_DIR, 1, S5_ST), F32),
                  jax.ShapeDtypeStruct((path.n_seq, N_DIR, 1, S5_ST), F32))
    return pl.pallas_call(
        functools.partial(_s5_kernel, nc=nc),
        grid=(path.n_seq, N_DIR, nc),
        in_specs=[_row_spec(path, GROUP_W, COL_S5),
                  _dir_spec((S5_CB, gpb * S5_CH, gpb * S5_P)), _dir_spec((S5_CB, gpb * S5_CH, gpb * S5_P)),
                  _dir_spec((S5_CB, gpb * S5_P, gpb * S5_CH)), _dir_spec((S5_CB, gpb * S5_P, gpb * S5_CH)),
                  _dir_spec((1, S5_ST)), _dir_spec((1, S5_ST)),
                  _const_spec((1, GROUP_W)), _const_spec((GROUP_W, GROUP_W)),
                  _state_spec((1, S5_ST)), _state_spec((1, S5_ST))],
        out_specs=(_out_row_spec(path), _state_spec((1, S5_ST)), _state_spec((1, S5_ST))),
        out_shape=out_shapes,
        scratch_shapes=[pltpu.VMEM((CHUNK, S5_ST), F32)] * 4
                       + [pltpu.VMEM((1, S5_ST), F32)] * 2
                       + [pltpu.VMEM((path.seq_len, GROUP_W), F32)],
        compiler_params=_cparams(3),
        name="s5",
    )(proj, bre, bim, cre, cim, lr, li, dskip, wglu, h0r, h0i)


def _mlstm_kernel(q_ref, k_ref, v_ref, o_ref, g_ref, bias_ref, norm_ref, c0_ref, n0_ref, m0_ref,
                  y_ref, cn_ref, nn_ref, mn_ref, caug, mscr, ybwd, *, nc):
    d = pl.program_id(1)
    n = pl.program_id(2)
    chunk = _chunk_index(d, n, nc)
    dh = MLSTM_DH

    @pl.when(n == 0)
    def _():
        for h in range(MLSTM_HEADS):
            caug[h, :, 0:dh] = c0_ref[h]
            caug[h, :, dh:2 * dh] = jnp.broadcast_to(n0_ref[h], (dh, dh))
            mscr[h] = m0_ref[h]

    mask, mask_t, _ = _dir_masks(d)
    m01, mt01 = _to01(mask), _to01(mask_t)
    g = g_ref[...] + bias_ref[...]
    lane = lax.broadcasted_iota(jnp.int32, (CHUNK, LANES), 1)
    p = jnp.where(lane < MLSTM_HEADS, g,
                  jnp.where(lane < 2 * MLSTM_HEADS, jax.nn.log_sigmoid(g), 0.0))
    pt = p.T
    bcol = _dot01_left(m01, p)
    brow = _dot01_right(pt, mt01)
    btot = jnp.sum(p, axis=0, keepdims=True)
    ones = jnp.ones((CHUNK, dh), F32)
    outs = []
    for h in range(MLSTM_HEADS):
        sl = slice(h * dh, (h + 1) * dh)
        q = q_ref[:, sl]
        k = k_ref[:, sl] * (dh ** -0.5)
        v = v_ref[:, sl]
        fl = MLSTM_HEADS + h
        bc = bcol[:, fl:fl + 1]
        br = brow[fl:fl + 1, :]
        igr = pt[h:h + 1, :]
        igc = p[:, h:h + 1]
        bt = btot[:, fl:fl + 1]
        mm = mscr[h][:, 0:1]
        dm = jnp.where(mask, bc - br + igr, -jnp.inf)
        m_inter = bc + mm
        m_t = jnp.maximum(m_inter, jnp.max(dm, axis=1, keepdims=True))
        sc = _bdot_nt(q, k) * jnp.exp(dm - m_t)
        w_inter = jnp.exp(m_inter - m_t)
        cst = caug[h]
        qc = _bdot(q, cst)
        num = _bdot(sc, v) + w_inter * qc[:, 0:dh]
        den = jnp.sum(sc, axis=1, keepdims=True) + w_inter * qc[:, dh:dh + 1]
        outs.append(num / jnp.maximum(jnp.abs(den), jnp.exp(-m_t)))
        gcol = bt - bc + igc
        m_new = jnp.maximum(bt + mm, jnp.max(gcol, axis=0, keepdims=True))
        w_s = jnp.exp(gcol - m_new)
        decay_old = jnp.exp(bt + mm - m_new)
        vaug = jnp.concatenate([v, ones], axis=1)
        caug[h] = decay_old * cst + _bdot((w_s * k).T, vaug)
        mscr[h] = jnp.broadcast_to(m_new, (1, LANES))
    hcat = jnp.concatenate(outs, axis=1)
    off = pl.multiple_of(chunk * CHUNK, CHUNK)

    @pl.when(d == 0)
    def _():
        ybwd[pl.ds(off, CHUNK), :] = hcat

    @pl.when(d == 1)
    def _():
        hs = _head_norm(hcat + ybwd[pl.ds(off, CHUNK), :], MLSTM_HEADS, dh) * norm_ref[...]
        y_ref[...] = (jax.nn.sigmoid(o_ref[...]) * hs).astype(BF16)

    @pl.when(n == nc - 1)
    def _():
        for h in range(MLSTM_HEADS):
            cn_ref[h] = caug[h, :, 0:dh]
            nn_ref[h] = caug[h, :, dh:dh + 1]
            mn_ref[h] = mscr[h]


def _mlstm_mixer(proj, bias, norm, c0, n0, m0, path):
    nc = path.nc
    hh, dh = MLSTM_HEADS, MLSTM_DH
    out_shapes = (jax.ShapeDtypeStruct((path.rows, GROUP_W), BF16),
                  jax.ShapeDtypeStruct((path.n_seq, N_DIR, hh, dh, dh), F32),
                  jax.ShapeDtypeStruct((path.n_seq, N_DIR, hh, dh, 1), F32),
                  jax.ShapeDtypeStruct((path.n_seq, N_DIR, hh, 1, LANES), F32))
    return pl.pallas_call(
        functools.partial(_mlstm_kernel, nc=nc),
        grid=(path.n_seq, N_DIR, nc),
        in_specs=[_row_spec(path, GROUP_W, COL_MQ), _row_spec(path, GROUP_W, COL_MK),
                  _row_spec(path, GROUP_W, COL_MV), _row_spec(path, GROUP_W, COL_MO),
                  _gate_spec(path), _dir_spec((1, LANES)), _const_spec((1, GROUP_W)),
                  _state_spec((hh, dh, dh)), _state_spec((hh, dh, 1)), _state_spec((hh, 1, LANES))],
        out_specs=(_out_row_spec(path), _state_spec((hh, dh, dh)), _state_spec((hh, dh, 1)),
                   _state_spec((hh, 1, LANES))),
        out_shape=out_shapes,
        scratch_shapes=[pltpu.VMEM((hh, dh, 2 * dh), F32), pltpu.VMEM((hh, 1, LANES), F32),
                        pltpu.VMEM((path.seq_len, GROUP_W), F32)],
        compiler_params=_cparams(3),
        name="mlstm",
    )(proj, proj, proj, proj, proj, bias, norm, c0, n0, m0)


def _ret_kernel(*refs, nc, rope):
    if rope:
        (q_ref, k_ref, v_ref, g_ref, lg_ref, norm_ref, s0_ref, cos_ref, sin_ref,
         y_ref, sn_ref, sscr, ybwd) = refs
    else:
        (q_ref, k_ref, v_ref, g_ref, lg_ref, norm_ref, s0_ref,
         y_ref, sn_ref, sscr, ybwd) = refs
    d = pl.program_id(1)
    n = pl.program_id(2)
    chunk = _chunk_index(d, n, nc)
    dh = RET_DH

    @pl.when(n == 0)
    def _():
        sscr[...] = s0_ref[...]

    mask, _, row = _dir_masks(d)
    col = lax.broadcasted_iota(jnp.int32, (CHUNK, CHUNK), 1)
    absdiff = jnp.abs(row - col).astype(F32)
    pos = jnp.where(d == 1, row, CHUNK - 1 - row).astype(F32)
    outs = []
    for h in range(RET_HEADS):
        sl = slice(h * dh, (h + 1) * dh)
        q = q_ref[:, sl]
        k = k_ref[:, sl]
        v = v_ref[:, sl]
        if rope:
            cos, sin = cos_ref[...], sin_ref[...]
            q = q * cos + pltpu.roll(q, dh // 2, 1) * sin
            k = k * cos + pltpu.roll(k, dh // 2, 1) * sin
        k = k * (dh ** -0.5)
        lg = -jnp.exp(lg_ref[:, sl])
        dmat = jnp.where(mask, jnp.exp(absdiff * lg), 0.0)
        q_decay = jnp.exp((pos + 1.0) * lg)
        k_decay = jnp.exp((CHUNK - 1.0 - pos) * lg)
        st = sscr[h]
        att = _bdot_nt(q, k) * dmat
        outs.append(_bdot(att, v) + q_decay * _bdot(q, st))
        sscr[h] = jnp.exp(CHUNK * lg) * st + _bdot((k * k_decay).T, v)
    ycat = jnp.concatenate(outs, axis=1)
    off = pl.multiple_of(chunk * CHUNK, CHUNK)

    @pl.when(d == 0)
    def _():
        ybwd[pl.ds(off, CHUNK), :] = ycat

    @pl.when(d == 1)
    def _():
        ys = _head_norm(ycat + ybwd[pl.ds(off, CHUNK), :], RET_HEADS, dh) * norm_ref[...]
        gate = g_ref[...]
        y_ref[...] = (ys * (gate * jax.nn.sigmoid(gate))).astype(BF16)

    @pl.when(n == nc - 1)
    def _():
        sn_ref[...] = sscr[...]


def _rope_tables(seq_len):
    n_rows = seq_len // GRID_W
    rows = jnp.repeat(jnp.arange(n_rows, dtype=F32), GRID_W)
    cols = jnp.tile(jnp.arange(GRID_W, dtype=F32), n_rows)
    quarter = RET_DH // 4
    freqs = ROPE_BASE ** (-jnp.arange(quarter, dtype=F32) / quarter)
    ang = jnp.concatenate([rows[:, None] * freqs, cols[:, None] * freqs], axis=-1)
    cos, sin = jnp.cos(ang), jnp.sin(ang)
    return jnp.concatenate([cos, cos], axis=-1), jnp.concatenate([-sin, sin], axis=-1)


def _ret_mixer(proj, lg, norm, s0, rope_tabs, path):
    nc = path.nc
    hh, dh = RET_HEADS, RET_DH
    in_specs = [_row_spec(path, GROUP_W, COL_RQ), _row_spec(path, GROUP_W, COL_RK),
                _row_spec(path, GROUP_W, COL_RV), _row_spec(path, GROUP_W, COL_RG),
                _dir_spec((1, GROUP_W)), _const_spec((1, GROUP_W)), _state_spec((hh, dh, dh))]
    args = [proj, proj, proj, proj, lg, norm, s0]
    if path.rope:
        tab_spec = pl.BlockSpec((CHUNK, dh), lambda s, d, n: (_chunk_index(d, n, nc), 0))
        in_specs += [tab_spec, tab_spec]
        args += list(rope_tabs)
    return pl.pallas_call(
        functools.partial(_ret_kernel, nc=nc, rope=path.rope),
        grid=(path.n_seq, N_DIR, nc),
        in_specs=in_specs,
        out_specs=(_out_row_spec(path), _state_spec((hh, dh, dh))),
        out_shape=(jax.ShapeDtypeStruct((path.rows, GROUP_W), BF16),
                   jax.ShapeDtypeStruct((path.n_seq, N_DIR, hh, dh, dh), F32)),
        scratch_shapes=[pltpu.VMEM((hh, dh, dh), F32), pltpu.VMEM((path.seq_len, GROUP_W), F32)],
        compiler_params=_cparams(3),
        name="ret",
    )(*args)


def _ssd_kernel(z_ref, x_ref, xp_ref, xn_ref, g_ref, bias_ref, alog_ref, cw_ref, cb_ref, dsk_ref,
                norm_ref, h0_ref, y_ref, hn_ref, ext, hscr, ybwd, *, nc):
    d = pl.program_id(1)
    n = pl.program_id(2)
    chunk = _chunk_index(d, n, nc)
    gw = SSD_HPG * SSD_P

    @pl.when(n == 0)
    def _():
        for gi in range(SSD_GROUPS):
            hscr[gi] = h0_ref[gi * SSD_HPG:(gi + 1) * SSD_HPG].reshape(gw, SSD_N).T

    ext[0:HALO, :] = jnp.where(chunk > 0, xp_ref[...], 0.0)
    ext[HALO:HALO + CHUNK, :] = x_ref[...]
    ext[HALO + CHUNK:, :] = jnp.where(chunk < nc - 1, xn_ref[...], 0.0)
    pad = (SSD_CONV - 1) // 2
    acc = cb_ref[...] + cw_ref[0:1, :] * ext[HALO - pad:HALO - pad + CHUNK, :]
    for j in range(1, SSD_CONV):
        acc = acc + cw_ref[j:j + 1, :] * ext[HALO - pad + j:HALO - pad + j + CHUNK, :]
    xbc = acc * jax.nn.sigmoid(acc)
    xs = xbc[:, 0:GROUP_W]

    mask, mask_t, _ = _dir_masks(d)
    m01, mt01 = _to01(mask), _to01(mask_t)
    lane = lax.broadcasted_iota(jnp.int32, (CHUNK, LANES), 1)
    dt_lanes = (lane >= 2 * MLSTM_HEADS) & (lane < 2 * MLSTM_HEADS + SSD_HEADS)
    dt = jnp.where(dt_lanes, jax.nn.softplus(g_ref[...] + bias_ref[...]), 0.0)
    a = dt * -jnp.exp(alog_ref[...])
    at = a.T
    ccol = _dot01_left(m01, a)
    crow = _dot01_right(at, mt01)
    ctot = jnp.sum(a, axis=0, keepdims=True)

    y_parts = []
    for gi in range(SSD_GROUPS):
        bm = xbc[:, GROUP_W + gi * SSD_N:GROUP_W + (gi + 1) * SSD_N]
        cm = xbc[:, GROUP_W + (SSD_GROUPS + gi) * SSD_N:GROUP_W + (SSD_GROUPS + gi + 1) * SSD_N]
        cb = _bdot_nt(cm, bm)
        hst = hscr[gi]
        inter = _bdot(cm, hst)
        wx_parts, e_parts, tot_parts = [], [], []
        for r in range(SSD_HPG):
            hd = gi * SSD_HPG + r
            ln = 2 * MLSTM_HEADS + hd
            x = xs[:, hd * SSD_P:(hd + 1) * SSD_P]
            cc = ccol[:, ln:ln + 1]
            cr = crow[ln:ln + 1, :]
            dtc = dt[:, ln:ln + 1]
            tot = ctot[:, ln:ln + 1]
            decay = jnp.exp(jnp.where(mask, cc - cr, -jnp.inf))
            y = _bdot(cb * decay, dtc * x) + jnp.exp(cc) * inter[:, r * SSD_P:(r + 1) * SSD_P]
            y_parts.append(y)
            wx_parts.append((jnp.exp(tot - cc) * dtc) * x)
            tot_parts.append(jnp.broadcast_to(jnp.exp(tot), (1, SSD_P)))
        wx = jnp.concatenate(wx_parts, axis=1)
        hscr[gi] = jnp.concatenate(tot_parts, axis=1) * hst + _bdot(bm.T, wx)
    ycat = jnp.concatenate(y_parts, axis=1)
    off = pl.multiple_of(chunk * CHUNK, CHUNK)

    @pl.when(d == 0)
    def _():
        ybwd[pl.ds(off, CHUNK), :] = ycat

    @pl.when(d == 1)
    def _():
        z = z_ref[...]
        yy = (dsk_ref[...] * xs + ycat + ybwd[pl.ds(off, CHUNK), :]) * (z * jax.nn.sigmoid(z))
        y_ref[...] = _rms(yy, norm_ref[...]).astype(BF16)

    @pl.when(n == nc - 1)
    def _():
        for gi in range(SSD_GROUPS):
            hn_ref[gi * SSD_HPG:(gi + 1) * SSD_HPG] = hscr[gi].T.reshape(SSD_HPG, SSD_P, SSD_N)


def _ssd_mixer(proj, bias, alog, conv_w, conv_b, dskip, norm, h0, path):
    nc = path.nc
    per = CHUNK // HALO
    n_halo = path.rows // HALO

    def prev_idx(s, d, n):
        return (jnp.maximum((s * nc + _chunk_index(d, n, nc)) * per - 1, 0), COL_XBC)

    def next_idx(s, d, n):
        return (jnp.minimum((s * nc + _chunk_index(d, n, nc) + 1) * per, n_halo - 1), COL_XBC)

    st_shape = (SSD_HEADS, SSD_P, SSD_N)
    return pl.pallas_call(
        functools.partial(_ssd_kernel, nc=nc),
        grid=(path.n_seq, N_DIR, nc),
        in_specs=[_row_spec(path, GROUP_W, COL_DZ), _row_spec(path, SSD_XBC, COL_XBC),
                  pl.BlockSpec((HALO, SSD_XBC), prev_idx), pl.BlockSpec((HALO, SSD_XBC), next_idx),
                  _gate_spec(path), _dir_spec((1, LANES)), _dir_spec((1, LANES)),
                  _const_spec((SSD_CONV, SSD_XBC)), _const_spec((1, SSD_XBC)),
                  _const_spec((1, GROUP_W)), _const_spec((1, GROUP_W)), _state_spec(st_shape)],
        out_specs=(_out_row_spec(path), _state_spec(st_shape)),
        out_shape=(jax.ShapeDtypeStruct((path.rows, GROUP_W), BF16),
                   jax.ShapeDtypeStruct((path.n_seq, N_DIR) + st_shape, F32)),
        scratch_shapes=[pltpu.VMEM((CHUNK + 2 * HALO, SSD_XBC), F32),
                        pltpu.VMEM((SSD_GROUPS, SSD_N, SSD_HPG * SSD_P), F32),
                        pltpu.VMEM((path.seq_len, GROUP_W), F32)],
        compiler_params=_cparams(3),
        name="ssd",
    )(proj, proj, proj, proj, proj, bias, alog, conv_w, conv_b, dskip, norm, h0)


def _out_proj_kernel(ya_ref, yb_ref, yc_ref, yd_ref, w_ref, x_ref, g1_ref, g_ref, sc_ref, sh_ref,
                     x1_ref, h2_ref):
    acc = jnp.dot(ya_ref[...], w_ref[0], preferred_element_type=F32)
    for i, y_ref in enumerate((yb_ref, yc_ref, yd_ref)):
        acc = acc + jnp.dot(y_ref[...], w_ref[i + 1], preferred_element_type=F32)
    x1 = x_ref[...] + g1_ref[...] * acc
    x1_ref[...] = x1
    h2_ref[...] = (_rms(x1, g_ref[...]) * (1.0 + sc_ref[...]) + sh_ref[...]).astype(BF16)


def _out_proj(ys, x, mod3, norm2, w_out_p, layer, path, tm=256):
    rows = x.shape[0]
    yspec = pl.BlockSpec((tm, GROUP_W), lambda i: (i, 0))
    xspec = pl.BlockSpec((tm, D_MODEL), lambda i: (i, 0))
    return pl.pallas_call(
        _out_proj_kernel,
        grid=(rows // tm,),
        in_specs=[yspec, yspec, yspec, yspec,
                  pl.BlockSpec((None, 4, GROUP_W, D_MODEL), lambda i: (layer, 0, 0, 0)),
                  xspec,
                  _mod_spec(layer, path, path.seq_len, tm, 2, 1),
                  pl.BlockSpec((None, 1, D_MODEL), lambda i: (layer, 0, 0)),
                  _mod_spec(layer, path, path.seq_len, tm, 4, 1),
                  _mod_spec(layer, path, path.seq_len, tm, 3, 1)],
        out_specs=(xspec, xspec),
        out_shape=(jax.ShapeDtypeStruct((rows, D_MODEL), F32),
                   jax.ShapeDtypeStruct((rows, D_MODEL), BF16)),
        compiler_params=_cparams(1),
        name="out_proj",
    )(*ys, w_out_p, x, mod3, norm2.reshape(DEPTH, 1, D_MODEL), mod3, mod3)


def _mlp_kernel(h2_ref, x1_ref, w1_ref, w2_ref, g2_ref, fn_ref, o_ref, acc, *, nk, final):
    k = pl.program_id(1)
    a = jnp.dot(h2_ref[...], w1_ref[...], preferred_element_type=F32)
    a = jnp.square(jnp.maximum(a, 0.0))
    part = jnp.dot(a.astype(BF16), w2_ref[...], preferred_element_type=F32)

    @pl.when(k == 0)
    def _():
        acc[...] = part

    @pl.when(k > 0)
    def _():
        acc[...] += part

    @pl.when(k == nk - 1)
    def _():
        out = x1_ref[...] + g2_ref[...] * acc[...]
        o_ref[...] = _rms(out, fn_ref[...]) if final else out


def _mlp(h2, x1, mod3, w1, w2, final_norm, layer, path, final, tm=512, tf=1024):
    rows = x1.shape[0]
    nk = D_FF // tf
    xspec = pl.BlockSpec((tm, D_MODEL), lambda i, k: (i, 0))
    return pl.pallas_call(
        functools.partial(_mlp_kernel, nk=nk, final=final),
        grid=(rows // tm, nk),
        in_specs=[xspec, xspec,
                  pl.BlockSpec((None, D_MODEL, tf), lambda i, k: (layer, 0, k)),
                  pl.BlockSpec((None, tf, D_MODEL), lambda i, k: (layer, k, 0)),
                  _mod_spec(layer, path, path.seq_len, tm, 5, 2),
                  pl.BlockSpec((1, D_MODEL), lambda i, k: (0, 0))],
        out_specs=xspec,
        out_shape=jax.ShapeDtypeStruct((rows, D_MODEL), F32),
        scratch_shapes=[pltpu.VMEM((tm, D_MODEL), F32)],
        compiler_params=_cparams(2),
        name="mlp",
    )(h2, x1, w1, w2, mod3, final_norm.reshape(1, D_MODEL))


def _permute_w_in(w_in):
    o_mi = 5 * GROUP_W
    o_mf = o_mi + N_DIR * MLSTM_HEADS
    o_r = o_mf + N_DIR * MLSTM_HEADS
    o_dt = D_IN - N_DIR * SSD_HEADS
    zeros = jnp.zeros(w_in.shape[:2] + (LANES - 2 * MLSTM_HEADS - SSD_HEADS,), w_in.dtype)
    parts = [w_in[..., :o_mi], w_in[..., o_r:o_dt]]
    for di in range(N_DIR):
        parts += [w_in[..., o_mi + di * MLSTM_HEADS:o_mi + (di + 1) * MLSTM_HEADS],
                  w_in[..., o_mf + di * MLSTM_HEADS:o_mf + (di + 1) * MLSTM_HEADS],
                  w_in[..., o_dt + di * SSD_HEADS:o_dt + (di + 1) * SSD_HEADS], zeros]
    return jnp.concatenate(parts, axis=-1).astype(BF16)


def _gate_rows(i_part, f_part, dt_part):
    pad = jnp.zeros(i_part.shape[:2] + (LANES - 2 * MLSTM_HEADS - SSD_HEADS,), F32)
    return jnp.concatenate([i_part, f_part, dt_part, pad], axis=-1)[:, :, None, :]


def kernel(x_prompt, x_sample, c, state_s5_re, state_s5_im, state_mlstm_c, state_mlstm_n, state_mlstm_m, state_ret, state_ssd, c_ctx, norm1, w_ada, b_ada, w_in, s5_lambda_re, s5_lambda_im, s5_log_dt, s5_b_re, s5_b_im, s5_c_re, s5_c_im, s5_d, s5_w_glu, mlstm_i_bias, mlstm_f_bias, mlstm_norm, ret_log_rate, ret_norm, ssd_conv_w, ssd_conv_b, ssd_dt_bias, ssd_a_log, ssd_d, ssd_norm, w_out, norm2, w_mlp1, w_mlp2, final_norm):
    bp, seq = x_prompt.shape[0], x_prompt.shape[1]
    bd, dseq = x_sample.shape[0], x_sample.shape[1]
    ctx = _Path(bp, seq, 0, 0, False)
    lat = _Path(bd, dseq, 1, 1, True)

    cvec = jnp.concatenate([c_ctx[None], c, jnp.zeros((8 - 1 - bd, D_MODEL), F32)], axis=0)
    mod3 = _ada(cvec, w_ada, b_ada).reshape(DEPTH * 8 * 6, 1, D_MODEL)

    w_in_p = _permute_w_in(w_in)
    w_out_p = w_out.astype(BF16).reshape(DEPTH, 4, GROUP_W, D_MODEL)
    w1 = w_mlp1.astype(BF16)
    w2 = w_mlp2.astype(BF16)
    wglu = s5_w_glu.astype(BF16)
    zeros8 = jnp.zeros((DEPTH, N_DIR, SSD_HEADS), F32)
    zeros4 = jnp.zeros((DEPTH, N_DIR, MLSTM_HEADS), F32)
    gate_bias = _gate_rows(mlstm_i_bias, mlstm_f_bias, ssd_dt_bias)
    alog_rows = _gate_rows(zeros4, zeros4, ssd_a_log)
    ret_lg = jnp.repeat(ret_log_rate, RET_DH, axis=-1)[:, :, None, :]
    ssd_d_row = jnp.repeat(ssd_d, SSD_P, axis=-1)[:, None, :]
    rope_tabs = _rope_tables(dseq)
    del zeros8

    def states(path, layer):
        if path is ctx:
            n = path.n_seq
            z = lambda *s: jnp.zeros((n, N_DIR) + s, F32)
            return (z(1, S5_ST), z(1, S5_ST), z(MLSTM_HEADS, MLSTM_DH, MLSTM_DH),
                    z(MLSTM_HEADS, MLSTM_DH, 1), z(MLSTM_HEADS, 1, LANES),
                    z(RET_HEADS, RET_DH, RET_DH), z(SSD_HEADS, SSD_P, SSD_N))
        return (state_s5_re[:, layer].reshape(bd, N_DIR, 1, S5_ST),
                state_s5_im[:, layer].reshape(bd, N_DIR, 1, S5_ST),
                state_mlstm_c[:, layer],
                state_mlstm_n[:, layer][..., None],
                jnp.broadcast_to(state_mlstm_m[:, layer][..., None, None],
                                 (bd, N_DIR, MLSTM_HEADS, 1, LANES)),
                state_ret[:, layer], state_ssd[:, layer])

    xs = {ctx: x_prompt.reshape(ctx.rows, D_MODEL), lat: x_sample.reshape(lat.rows, D_MODEL)}
    new_states = []
    for layer in range(DEPTH):
        s5_prm = _s5_params(s5_lambda_re[layer], s5_lambda_im[layer], s5_log_dt[layer],
                            s5_b_re[layer], s5_b_im[layer], s5_c_re[layer], s5_c_im[layer])
        s5_prm = s5_prm + (s5_d[layer][None], wglu[layer])
        for path in (ctx, lat):
            x = xs[path]
            st = states(path, layer)
            proj = _in_proj(x, mod3, norm1, w_in_p, layer, path)
            ya, s5r, s5i = _s5_mixer(proj, s5_prm, st[0], st[1], path)
            yb, mc, mn, mm = _mlstm_mixer(proj, gate_bias[layer], mlstm_norm[layer][None],
                                          st[2], st[3], st[4], path)
            yc, rs = _ret_mixer(proj, ret_lg[layer], ret_norm[layer][None], st[5], rope_tabs, path)
            yd, dh = _ssd_mixer(proj, gate_bias[layer], alog_rows[layer], ssd_conv_w[layer],
                                ssd_conv_b[layer][None], ssd_d_row[layer], ssd_norm[layer][None],
                                st[6], path)
            x1, h2 = _out_proj((ya, yb, yc, yd), x, mod3, norm2, w_out_p, layer, path)
            xs[path] = _mlp(h2, x1, mod3, w1, w2, final_norm, layer, path, layer == DEPTH - 1)
            if path is ctx:
                new_states.append((s5r.reshape(bp, N_DIR, S5_GROUPS, S5_P),
                                   s5i.reshape(bp, N_DIR, S5_GROUPS, S5_P),
                                   mc, mn[..., 0], mm[..., 0, 0], rs, dh))
    y_prompt = xs[ctx].reshape(bp, seq, D_MODEL)
    y_sample = xs[lat].reshape(bd, dseq, D_MODEL)
    stacked = tuple(jnp.stack([s[i] for s in new_states], axis=1) for i in range(7))
    return (y_prompt, y_sample) + stacked
```

```python
import functools

import jax
import jax.numpy as jnp
from jax import lax
from jax.experimental import pallas as pl
from jax.experimental.pallas import tpu as pltpu

D_MODEL = 2048
DEPTH = 4
GRID_W = 64
N_DIR = 2
GROUP_W = D_MODEL // 4
S5_CH = 16
S5_GROUPS = GROUP_W // S5_CH
S5_P = 64
S5_ST = S5_GROUPS * S5_P
S5_CB = 4
S5_BW = S5_ST // S5_CB
S5_SEG = 8
S5_SLEN = 16
MLSTM_HEADS = 4
MLSTM_DH = GROUP_W // MLSTM_HEADS
RET_HEADS = 4
RET_DH = GROUP_W // RET_HEADS
SSD_HEADS = 8
SSD_P = GROUP_W // SSD_HEADS
SSD_GROUPS = 2
SSD_HPG = SSD_HEADS // SSD_GROUPS
SSD_N = 128
SSD_GW = SSD_HPG * SSD_P
SSD_CONV = 5
SSD_XBC = GROUP_W + 2 * SSD_GROUPS * SSD_N
D_FF = 4 * D_MODEL
CHUNK = 128
EPS = 1e-6
ROPE_BASE = 10000.0
D_IN = 10 * GROUP_W + 2 * N_DIR * MLSTM_HEADS + SSD_XBC + N_DIR * SSD_HEADS
LANES = 128
HALO = 8
D_IN_P = 10 * GROUP_W + SSD_XBC + N_DIR * LANES
COL_S5, COL_MQ, COL_MK, COL_MV, COL_MO, COL_RQ, COL_RK, COL_RV, COL_RG, COL_DZ = range(10)
COL_XBC = 10 * GROUP_W // SSD_XBC
COL_GATE = (10 * GROUP_W + SSD_XBC) // LANES
GATE_I, GATE_F, GATE_DT = 0, MLSTM_HEADS, 2 * MLSTM_HEADS
CPS = 2
VMEM_LIMIT = 56 * 1024 * 1024

F32 = jnp.float32
BF16 = jnp.bfloat16


def _bdot(a, b):
    return jnp.dot(a.astype(BF16), b.astype(BF16), preferred_element_type=F32)


def _bdot_nt(a, b):
    return lax.dot_general(a.astype(BF16), b.astype(BF16), (((1,), (1,)), ((), ())),
                           preferred_element_type=F32)


def _bdot_tn(a, b):
    return lax.dot_general(a.astype(BF16), b.astype(BF16), (((0,), (0,)), ((), ())),
                           preferred_element_type=F32)


def _split3(x):
    x1 = x.astype(BF16)
    r1 = x - x1.astype(F32)
    x2 = r1.astype(BF16)
    x3 = (r1 - x2.astype(F32)).astype(BF16)
    return x1, x2, x3


def _dot01_left(m01, x):
    return sum(jnp.dot(m01, p, preferred_element_type=F32) for p in _split3(x))


def _dot01_right(x, m01):
    return sum(jnp.dot(p, m01, preferred_element_type=F32) for p in _split3(x))


def _iota2():
    row = lax.broadcasted_iota(jnp.int32, (CHUNK, CHUNK), 0)
    col = lax.broadcasted_iota(jnp.int32, (CHUNK, CHUNK), 1)
    return row, col


def _dir_mask(di, row, col):
    return row >= col if di == 0 else row <= col


def _to01(mask):
    return jnp.where(mask, 1.0, 0.0).astype(BF16)


def _cparams(n_axes):
    return pltpu.CompilerParams(dimension_semantics=("arbitrary",) * n_axes,
                                vmem_limit_bytes=VMEM_LIMIT)


def _rms(x, g):
    return x * lax.rsqrt(jnp.mean(x * x, axis=-1, keepdims=True) + EPS) * g


def _head_norm(y, heads, dh):
    parts = []
    for h in range(heads):
        seg = y[:, h * dh:(h + 1) * dh]
        parts.append(seg * lax.rsqrt(jnp.mean(seg * seg, axis=-1, keepdims=True) + EPS))
    return jnp.concatenate(parts, axis=1)


class _Path:
    def __init__(self, n_seq, seq_len, mod_base, mod_per_seq, rope):
        self.n_seq, self.seq_len = n_seq, seq_len
        self.mod_base, self.mod_per_seq, self.rope = mod_base, mod_per_seq, rope
        self.rows = n_seq * seq_len
        self.nc = seq_len // CHUNK
        self.n_chunks = self.rows // CHUNK


def _ada_kernel(c_ref, w_ref, b_ref, o_ref):
    c = c_ref[...]
    a = c * jax.nn.sigmoid(c)
    o_ref[...] = _bdot(a, w_ref[...]) + b_ref[...]


def _ada(cvec, w_ada, b_ada, tn=1024):
    n = w_ada.shape[-1]
    return pl.pallas_call(
        _ada_kernel,
        grid=(DEPTH, n // tn),
        in_specs=[pl.BlockSpec((8, D_MODEL), lambda l, j: (0, 0)),
                  pl.BlockSpec((None, D_MODEL, tn), lambda l, j: (l, 0, j)),
                  pl.BlockSpec((None, 1, tn), lambda l, j: (l, 0, j))],
        out_specs=pl.BlockSpec((None, 8, tn), lambda l, j: (l, 0, j)),
        out_shape=jax.ShapeDtypeStruct((DEPTH, 8, n), F32),
        compiler_params=_cparams(2),
        name="ada",
    )(cvec, w_ada, b_ada.reshape(DEPTH, 1, n))


def _mod_spec(layer, path, tm, which):
    def index(i, *_):
        r = path.mod_base + path.mod_per_seq * ((i * tm) // path.seq_len)
        return ((layer * 8 + r) * 6 + which, 0, 0)
    return pl.BlockSpec((None, 1, D_MODEL), index)


def _in_proj_kernel(x_ref, g_ref, sc_ref, sh_ref, w_ref, o_ref, h_scr):
    @pl.when(pl.program_id(1) == 0)
    def _():
        h = _rms(x_ref[...], g_ref[...]) * (1.0 + sc_ref[...]) + sh_ref[...]
        h_scr[...] = h.astype(BF16)
    o_ref[...] = jnp.dot(h_scr[...], w_ref[...], preferred_element_type=F32)


def _in_proj(x, mod3, norm1, w_in_p, layer, path, tm=1024, tn=1280):
    rows = x.shape[0]
    return pl.pallas_call(
        _in_proj_kernel,
        grid=(rows // tm, D_IN_P // tn),
        in_specs=[pl.BlockSpec((tm, D_MODEL), lambda i, j: (i, 0)),
                  pl.BlockSpec((None, 1, D_MODEL), lambda i, j: (layer, 0, 0)),
                  _mod_spec(layer, path, tm, 1),
                  _mod_spec(layer, path, tm, 0),
                  pl.BlockSpec((None, D_MODEL, tn), lambda i, j: (layer, 0, j))],
        out_specs=pl.BlockSpec((tm, tn), lambda i, j: (i, j)),
        out_shape=jax.ShapeDtypeStruct((rows, D_IN_P), F32),
        scratch_shapes=[pltpu.VMEM((tm, D_MODEL), BF16)],
        compiler_params=_cparams(2),
        name="in_proj",
    )(x, norm1.reshape(DEPTH, 1, D_MODEL), mod3, mod3, w_in_p)


def _chunk_of(di, n, nc):
    return n if di == 0 else nc - 1 - n


def _seq_row_spec(path, di, width, col_block):
    nc = path.nc
    return pl.BlockSpec((CHUNK, width), lambda s, n: (s * nc + _chunk_of(di, n, nc), col_block))


def _seq_gate_spec(path, di):
    return _seq_row_spec(path, di, LANES, COL_GATE + di)


def _seq_snap_spec(path, di, shape):
    nc = path.nc
    zeros = (0,) * len(shape)
    return pl.BlockSpec((None,) + shape, lambda s, n: (s * nc + _chunk_of(di, n, nc),) + zeros)


def _seq_state_spec(shape):
    zeros = (0,) * (len(shape) + 1)
    return pl.BlockSpec((None, N_DIR) + shape, lambda s, n: (s,) + zeros)


def _full_spec(shape, n_axes=2):
    zeros = (0,) * len(shape)
    return pl.BlockSpec(shape, lambda *_: zeros)


def _out_row_spec(width, col_block):
    return pl.BlockSpec((CPS * CHUNK, width), lambda i: (i, col_block))


def _out_snap_spec(shape):
    zeros = (0,) * len(shape)
    return pl.BlockSpec((CPS,) + shape, lambda i: (i,) + zeros)


def _s5_kernel(uf_ref, ub_ref, bre_ref, bim_ref, cre_ref, cim_ref, pwr_ref, pwi_ref, h0r_ref, h0i_ref,
               yf_ref, yb_ref, hnr_ref, hni_ref,
               ut, yt, hlr, hli, st_r, st_i, *, nc):
    n = pl.program_id(1)

    @pl.when(n == 0)
    def _():
        st_r[...] = jnp.broadcast_to(h0r_ref[...], st_r.shape)
        st_i[...] = jnp.broadcast_to(h0i_ref[...], st_i.shape)

    u_refs = (uf_ref, ub_ref)
    y_refs = (yf_ref, yb_ref)
    dirs = range(N_DIR)
    seg = lax.broadcasted_iota(jnp.int32, (S5_SEG, S5_BW), 0)
    pos = (seg, S5_SEG - 1 - seg)

    def step_rows(di, kk):
        k = kk if di == 0 else S5_SLEN - 1 - kk
        return slice(k * S5_SEG, (k + 1) * S5_SEG)

    for cb in range(S5_CB):
        lanes = slice(cb * S5_BW, (cb + 1) * S5_BW)
        for di in dirs:
            ut[di, cb] = u_refs[di][:, cb * LANES:(cb + 1) * LANES]
            ucb = jnp.concatenate([ut[di, cb, pl.ds(k, S5_SEG, stride=S5_SLEN), :] for k in range(S5_SLEN)],
                                  axis=0).astype(BF16)
            hlr[di, cb] = jnp.dot(ucb, bre_ref[di, cb], preferred_element_type=F32)
            hli[di, cb] = jnp.dot(ucb, bim_ref[di, cb], preferred_element_type=F32)
        lr = [pwr_ref[di, 0, :, lanes] for di in dirs]
        li = [pwi_ref[di, 0, :, lanes] for di in dirs]
        hr = [jnp.zeros((S5_SEG, S5_BW), F32) for di in dirs]
        hi = [jnp.zeros((S5_SEG, S5_BW), F32) for di in dirs]
        for kk in range(S5_SLEN):
            for di in dirs:
                rows = step_rows(di, kk)
                hr[di], hi[di] = (lr[di] * hr[di] - li[di] * hi[di] + hlr[di, cb, rows, :],
                                  lr[di] * hi[di] + li[di] * hr[di] + hli[di, cb, rows, :])
                hlr[di, cb, rows, :] = hr[di]
                hli[di, cb, rows, :] = hi[di]
        cr, ci = [], []
        for di in dirs:
            one = 1 if di == 0 else S5_SEG - 1
            first = pos[di] == 0
            cr.append(jnp.where(first, pltpu.roll(st_r[di, :, lanes], one, 0), pltpu.roll(hr[di], one, 0)))
            ci.append(jnp.where(first, pltpu.roll(st_i[di, :, lanes], one, 0), pltpu.roll(hi[di], one, 0)))
        for lvl in range(S5_SEG.bit_length() - 1):
            dist = 1 << lvl
            for di in dirs:
                mr = pwr_ref[di, S5_SLEN - 1 + lvl, :, lanes]
                mi = pwi_ref[di, S5_SLEN - 1 + lvl, :, lanes]
                shift = dist if di == 0 else S5_SEG - dist
                sr = jnp.where(pos[di] >= dist, pltpu.roll(cr[di], shift, 0), 0.0)
                si = jnp.where(pos[di] >= dist, pltpu.roll(ci[di], shift, 0), 0.0)
                cr[di], ci[di] = cr[di] + mr * sr - mi * si, ci[di] + mr * si + mi * sr
        for di in dirs:
            l16r = pwr_ref[di, S5_SLEN - 1, :, lanes]
            l16i = pwi_ref[di, S5_SLEN - 1, :, lanes]
            st_r[di, :, lanes] = hr[di] + l16r * cr[di] - l16i * ci[di]
            st_i[di, :, lanes] = hi[di] + l16r * ci[di] + l16i * cr[di]
        for kk in range(S5_SLEN):
            for di in dirs:
                pr = pwr_ref[di, kk, :, lanes]
                pi = pwi_ref[di, kk, :, lanes]
                rows = step_rows(di, kk)
                hlr[di, cb, rows, :] = hlr[di, cb, rows, :] + pr * cr[di] - pi * ci[di]
                hli[di, cb, rows, :] = hli[di, cb, rows, :] + pr * ci[di] + pi * cr[di]
        for di in dirs:
            yt[di, cb] = (jnp.dot(hlr[di, cb].astype(BF16), cre_ref[di, cb], preferred_element_type=F32)
                          + jnp.dot(hli[di, cb].astype(BF16), cim_ref[di, cb], preferred_element_type=F32))
            per = S5_SLEN // S5_SEG
            for r in range(CHUNK // S5_SEG):
                start = (r % per) * S5_SEG * S5_SEG + r // per
                y_refs[di][r * S5_SEG:(r + 1) * S5_SEG, cb * LANES:(cb + 1) * LANES] = (
                    yt[di, cb, pl.ds(start, S5_SEG, stride=S5_SEG), :])

    @pl.when(n == nc - 1)
    def _():
        for di in range(N_DIR):
            last = S5_SEG - 1 if di == 0 else 0
            hnr_ref[di] = st_r[di, last:last + 1, :]
            hni_ref[di] = st_i[di, last:last + 1, :]


def _s5_params(lam_re, lam_im, log_dt, b_re, b_im, c_re, c_im):
    dt = jnp.exp(log_dt)[..., None]
    mag = jnp.exp(lam_re * dt)
    lbr, lbi = mag * jnp.cos(lam_im * dt), mag * jnp.sin(lam_im * dt)
    den = lam_re * lam_re + lam_im * lam_im
    fr = ((lbr - 1.0) * lam_re + lbi * lam_im) / den
    fi = (lbi * lam_re - (lbr - 1.0) * lam_im) / den
    bbr = fr[..., None] * b_re - fi[..., None] * b_im
    bbi = fr[..., None] * b_im + fi[..., None] * b_re
    gpb = S5_GROUPS // S5_CB
    eye = jnp.eye(gpb, dtype=F32)

    def pack_b(b):
        b = b.reshape(N_DIR, S5_CB, gpb, S5_P, S5_CH)
        m = jnp.einsum('dbgpc,gh->dbgchp', b, eye)
        return m.reshape(N_DIR, S5_CB, gpb * S5_CH, gpb * S5_P).astype(BF16)

    def pack_c(c):
        c = c.reshape(N_DIR, S5_CB, gpb, S5_CH, S5_P)
        m = jnp.einsum('dbgcp,gh->dbgphc', c, eye)
        return m.reshape(N_DIR, S5_CB, gpb * S5_P, gpb * S5_CH).astype(BF16)

    pr, pi = [lbr], [lbi]
    for _ in range(S5_SLEN - 1):
        pr, pi = pr + [pr[-1] * lbr - pi[-1] * lbi], pi + [pr[-1] * lbi + pi[-1] * lbr]
    for _ in range(S5_SEG.bit_length() - 2):
        pr, pi = pr + [pr[-1] * pr[-1] - pi[-1] * pi[-1]], pi + [2.0 * pr[-1] * pi[-1]]
    tab = lambda p: jnp.broadcast_to(jnp.stack(p, axis=1).reshape(N_DIR, len(p), 1, S5_ST),
                                     (N_DIR, len(p), S5_SEG, S5_ST))
    return pack_b(bbr), pack_b(bbi), pack_c(c_re), pack_c(-c_im), tab(pr), tab(pi)


def _s5_mixer(proj, prm, h0r, h0i, path):
    nc = path.nc
    bre, bim, cre, cim, pwr, pwi = prm
    yshape = jax.ShapeDtypeStruct((path.rows, GROUP_W), F32)
    stshape = jax.ShapeDtypeStruct((path.n_seq, N_DIR, 1, S5_ST), F32)
    yspec = lambda di: pl.BlockSpec((CHUNK, GROUP_W), lambda s, n: (s * nc + _chunk_of(di, n, nc), 0))
    dir_scr = lambda rows: pltpu.VMEM((N_DIR, S5_CB, rows, S5_BW), F32)
    return pl.pallas_call(
        functools.partial(_s5_kernel, nc=nc),
        grid=(path.n_seq, nc),
        in_specs=[_seq_row_spec(path, 0, GROUP_W, COL_S5), _seq_row_spec(path, 1, GROUP_W, COL_S5),
                  _full_spec(bre.shape), _full_spec(bim.shape), _full_spec(cre.shape),
                  _full_spec(cim.shape), _full_spec(pwr.shape), _full_spec(pwi.shape),
                  _seq_state_spec((1, S5_ST)), _seq_state_spec((1, S5_ST))],
        out_specs=(yspec(0), yspec(1), _seq_state_spec((1, S5_ST)), _seq_state_spec((1, S5_ST))),
        out_shape=(yshape, yshape, stshape, stshape),
        scratch_shapes=[pltpu.VMEM((N_DIR, S5_CB, CHUNK, LANES), F32)] * 2
                       + [dir_scr(CHUNK)] * 2 + [pltpu.VMEM((N_DIR, S5_SEG, S5_ST), F32)] * 2,
        compiler_params=_cparams(2),
        name="s5",
    )(proj, proj, bre, bim, cre, cim, pwr, pwi, h0r, h0i)


def _gate_act(g):
    lane = lax.broadcasted_iota(jnp.int32, g.shape, 1)
    return jnp.where(lane < GATE_F, g, jnp.where(lane < GATE_DT, jax.nn.log_sigmoid(g), 0.0))


def _mlstm_state_kernel(kf_ref, vf_ref, gf_ref, kb_ref, vb_ref, gb_ref, bias_ref, c0_ref, n0_ref, m0_ref,
                        cf_ref, mf_ref, cb_ref, mb_ref, cn_ref, nn_ref, mn_ref, caug, mscr, *, nc):
    n = pl.program_id(1)
    dh = MLSTM_DH

    @pl.when(n == 0)
    def _():
        for di in range(N_DIR):
            for h in range(MLSTM_HEADS):
                caug[di, h, 0:dh, :] = c0_ref[di, h].T
                caug[di, h, dh:2 * dh, :] = jnp.broadcast_to(n0_ref[di, h], (dh, dh))
                mscr[di, h] = m0_ref[di, h]

    row, col = _iota2()
    ones = jnp.ones((CHUNK, dh), F32)
    dirs = ((kf_ref, vf_ref, gf_ref, cf_ref, mf_ref), (kb_ref, vb_ref, gb_ref, cb_ref, mb_ref))
    for di, (k_ref, v_ref, g_ref, c_out, m_out) in enumerate(dirs):
        p = _gate_act(g_ref[...] + bias_ref[di])
        bcol = _dot01_left(_to01(_dir_mask(di, row, col)), p)
        btot = jnp.sum(p, axis=0, keepdims=True)
        for h in range(MLSTM_HEADS):
            sl = slice(h * dh, (h + 1) * dh)
            k = k_ref[:, sl] * (dh ** -0.5)
            v = v_ref[:, sl]
            bc = bcol[:, GATE_F + h:GATE_F + h + 1]
            bt = btot[:, GATE_F + h:GATE_F + h + 1]
            igc = p[:, h:h + 1]
            mrow = mscr[di, h]
            mm = mrow[:, 0:1]
            cst = caug[di, h]
            c_out[h] = cst.astype(BF16)
            m_out[h] = mrow
            gcol = bt - bc + igc
            m_new = jnp.maximum(bt + mm, jnp.max(gcol, axis=0, keepdims=True))
            w_s = jnp.exp(gcol - m_new)
            decay_old = jnp.exp(bt + mm - m_new)
            vaug = jnp.concatenate([v, ones], axis=1)
            caug[di, h] = decay_old * cst + _bdot_tn(vaug, w_s * k)
            mscr[di, h] = jnp.broadcast_to(m_new, (1, LANES))

    @pl.when(n == nc - 1)
    def _():
        for di in range(N_DIR):
            for h in range(MLSTM_HEADS):
                cn_ref[di, h] = caug[di, h, 0:dh, :].T
                nn_ref[di, h] = caug[di, h, dh:dh + 1, :]
                mn_ref[di, h] = mscr[di, h]


def _mlstm_out_kernel(q_ref, k_ref, v_ref, o_ref, g0_ref, g1_ref, bias_ref, norm_ref,
                      cf_ref, mf_ref, cb_ref, mb_ref, y_ref):
    dh = MLSTM_DH
    row, col = _iota2()
    masks = [_dir_mask(di, row, col) for di in range(N_DIR)]
    masks_t = [masks[1 - di] for di in range(N_DIR)]
    m01 = [_to01(m) for m in masks]
    mt01 = [_to01(m) for m in masks_t]
    g_refs = (g0_ref, g1_ref)
    c_refs = (cf_ref, cb_ref)
    m_refs = (mf_ref, mb_ref)
    for ci in range(CPS):
        rs = slice(ci * CHUNK, (ci + 1) * CHUNK)
        ps, bcols, brows = [], [], []
        for di in range(N_DIR):
            p = _gate_act(g_refs[di][rs, :] + bias_ref[di])
            ps.append(p)
            bcols.append(_dot01_left(m01[di], p))
            brows.append(_dot01_right(p.T, mt01[di]))
        outs = []
        for h in range(MLSTM_HEADS):
            sl = slice(h * dh, (h + 1) * dh)
            q = q_ref[rs, sl].astype(BF16)
            k = (k_ref[rs, sl] * (dh ** -0.5)).astype(BF16)
            v = v_ref[rs, sl].astype(BF16)
            qk = _bdot_nt(k, q)
            wsum = None
            extra = None
            for di in range(N_DIR):
                fl = GATE_F + h
                a_s = ps[di][:, h:h + 1] - bcols[di][:, fl:fl + 1]
                b_t = brows[di][fl:fl + 1, :]
                mm = m_refs[di][ci, h]
                am = jnp.where(masks_t[di], a_s, -jnp.inf)
                big = jnp.maximum(mm, jnp.max(am, axis=0, keepdims=True))
                sc = qk * jnp.exp(am - big)
                w_inter = jnp.exp(mm - big)
                qc = _bdot_nt(c_refs[di][ci, h], q)
                den = jnp.sum(sc, axis=0, keepdims=True) + w_inter * qc[dh:dh + 1, :]
                r = 1.0 / jnp.maximum(jnp.abs(den), jnp.exp(-(b_t + big)))
                wsum = sc * r if wsum is None else wsum + sc * r
                e = qc[0:dh, :] * (w_inter * r)
                extra = e if extra is None else extra + e
            ht = _bdot_tn(v, wsum) + extra
            ht = ht * lax.rsqrt(jnp.mean(ht * ht, axis=0, keepdims=True) + EPS)
            outs.append(ht.T)
        hs = jnp.concatenate(outs, axis=1) * norm_ref[...]
        y_ref[rs, :] = (jax.nn.sigmoid(o_ref[rs, :]) * hs).astype(BF16)


def _mlstm_mixer(proj, bias, norm, c0, n0, m0, path):
    nc = path.nc
    hh, dh = MLSTM_HEADS, MLSTM_DH
    csnap = jax.ShapeDtypeStruct((path.n_chunks, hh, 2 * dh, dh), BF16)
    msnap = jax.ShapeDtypeStruct((path.n_chunks, hh, 1, LANES), F32)
    seq_in = lambda di: [_seq_row_spec(path, di, GROUP_W, COL_MK), _seq_row_spec(path, di, GROUP_W, COL_MV),
                         _seq_gate_spec(path, di)]
    snap_out = lambda di: [_seq_snap_spec(path, di, (hh, 2 * dh, dh)), _seq_snap_spec(path, di, (hh, 1, LANES))]
    cf, mf, cb, mb, cn, nn, mn = pl.pallas_call(
        functools.partial(_mlstm_state_kernel, nc=nc),
        grid=(path.n_seq, nc),
        in_specs=seq_in(0) + seq_in(1) + [_full_spec((N_DIR, 1, LANES)), _seq_state_spec((hh, dh, dh)),
                                         _seq_state_spec((hh, 1, dh)), _seq_state_spec((hh, 1, LANES))],
        out_specs=tuple(snap_out(0) + snap_out(1) + [_seq_state_spec((hh, dh, dh)),
                                                     _seq_state_spec((hh, 1, dh)),
                                                     _seq_state_spec((hh, 1, LANES))]),
        out_shape=(csnap, msnap, csnap, msnap,
                   jax.ShapeDtypeStruct((path.n_seq, N_DIR, hh, dh, dh), F32),
                   jax.ShapeDtypeStruct((path.n_seq, N_DIR, hh, 1, dh), F32),
                   jax.ShapeDtypeStruct((path.n_seq, N_DIR, hh, 1, LANES), F32)),
        scratch_shapes=[pltpu.VMEM((N_DIR, hh, 2 * dh, dh), F32), pltpu.VMEM((N_DIR, hh, 1, LANES), F32)],
        compiler_params=_cparams(2),
        name="mlstm_state",
    )(proj, proj, proj, proj, proj, proj, bias, c0, n0, m0)
    y = pl.pallas_call(
        _mlstm_out_kernel,
        grid=(path.n_chunks // CPS,),
        in_specs=[_out_row_spec(GROUP_W, COL_MQ), _out_row_spec(GROUP_W, COL_MK),
                  _out_row_spec(GROUP_W, COL_MV), _out_row_spec(GROUP_W, COL_MO),
                  _out_row_spec(LANES, COL_GATE), _out_row_spec(LANES, COL_GATE + 1),
                  _full_spec((N_DIR, 1, LANES)), _full_spec((1, GROUP_W)),
                  _out_snap_spec((hh, 2 * dh, dh)), _out_snap_spec((hh, 1, LANES)),
                  _out_snap_spec((hh, 2 * dh, dh)), _out_snap_spec((hh, 1, LANES))],
        out_specs=_out_row_spec(GROUP_W, 0),
        out_shape=jax.ShapeDtypeStruct((path.rows, GROUP_W), BF16),
        compiler_params=_cparams(1),
        name="mlstm_out",
    )(proj, proj, proj, proj, proj, proj, bias, norm, cf, mf, cb, mb)
    return y, cn, nn, mn


def _rope(a, cos, sin):
    return a * cos + pltpu.roll(a, RET_DH // 2, 1) * sin


def _ret_state_kernel(*refs, nc, rope):
    if rope:
        (kf_ref, vf_ref, kb_ref, vb_ref, lg_ref, s0_ref, cosf_ref, sinf_ref, cosb_ref, sinb_ref,
         sf_ref, sb_ref, sn_ref, sscr) = refs
        tabs = ((cosf_ref, sinf_ref), (cosb_ref, sinb_ref))
    else:
        kf_ref, vf_ref, kb_ref, vb_ref, lg_ref, s0_ref, sf_ref, sb_ref, sn_ref, sscr = refs
    n = pl.program_id(1)
    dh = RET_DH

    @pl.when(n == 0)
    def _():
        sscr[...] = s0_ref[...]

    t = lax.broadcasted_iota(jnp.int32, (CHUNK, dh), 0).astype(F32)
    for di, (k_ref, v_ref, s_out) in enumerate(((kf_ref, vf_ref, sf_ref), (kb_ref, vb_ref, sb_ref))):
        left = CHUNK - 1.0 - t if di == 0 else t
        for h in range(RET_HEADS):
            sl = slice(h * dh, (h + 1) * dh)
            k = k_ref[:, sl]
            if rope:
                k = _rope(k, tabs[di][0][...], tabs[di][1][...])
            k = k * (dh ** -0.5)
            lg = -jnp.exp(lg_ref[di, :, sl])
            st = sscr[di, h]
            s_out[h] = st.astype(BF16)
            sscr[di, h] = jnp.exp(CHUNK * lg) * st + _bdot((k * jnp.exp(left * lg)).T, v_ref[:, sl])

    @pl.when(n == nc - 1)
    def _():
        sn_ref[...] = sscr[...]


def _ret_out_kernel(*refs, rope):
    if rope:
        (q_ref, k_ref, v_ref, g_ref, lg_ref, norm_ref, sf_ref, sb_ref, cos_ref, sin_ref,
         y_ref, dsum, qdec) = refs
    else:
        q_ref, k_ref, v_ref, g_ref, lg_ref, norm_ref, sf_ref, sb_ref, y_ref, dsum, qdec = refs
    dh = RET_DH

    @pl.when(pl.program_id(0) == 0)
    def _():
        row, col = _iota2()
        absdiff = jnp.abs(row - col).astype(F32)
        t = row.astype(F32)
        for h in range(RET_HEADS):
            sl = slice(h * dh, (h + 1) * dh)
            lgf = -jnp.exp(lg_ref[0, :, sl])
            lgb = -jnp.exp(lg_ref[1, :, sl])
            both = jnp.exp(absdiff * jnp.where(row >= col, lgf, lgb))
            dsum[h] = jnp.where(row == col, 2.0, both)
            qdec[0, h] = jnp.exp((t + 1.0) * lgf)
            qdec[1, h] = jnp.exp((CHUNK - t) * lgb)

    s_refs = (sf_ref, sb_ref)
    for ci in range(CPS):
        rs = slice(ci * CHUNK, (ci + 1) * CHUNK)
        outs = []
        for h in range(RET_HEADS):
            sl = slice(h * dh, (h + 1) * dh)
            q = q_ref[rs, sl]
            k = k_ref[rs, sl]
            if rope:
                cos, sin = cos_ref[rs, :], sin_ref[rs, :]
                q = _rope(q, cos, sin)
                k = _rope(k, cos, sin)
            q = q.astype(BF16)
            k = (k * (dh ** -0.5)).astype(BF16)
            y = _bdot(_bdot_nt(q, k) * dsum[h], v_ref[rs, sl])
            for di in range(N_DIR):
                y = y + qdec[di, h] * jnp.dot(q, s_refs[di][ci, h], preferred_element_type=F32)
            outs.append(y)
        ys = _head_norm(jnp.concatenate(outs, axis=1), RET_HEADS, dh) * norm_ref[...]
        gate = g_ref[rs, :]
        y_ref[rs, :] = (ys * (gate * jax.nn.sigmoid(gate))).astype(BF16)


def _rope_tables(seq_len):
    n_rows = seq_len // GRID_W
    rows = jnp.repeat(jnp.arange(n_rows, dtype=F32), GRID_W)
    cols = jnp.tile(jnp.arange(GRID_W, dtype=F32), n_rows)
    quarter = RET_DH // 4
    freqs = ROPE_BASE ** (-jnp.arange(quarter, dtype=F32) / quarter)
    ang = jnp.concatenate([rows[:, None] * freqs, cols[:, None] * freqs], axis=-1)
    cos, sin = jnp.cos(ang), jnp.sin(ang)
    return jnp.concatenate([cos, cos], axis=-1), jnp.concatenate([-sin, sin], axis=-1)


def _ret_mixer(proj, lg, norm, s0, rope_tabs, path):
    nc = path.nc
    hh, dh = RET_HEADS, RET_DH
    snap = jax.ShapeDtypeStruct((path.n_chunks, hh, dh, dh), BF16)
    seq_in = lambda di: [_seq_row_spec(path, di, GROUP_W, COL_RK), _seq_row_spec(path, di, GROUP_W, COL_RV)]
    in_specs = seq_in(0) + seq_in(1) + [_full_spec((N_DIR, 1, GROUP_W)), _seq_state_spec((hh, dh, dh))]
    args = [proj, proj, proj, proj, lg, s0]
    if path.rope:
        for di in range(N_DIR):
            tab = pl.BlockSpec((CHUNK, dh), lambda s, n, di=di: (_chunk_of(di, n, nc), 0))
            in_specs += [tab, tab]
            args += list(rope_tabs)
    sf, sb, sn = pl.pallas_call(
        functools.partial(_ret_state_kernel, nc=nc, rope=path.rope),
        grid=(path.n_seq, nc),
        in_specs=in_specs,
        out_specs=(_seq_snap_spec(path, 0, (hh, dh, dh)), _seq_snap_spec(path, 1, (hh, dh, dh)),
                   _seq_state_spec((hh, dh, dh))),
        out_shape=(snap, snap, jax.ShapeDtypeStruct((path.n_seq, N_DIR, hh, dh, dh), F32)),
        scratch_shapes=[pltpu.VMEM((N_DIR, hh, dh, dh), F32)],
        compiler_params=_cparams(2),
        name="ret_state",
    )(*args)
    in_specs = [_out_row_spec(GROUP_W, COL_RQ), _out_row_spec(GROUP_W, COL_RK),
                _out_row_spec(GROUP_W, COL_RV), _out_row_spec(GROUP_W, COL_RG),
                _full_spec((N_DIR, 1, GROUP_W)), _full_spec((1, GROUP_W)),
                _out_snap_spec((hh, dh, dh)), _out_snap_spec((hh, dh, dh))]
    args = [proj, proj, proj, proj, lg, norm, sf, sb]
    if path.rope:
        per_seq = nc // CPS
        tab = pl.BlockSpec((CPS * CHUNK, dh), lambda i: (i % per_seq, 0))
        in_specs += [tab, tab]
        args += list(rope_tabs)
    y = pl.pallas_call(
        functools.partial(_ret_out_kernel, rope=path.rope),
        grid=(path.n_chunks // CPS,),
        in_specs=in_specs,
        out_specs=_out_row_spec(GROUP_W, 0),
        out_shape=jax.ShapeDtypeStruct((path.rows, GROUP_W), BF16),
        scratch_shapes=[pltpu.VMEM((hh, CHUNK, CHUNK), F32), pltpu.VMEM((N_DIR, hh, CHUNK, dh), F32)],
        compiler_params=_cparams(1),
        name="ret_out",
    )(*args)
    return y, sn


def _conv_kernel(x_ref, xp_ref, xn_ref, cw_ref, cb_ref, o_ref, ext, *, bps, tr):
    ib = pl.program_id(0) % bps
    ext[0:HALO, :] = jnp.where(ib > 0, xp_ref[...], 0.0)
    ext[HALO:HALO + tr, :] = x_ref[...]
    ext[HALO + tr:, :] = jnp.where(ib < bps - 1, xn_ref[...], 0.0)
    pad = (SSD_CONV - 1) // 2
    acc = cb_ref[...] + cw_ref[0:1, :] * ext[HALO - pad:HALO - pad + tr, :]
    for j in range(1, SSD_CONV):
        acc = acc + cw_ref[j:j + 1, :] * ext[HALO - pad + j:HALO - pad + j + tr, :]
    o_ref[...] = acc * jax.nn.sigmoid(acc)


def _ssd_conv(proj, conv_w, conv_b, path, tr=256):
    per = tr // HALO
    n_halo = path.rows // HALO
    bps = path.seq_len // tr
    return pl.pallas_call(
        functools.partial(_conv_kernel, bps=bps, tr=tr),
        grid=(path.rows // tr,),
        in_specs=[pl.BlockSpec((tr, SSD_XBC), lambda i: (i, COL_XBC)),
                  pl.BlockSpec((HALO, SSD_XBC), lambda i: (jnp.maximum(i * per - 1, 0), COL_XBC)),
                  pl.BlockSpec((HALO, SSD_XBC), lambda i: (jnp.minimum((i + 1) * per, n_halo - 1), COL_XBC)),
                  _full_spec((SSD_CONV, SSD_XBC)), _full_spec((1, SSD_XBC))],
        out_specs=pl.BlockSpec((tr, SSD_XBC), lambda i: (i, 0)),
        out_shape=jax.ShapeDtypeStruct((path.rows, SSD_XBC), F32),
        scratch_shapes=[pltpu.VMEM((tr + 2 * HALO, SSD_XBC), F32)],
        compiler_params=_cparams(1),
        name="ssd_conv",
    )(proj, proj, proj, conv_w, conv_b)


def _ssd_rates(g, bias, alog):
    lane = lax.broadcasted_iota(jnp.int32, g.shape, 1)
    dt_lanes = (lane >= GATE_DT) & (lane < GATE_DT + SSD_HEADS)
    dt = jnp.where(dt_lanes, jax.nn.softplus(g + bias), 0.0)
    return dt, dt * -jnp.exp(alog)


def _ssd_state_kernel(xf_ref, bf_ref, gf_ref, xb_ref, bb_ref, gb_ref, bias_ref, alog_ref, h0_ref,
                      hf_ref, hb_ref, hn_ref, hscr, *, nc):
    n = pl.program_id(1)

    @pl.when(n == 0)
    def _():
        for di in range(N_DIR):
            for gi in range(SSD_GROUPS):
                hscr[di, gi] = h0_ref[di, gi * SSD_HPG:(gi + 1) * SSD_HPG].reshape(SSD_GW, SSD_N).T

    row, col = _iota2()
    dirs = ((xf_ref, bf_ref, gf_ref, hf_ref), (xb_ref, bb_ref, gb_ref, hb_ref))
    for di, (x_ref, b_ref, g_ref, h_out) in enumerate(dirs):
        dt, a = _ssd_rates(g_ref[...], bias_ref[di], alog_ref[di])
        ccol = _dot01_left(_to01(_dir_mask(di, row, col)), a)
        ctot = jnp.sum(a, axis=0, keepdims=True)
        for gi in range(SSD_GROUPS):
            hst = hscr[di, gi]
            h_out[gi] = hst.astype(BF16)
            wx_parts, tot_parts = [], []
            for r in range(SSD_HPG):
                hd = gi * SSD_HPG + r
                ln = GATE_DT + hd
                tot = ctot[:, ln:ln + 1]
                w = jnp.exp(tot - ccol[:, ln:ln + 1]) * dt[:, ln:ln + 1]
                wx_parts.append(w * x_ref[:, hd * SSD_P:(hd + 1) * SSD_P])
                tot_parts.append(jnp.broadcast_to(jnp.exp(tot), (1, SSD_P)))
            wx = jnp.concatenate(wx_parts, axis=1)
            bm = b_ref[:, gi * SSD_N:(gi + 1) * SSD_N]
            hscr[di, gi] = jnp.concatenate(tot_parts, axis=1) * hst + _bdot(bm.T, wx)

    @pl.when(n == nc - 1)
    def _():
        for di in range(N_DIR):
            for gi in range(SSD_GROUPS):
                hn_ref[di, gi * SSD_HPG:(gi + 1) * SSD_HPG] = hscr[di, gi].T.reshape(SSD_HPG, SSD_P, SSD_N)


def _ssd_out_kernel(z_ref, xbc_ref, g0_ref, g1_ref, bias_ref, alog_ref, dsk_ref, norm_ref,
                    hf_ref, hb_ref, y_ref):
    row, col = _iota2()
    masks = [_dir_mask(di, row, col) for di in range(N_DIR)]
    m01 = [_to01(m) for m in masks]
    mt01 = [_to01(masks[1 - di]) for di in range(N_DIR)]
    g_refs = (g0_ref, g1_ref)
    h_refs = (hf_ref, hb_ref)
    for ci in range(CPS):
        rs = slice(ci * CHUNK, (ci + 1) * CHUNK)
        dtts, ccols, crows = [], [], []
        for di in range(N_DIR):
            dt, a = _ssd_rates(g_refs[di][rs, :], bias_ref[di], alog_ref[di])
            dtts.append(dt.T)
            ccols.append(_dot01_left(m01[di], a))
            crows.append(_dot01_right(a.T, mt01[di]))
        xs = xbc_ref[rs, 0:GROUP_W]
        y_parts = []
        for gi in range(SSD_GROUPS):
            bm = xbc_ref[rs, GROUP_W + gi * SSD_N:GROUP_W + (gi + 1) * SSD_N].astype(BF16)
            c0 = GROUP_W + (SSD_GROUPS + gi) * SSD_N
            cm = xbc_ref[rs, c0:c0 + SSD_N].astype(BF16)
            cb = _bdot_nt(cm, bm)
            inter = [jnp.dot(cm, h_refs[di][ci, gi], preferred_element_type=F32) for di in range(N_DIR)]
            for r in range(SSD_HPG):
                hd = gi * SSD_HPG + r
                ln = GATE_DT + hd
                hs = slice(r * SSD_P, (r + 1) * SSD_P)
                wmat = None
                y = None
                for di in range(N_DIR):
                    cc = ccols[di][:, ln:ln + 1]
                    decay = jnp.exp(jnp.where(masks[di], cc - crows[di][ln:ln + 1, :], -jnp.inf))
                    wd = decay * dtts[di][ln:ln + 1, :]
                    wmat = wd if wmat is None else wmat + wd
                    e = jnp.exp(cc) * inter[di][:, hs]
                    y = e if y is None else y + e
                y_parts.append(y + _bdot(cb * wmat, xs[:, hd * SSD_P:(hd + 1) * SSD_P]))
        z = z_ref[rs, :]
        yy = (dsk_ref[...] * xs + jnp.concatenate(y_parts, axis=1)) * (z * jax.nn.sigmoid(z))
        y_ref[rs, :] = _rms(yy, norm_ref[...]).astype(BF16)


def _ssd_mixer(proj, bias, alog, conv_w, conv_b, dskip, norm, h0, path):
    nc = path.nc
    xbc = _ssd_conv(proj, conv_w, conv_b, path)
    st_shape = (SSD_HEADS, SSD_P, SSD_N)
    snap_shape = (SSD_GROUPS, SSD_N, SSD_GW)
    snap = jax.ShapeDtypeStruct((path.n_chunks,) + snap_shape, BF16)

    def seq_in(di):
        return [pl.BlockSpec((CHUNK, GROUP_W), lambda s, n: (s * nc + _chunk_of(di, n, nc), 0)),
                pl.BlockSpec((CHUNK, SSD_GROUPS * SSD_N),
                             lambda s, n: (s * nc + _chunk_of(di, n, nc), GROUP_W // (SSD_GROUPS * SSD_N))),
                _seq_gate_spec(path, di)]

    hf, hb, hn = pl.pallas_call(
        functools.partial(_ssd_state_kernel, nc=nc),
        grid=(path.n_seq, nc),
        in_specs=seq_in(0) + seq_in(1) + [_full_spec((N_DIR, 1, LANES)), _full_spec((N_DIR, 1, LANES)),
                                         _seq_state_spec(st_shape)],
        out_specs=(_seq_snap_spec(path, 0, snap_shape), _seq_snap_spec(path, 1, snap_shape),
                   _seq_state_spec(st_shape)),
        out_shape=(snap, snap, jax.ShapeDtypeStruct((path.n_seq, N_DIR) + st_shape, F32)),
        scratch_shapes=[pltpu.VMEM((N_DIR,) + snap_shape, F32)],
        compiler_params=_cparams(2),
        name="ssd_state",
    )(xbc, xbc, proj, xbc, xbc, proj, bias, alog, h0)
    y = pl.pallas_call(
        _ssd_out_kernel,
        grid=(path.n_chunks // CPS,),
        in_specs=[_out_row_spec(GROUP_W, COL_DZ), _out_row_spec(SSD_XBC, 0),
                  _out_row_spec(LANES, COL_GATE), _out_row_spec(LANES, COL_GATE + 1),
                  _full_spec((N_DIR, 1, LANES)), _full_spec((N_DIR, 1, LANES)),
                  _full_spec((1, GROUP_W)), _full_spec((1, GROUP_W)),
                  _out_snap_spec(snap_shape), _out_snap_spec(snap_shape)],
        out_specs=_out_row_spec(GROUP_W, 0),
        out_shape=jax.ShapeDtypeStruct((path.rows, GROUP_W), BF16),
        compiler_params=_cparams(1),
        name="ssd_out",
    )(proj, xbc, proj, proj, bias, alog, dskip, norm, hf, hb)
    return y, hn


def _out_proj_kernel(u_ref, yf_ref, yb_ref, dsk_ref, wglu_ref, yb2_ref, yc_ref, yd_ref, w_ref, x_ref,
                     g1_ref, g_ref, sc_ref, sh_ref, x1_ref, h2_ref):
    t = jax.nn.gelu(dsk_ref[...] * u_ref[...] + yf_ref[...] + yb_ref[...])
    ya = (t * jax.nn.sigmoid(_bdot(t, wglu_ref[...]))).astype(BF16)
    acc = jnp.dot(ya, w_ref[0], preferred_element_type=F32)
    for i, y_ref in enumerate((yb2_ref, yc_ref, yd_ref)):
        acc = acc + jnp.dot(y_ref[...], w_ref[i + 1], preferred_element_type=F32)
    x1 = x_ref[...] + g1_ref[...] * acc
    x1_ref[...] = x1
    h2_ref[...] = (_rms(x1, g_ref[...]) * (1.0 + sc_ref[...]) + sh_ref[...]).astype(BF16)


def _out_proj(proj, s5_y, s5_post, ys, x, mod3, norm2, w_out_p, layer, path, tm=256):
    rows = x.shape[0]
    yspec = pl.BlockSpec((tm, GROUP_W), lambda i: (i, 0))
    xspec = pl.BlockSpec((tm, D_MODEL), lambda i: (i, 0))
    dskip, wglu = s5_post
    return pl.pallas_call(
        _out_proj_kernel,
        grid=(rows // tm,),
        in_specs=[pl.BlockSpec((tm, GROUP_W), lambda i: (i, COL_S5)), yspec, yspec,
                  _full_spec((1, GROUP_W)), _full_spec((GROUP_W, GROUP_W)),
                  yspec, yspec, yspec,
                  pl.BlockSpec((None, 4, GROUP_W, D_MODEL), lambda i: (layer, 0, 0, 0)),
                  xspec,
                  _mod_spec(layer, path, tm, 2),
                  pl.BlockSpec((None, 1, D_MODEL), lambda i: (layer, 0, 0)),
                  _mod_spec(layer, path, tm, 4),
                  _mod_spec(layer, path, tm, 3)],
        out_specs=(xspec, xspec),
        out_shape=(jax.ShapeDtypeStruct((rows, D_MODEL), F32),
                   jax.ShapeDtypeStruct((rows, D_MODEL), BF16)),
        compiler_params=_cparams(1),
        name="out_proj",
    )(proj, s5_y[0], s5_y[1], dskip, wglu, *ys, w_out_p, x, mod3, norm2.reshape(DEPTH, 1, D_MODEL),
      mod3, mod3)


def _mlp_kernel(h2_ref, x1_ref, w1_ref, w2_ref, g2_ref, fn_ref, o_ref, acc, *, nk, final):
    k = pl.program_id(1)
    a = jnp.dot(h2_ref[...], w1_ref[...], preferred_element_type=F32)
    a = jnp.square(jnp.maximum(a, 0.0))
    part = jnp.dot(a.astype(BF16), w2_ref[...], preferred_element_type=F32)

    @pl.when(k == 0)
    def _():
        acc[...] = part

    @pl.when(k > 0)
    def _():
        acc[...] += part

    @pl.when(k == nk - 1)
    def _():
        out = x1_ref[...] + g2_ref[...] * acc[...]
        o_ref[...] = _rms(out, fn_ref[...]) if final else out


def _mlp(h2, x1, mod3, w1, w2, final_norm, layer, path, final, tm=512, tf=1024):
    rows = x1.shape[0]
    nk = D_FF // tf
    xspec = pl.BlockSpec((tm, D_MODEL), lambda i, k: (i, 0))
    return pl.pallas_call(
        functools.partial(_mlp_kernel, nk=nk, final=final),
        grid=(rows // tm, nk),
        in_specs=[xspec, xspec,
                  pl.BlockSpec((None, D_MODEL, tf), lambda i, k: (layer, 0, k)),
                  pl.BlockSpec((None, tf, D_MODEL), lambda i, k: (layer, k, 0)),
                  _mod_spec(layer, path, tm, 5),
                  pl.BlockSpec((1, D_MODEL), lambda i, k: (0, 0))],
        out_specs=xspec,
        out_shape=jax.ShapeDtypeStruct((rows, D_MODEL), F32),
        scratch_shapes=[pltpu.VMEM((tm, D_MODEL), F32)],
        compiler_params=_cparams(2),
        name="mlp",
    )(h2, x1, w1, w2, mod3, final_norm.reshape(1, D_MODEL))


def _permute_w_in(w_in):
    o_mi = 5 * GROUP_W
    o_mf = o_mi + N_DIR * MLSTM_HEADS
    o_r = o_mf + N_DIR * MLSTM_HEADS
    o_dt = D_IN - N_DIR * SSD_HEADS
    zeros = jnp.zeros(w_in.shape[:2] + (LANES - 2 * MLSTM_HEADS - SSD_HEADS,), w_in.dtype)
    parts = [w_in[..., :o_mi], w_in[..., o_r:o_dt]]
    for di in range(N_DIR):
        parts += [w_in[..., o_mi + di * MLSTM_HEADS:o_mi + (di + 1) * MLSTM_HEADS],
                  w_in[..., o_mf + di * MLSTM_HEADS:o_mf + (di + 1) * MLSTM_HEADS],
                  w_in[..., o_dt + di * SSD_HEADS:o_dt + (di + 1) * SSD_HEADS], zeros]
    return jnp.concatenate(parts, axis=-1).astype(BF16)


def _gate_rows(i_part, f_part, dt_part):
    pad = jnp.zeros(i_part.shape[:2] + (LANES - 2 * MLSTM_HEADS - SSD_HEADS,), F32)
    return jnp.concatenate([i_part, f_part, dt_part, pad], axis=-1)[:, :, None, :]


def kernel(x_prompt, x_sample, c, state_s5_re, state_s5_im, state_mlstm_c, state_mlstm_n, state_mlstm_m, state_ret, state_ssd, c_ctx, norm1, w_ada, b_ada, w_in, s5_lambda_re, s5_lambda_im, s5_log_dt, s5_b_re, s5_b_im, s5_c_re, s5_c_im, s5_d, s5_w_glu, mlstm_i_bias, mlstm_f_bias, mlstm_norm, ret_log_rate, ret_norm, ssd_conv_w, ssd_conv_b, ssd_dt_bias, ssd_a_log, ssd_d, ssd_norm, w_out, norm2, w_mlp1, w_mlp2, final_norm):
    bp, seq = x_prompt.shape[0], x_prompt.shape[1]
    bd, dseq = x_sample.shape[0], x_sample.shape[1]
    ctx = _Path(bp, seq, 0, 0, False)
    lat = _Path(bd, dseq, 1, 1, True)

    cvec = jnp.concatenate([c_ctx[None], c, jnp.zeros((8 - 1 - bd, D_MODEL), F32)], axis=0)
    mod3 = _ada(cvec, w_ada, b_ada).reshape(DEPTH * 8 * 6, 1, D_MODEL)

    w_in_p = _permute_w_in(w_in)
    w_out_p = w_out.astype(BF16).reshape(DEPTH, 4, GROUP_W, D_MODEL)
    w1 = w_mlp1.astype(BF16)
    w2 = w_mlp2.astype(BF16)
    wglu = s5_w_glu.astype(BF16)
    zeros4 = jnp.zeros((DEPTH, N_DIR, MLSTM_HEADS), F32)
    gate_bias = _gate_rows(mlstm_i_bias, mlstm_f_bias, ssd_dt_bias)
    alog_rows = _gate_rows(zeros4, zeros4, ssd_a_log)
    ret_lg = jnp.repeat(ret_log_rate, RET_DH, axis=-1)[:, :, None, :]
    ssd_d_row = jnp.repeat(ssd_d, SSD_P, axis=-1)[:, None, :]
    rope_tabs = _rope_tables(dseq)

    def states(path, layer):
        if path is ctx:
            n = path.n_seq
            z = lambda *s: jnp.zeros((n, N_DIR) + s, F32)
            return (z(1, S5_ST), z(1, S5_ST), z(MLSTM_HEADS, MLSTM_DH, MLSTM_DH),
                    z(MLSTM_HEADS, 1, MLSTM_DH), z(MLSTM_HEADS, 1, LANES),
                    z(RET_HEADS, RET_DH, RET_DH), z(SSD_HEADS, SSD_P, SSD_N))
        return (state_s5_re[:, layer].reshape(bd, N_DIR, 1, S5_ST),
                state_s5_im[:, layer].reshape(bd, N_DIR, 1, S5_ST),
                state_mlstm_c[:, layer],
                state_mlstm_n[:, layer][..., None, :],
                jnp.broadcast_to(state_mlstm_m[:, layer][..., None, None],
                                 (bd, N_DIR, MLSTM_HEADS, 1, LANES)),
                state_ret[:, layer], state_ssd[:, layer])

    xs = {ctx: x_prompt.reshape(ctx.rows, D_MODEL), lat: x_sample.reshape(lat.rows, D_MODEL)}
    new_states = []
    for layer in range(DEPTH):
        s5_prm = _s5_params(s5_lambda_re[layer], s5_lambda_im[layer], s5_log_dt[layer],
                            s5_b_re[layer], s5_b_im[layer], s5_c_re[layer], s5_c_im[layer])
        s5_post = (s5_d[layer][None], wglu[layer])
        for path in (ctx, lat):
            x = xs[path]
            st = states(path, layer)
            proj = _in_proj(x, mod3, norm1, w_in_p, layer, path)
            s5f, s5b, s5r, s5i = _s5_mixer(proj, s5_prm, st[0], st[1], path)
            yb, mc, mn, mm = _mlstm_mixer(proj, gate_bias[layer], mlstm_norm[layer][None],
                                          st[2], st[3], st[4], path)
            yc, rs = _ret_mixer(proj, ret_lg[layer], ret_norm[layer][None], st[5], rope_tabs, path)
            yd, dh = _ssd_mixer(proj, gate_bias[layer], alog_rows[layer], ssd_conv_w[layer],
                                ssd_conv_b[layer][None], ssd_d_row[layer], ssd_norm[layer][None],
                                st[6], path)
            x1, h2 = _out_proj(proj, (s5f, s5b), s5_post, (yb, yc, yd), x, mod3, norm2, w_out_p,
                               layer, path)
            xs[path] = _mlp(h2, x1, mod3, w1, w2, final_norm, layer, path, layer == DEPTH - 1)
            if path is ctx:
                new_states.append((s5r.reshape(bp, N_DIR, S5_GROUPS, S5_P),
                                   s5i.reshape(bp, N_DIR, S5_GROUPS, S5_P),
                                   mc, mn[..., 0, :], mm[..., 0, 0], rs, dh))
    y_prompt = xs[ctx].reshape(bp, seq, D_MODEL)
    y_sample = xs[lat].reshape(bd, dseq, D_MODEL)
    stacked = tuple(jnp.stack([s[i] for s in new_states], axis=1) for i in range(7))
    return (y_prompt, y_sample) + stacked
```

```python
import functools
from typing import Callable, NamedTuple

import jax
import jax.numpy as jnp
from jax import lax
from jax.experimental import pallas as pl
from jax.experimental.pallas import tpu as pltpu

D_MODEL = 2048
DEPTH = 4
GRID_W = 64
N_DIR = 2
GROUP_W = D_MODEL // 4
S5_CH = 16
S5_GROUPS = GROUP_W // S5_CH
S5_P = 64
S5_ST = S5_GROUPS * S5_P
S5_CB = 4
S5_BW = S5_ST // S5_CB
S5_SEG = 8
S5_SLEN = 16
MLSTM_HEADS = 4
MLSTM_DH = GROUP_W // MLSTM_HEADS
RET_HEADS = 4
RET_DH = GROUP_W // RET_HEADS
SSD_HEADS = 8
SSD_P = GROUP_W // SSD_HEADS
SSD_GROUPS = 2
SSD_HPG = SSD_HEADS // SSD_GROUPS
SSD_N = 128
SSD_GW = SSD_HPG * SSD_P
SSD_CONV = 5
SSD_XBC = GROUP_W + 2 * SSD_GROUPS * SSD_N
D_FF = 4 * D_MODEL
CHUNK = 128
EPS = 1e-6
ROPE_BASE = 10000.0
D_IN = 10 * GROUP_W + 2 * N_DIR * MLSTM_HEADS + SSD_XBC + N_DIR * SSD_HEADS
LANES = 128
HALO = 8
D_IN_P = 10 * GROUP_W + SSD_XBC + N_DIR * LANES
COL_S5, COL_MQ, COL_MK, COL_MV, COL_MO, COL_RQ, COL_RK, COL_RV, COL_RG, COL_DZ = range(10)
COL_XBC = 10 * GROUP_W // SSD_XBC
COL_GATE = (10 * GROUP_W + SSD_XBC) // LANES
GATE_I, GATE_F, GATE_DT = 0, MLSTM_HEADS, 2 * MLSTM_HEADS
CPS = 2
VMEM_LIMIT = 56 * 1024 * 1024

F32 = jnp.float32
BF16 = jnp.bfloat16


def _bdot(a, b):
    return jnp.dot(a.astype(BF16), b.astype(BF16), preferred_element_type=F32)


def _bdot_nt(a, b):
    return lax.dot_general(a.astype(BF16), b.astype(BF16), (((1,), (1,)), ((), ())),
                           preferred_element_type=F32)


def _bdot_tn(a, b):
    return lax.dot_general(a.astype(BF16), b.astype(BF16), (((0,), (0,)), ((), ())),
                           preferred_element_type=F32)


def _split3(x):
    x1 = x.astype(BF16)
    r1 = x - x1.astype(F32)
    x2 = r1.astype(BF16)
    x3 = (r1 - x2.astype(F32)).astype(BF16)
    return x1, x2, x3


def _dot01_left(m01, x):
    return sum(jnp.dot(m01, p, preferred_element_type=F32) for p in _split3(x))


def _dot01_right(x, m01):
    return sum(jnp.dot(p, m01, preferred_element_type=F32) for p in _split3(x))


def _iota2():
    row = lax.broadcasted_iota(jnp.int32, (CHUNK, CHUNK), 0)
    col = lax.broadcasted_iota(jnp.int32, (CHUNK, CHUNK), 1)
    return row, col


def _dir_mask(di, row, col):
    return row >= col if di == 0 else row <= col


def _to01(mask):
    return jnp.where(mask, 1.0, 0.0).astype(BF16)


def _cparams(n_axes):
    return pltpu.CompilerParams(dimension_semantics=("arbitrary",) * n_axes,
                                vmem_limit_bytes=VMEM_LIMIT)


def _rms(x, g):
    return x * lax.rsqrt(jnp.mean(x * x, axis=-1, keepdims=True) + EPS) * g


def _head_norm(y, heads, dh):
    parts = []
    for h in range(heads):
        seg = y[:, h * dh:(h + 1) * dh]
        parts.append(seg * lax.rsqrt(jnp.mean(seg * seg, axis=-1, keepdims=True) + EPS))
    return jnp.concatenate(parts, axis=1)


class _Path:
    def __init__(self, n_seq, seq_len, mod_base, mod_per_seq, rope):
        self.n_seq, self.seq_len = n_seq, seq_len
        self.mod_base, self.mod_per_seq, self.rope = mod_base, mod_per_seq, rope
        self.rows = n_seq * seq_len
        self.nc = seq_len // CHUNK
        self.n_chunks = self.rows // CHUNK


def _ada_kernel(c_ref, w_ref, b_ref, o_ref):
    c = c_ref[...]
    a = c * jax.nn.sigmoid(c)
    o_ref[...] = _bdot(a, w_ref[...]) + b_ref[...]


def _ada(cvec, w_ada, b_ada, tn=1024):
    n = w_ada.shape[-1]
    return pl.pallas_call(
        _ada_kernel,
        grid=(DEPTH, n // tn),
        in_specs=[pl.BlockSpec((8, D_MODEL), lambda l, j: (0, 0)),
                  pl.BlockSpec((None, D_MODEL, tn), lambda l, j: (l, 0, j)),
                  pl.BlockSpec((None, 1, tn), lambda l, j: (l, 0, j))],
        out_specs=pl.BlockSpec((None, 8, tn), lambda l, j: (l, 0, j)),
        out_shape=jax.ShapeDtypeStruct((DEPTH, 8, n), F32),
        compiler_params=_cparams(2),
        name="ada",
    )(cvec, w_ada, b_ada.reshape(DEPTH, 1, n))


def _mod_spec(layer, path, tm, which):
    def index(i, *_):
        r = path.mod_base + path.mod_per_seq * ((i * tm) // path.seq_len)
        return ((layer * 8 + r) * 6 + which, 0, 0)
    return pl.BlockSpec((None, 1, D_MODEL), index)


def _in_proj_kernel(x_ref, g_ref, sc_ref, sh_ref, w_ref, o_ref, h_scr):
    @pl.when(pl.program_id(1) == 0)
    def _():
        h = _rms(x_ref[...], g_ref[...]) * (1.0 + sc_ref[...]) + sh_ref[...]
        h_scr[...] = h.astype(BF16)
    o_ref[...] = jnp.dot(h_scr[...], w_ref[...], preferred_element_type=F32)


def _in_proj(x, mod3, norm1, w_in_p, layer, path, tm=1024, tn=1280):
    rows = x.shape[0]
    return pl.pallas_call(
        _in_proj_kernel,
        grid=(rows // tm, D_IN_P // tn),
        in_specs=[pl.BlockSpec((tm, D_MODEL), lambda i, j: (i, 0)),
                  pl.BlockSpec((None, 1, D_MODEL), lambda i, j: (layer, 0, 0)),
                  _mod_spec(layer, path, tm, 1),
                  _mod_spec(layer, path, tm, 0),
                  pl.BlockSpec((None, D_MODEL, tn), lambda i, j: (layer, 0, j))],
        out_specs=pl.BlockSpec((tm, tn), lambda i, j: (i, j)),
        out_shape=jax.ShapeDtypeStruct((rows, D_IN_P), F32),
        scratch_shapes=[pltpu.VMEM((tm, D_MODEL), BF16)],
        compiler_params=_cparams(2),
        name="in_proj",
    )(x, norm1.reshape(DEPTH, 1, D_MODEL), mod3, mod3, w_in_p)


def _chunk_of(di, n, nc):
    return n if di == 0 else nc - 1 - n


def _seq_row_spec(path, di, width, col_block):
    nc = path.nc
    return pl.BlockSpec((CHUNK, width), lambda s, n: (s * nc + _chunk_of(di, n, nc), col_block))


def _seq_gate_spec(path, di):
    return _seq_row_spec(path, di, LANES, COL_GATE + di)


def _seq_snap_spec(path, di, shape):
    nc = path.nc
    zeros = (0,) * len(shape)
    return pl.BlockSpec((None,) + shape, lambda s, n: (s * nc + _chunk_of(di, n, nc),) + zeros)


def _seq_state_spec(shape):
    zeros = (0,) * (len(shape) + 1)
    return pl.BlockSpec((None, N_DIR) + shape, lambda s, n: (s,) + zeros)


def _full_spec(shape, n_axes=2):
    zeros = (0,) * len(shape)
    return pl.BlockSpec(shape, lambda *_: zeros)


def _out_row_spec(width, col_block):
    return pl.BlockSpec((CPS * CHUNK, width), lambda i: (i, col_block))


def _out_snap_spec(shape):
    zeros = (0,) * len(shape)
    return pl.BlockSpec((CPS,) + shape, lambda i: (i,) + zeros)


class _Part(NamedTuple):
    kernel: Callable
    in_specs: list
    out_specs: list
    out_shape: list
    scratch: list
    args: list


def _fused_kernel(*refs, layout):
    n_in = sum(ni for _, ni, _, _ in layout)
    n_out = sum(no for _, _, no, _ in layout)
    ins, outs, scr = refs[:n_in], refs[n_in:n_in + n_out], refs[n_in + n_out:]
    i = o = s = 0
    for body, ni, no, ns in layout:
        body(*ins[i:i + ni], *outs[o:o + no], *scr[s:s + ns])
        i, o, s = i + ni, o + no, s + ns


def _fused_call(parts, grid, name):
    layout = tuple((p.kernel, len(p.in_specs), len(p.out_specs), len(p.scratch)) for p in parts)
    outs = pl.pallas_call(
        functools.partial(_fused_kernel, layout=layout),
        grid=grid,
        in_specs=[s for p in parts for s in p.in_specs],
        out_specs=tuple(s for p in parts for s in p.out_specs),
        out_shape=tuple(s for p in parts for s in p.out_shape),
        scratch_shapes=[s for p in parts for s in p.scratch],
        compiler_params=_cparams(len(grid)),
        name=name,
    )(*[a for p in parts for a in p.args])
    split, o = [], 0
    for p in parts:
        split.append(outs[o:o + len(p.out_specs)])
        o += len(p.out_specs)
    return split


def _s5_kernel(uf_ref, ub_ref, bre_ref, bim_ref, cre_ref, cim_ref, pwr_ref, pwi_ref, h0r_ref, h0i_ref,
               yf_ref, yb_ref, hnr_ref, hni_ref,
               ut, yt, hlr, hli, st_r, st_i, *, nc):
    n = pl.program_id(1)

    @pl.when(n == 0)
    def _():
        st_r[...] = jnp.broadcast_to(h0r_ref[...], st_r.shape)
        st_i[...] = jnp.broadcast_to(h0i_ref[...], st_i.shape)

    u_refs = (uf_ref, ub_ref)
    y_refs = (yf_ref, yb_ref)
    dirs = range(N_DIR)
    seg = lax.broadcasted_iota(jnp.int32, (S5_SEG, S5_BW), 0)
    pos = (seg, S5_SEG - 1 - seg)

    def step_rows(di, kk):
        k = kk if di == 0 else S5_SLEN - 1 - kk
        return slice(k * S5_SEG, (k + 1) * S5_SEG)

    for cb in range(S5_CB):
        lanes = slice(cb * S5_BW, (cb + 1) * S5_BW)
        for di in dirs:
            ut[di, cb] = u_refs[di][:, cb * LANES:(cb + 1) * LANES]
            ucb = jnp.concatenate([ut[di, cb, pl.ds(k, S5_SEG, stride=S5_SLEN), :] for k in range(S5_SLEN)],
                                  axis=0).astype(BF16)
            hlr[di, cb] = jnp.dot(ucb, bre_ref[di, cb], preferred_element_type=F32)
            hli[di, cb] = jnp.dot(ucb, bim_ref[di, cb], preferred_element_type=F32)
        lr = [pwr_ref[di, 0, :, lanes] for di in dirs]
        li = [pwi_ref[di, 0, :, lanes] for di in dirs]
        hr = [jnp.zeros((S5_SEG, S5_BW), F32) for di in dirs]
        hi = [jnp.zeros((S5_SEG, S5_BW), F32) for di in dirs]
        for kk in range(S5_SLEN):
            for di in dirs:
                rows = step_rows(di, kk)
                hr[di], hi[di] = (lr[di] * hr[di] - li[di] * hi[di] + hlr[di, cb, rows, :],
                                  lr[di] * hi[di] + li[di] * hr[di] + hli[di, cb, rows, :])
                hlr[di, cb, rows, :] = hr[di]
                hli[di, cb, rows, :] = hi[di]
        cr, ci = [], []
        for di in dirs:
            one = 1 if di == 0 else S5_SEG - 1
            first = pos[di] == 0
            cr.append(jnp.where(first, pltpu.roll(st_r[di, :, lanes], one, 0), pltpu.roll(hr[di], one, 0)))
            ci.append(jnp.where(first, pltpu.roll(st_i[di, :, lanes], one, 0), pltpu.roll(hi[di], one, 0)))
        for lvl in range(S5_SEG.bit_length() - 1):
            dist = 1 << lvl
            for di in dirs:
                mr = pwr_ref[di, S5_SLEN - 1 + lvl, :, lanes]
                mi = pwi_ref[di, S5_SLEN - 1 + lvl, :, lanes]
                shift = dist if di == 0 else S5_SEG - dist
                sr = jnp.where(pos[di] >= dist, pltpu.roll(cr[di], shift, 0), 0.0)
                si = jnp.where(pos[di] >= dist, pltpu.roll(ci[di], shift, 0), 0.0)
                cr[di], ci[di] = cr[di] + mr * sr - mi * si, ci[di] + mr * si + mi * sr
        for di in dirs:
            l16r = pwr_ref[di, S5_SLEN - 1, :, lanes]
            l16i = pwi_ref[di, S5_SLEN - 1, :, lanes]
            st_r[di, :, lanes] = hr[di] + l16r * cr[di] - l16i * ci[di]
            st_i[di, :, lanes] = hi[di] + l16r * ci[di] + l16i * cr[di]
        for kk in range(S5_SLEN):
            for di in dirs:
                pr = pwr_ref[di, kk, :, lanes]
                pi = pwi_ref[di, kk, :, lanes]
                rows = step_rows(di, kk)
                hlr[di, cb, rows, :] = hlr[di, cb, rows, :] + pr * cr[di] - pi * ci[di]
                hli[di, cb, rows, :] = hli[di, cb, rows, :] + pr * ci[di] + pi * cr[di]
        for di in dirs:
            yt[di, cb] = (jnp.dot(hlr[di, cb].astype(BF16), cre_ref[di, cb], preferred_element_type=F32)
                          + jnp.dot(hli[di, cb].astype(BF16), cim_ref[di, cb], preferred_element_type=F32))
            per = S5_SLEN // S5_SEG
            for r in range(CHUNK // S5_SEG):
                start = (r % per) * S5_SEG * S5_SEG + r // per
                y_refs[di][r * S5_SEG:(r + 1) * S5_SEG, cb * LANES:(cb + 1) * LANES] = (
                    yt[di, cb, pl.ds(start, S5_SEG, stride=S5_SEG), :])

    @pl.when(n == nc - 1)
    def _():
        for di in range(N_DIR):
            last = S5_SEG - 1 if di == 0 else 0
            hnr_ref[di] = st_r[di, last:last + 1, :]
            hni_ref[di] = st_i[di, last:last + 1, :]


def _s5_params(lam_re, lam_im, log_dt, b_re, b_im, c_re, c_im):
    dt = jnp.exp(log_dt)[..., None]
    mag = jnp.exp(lam_re * dt)
    lbr, lbi = mag * jnp.cos(lam_im * dt), mag * jnp.sin(lam_im * dt)
    den = lam_re * lam_re + lam_im * lam_im
    fr = ((lbr - 1.0) * lam_re + lbi * lam_im) / den
    fi = (lbi * lam_re - (lbr - 1.0) * lam_im) / den
    bbr = fr[..., None] * b_re - fi[..., None] * b_im
    bbi = fr[..., None] * b_im + fi[..., None] * b_re
    gpb = S5_GROUPS // S5_CB
    eye = jnp.eye(gpb, dtype=F32)

    def pack_b(b):
        b = b.reshape(N_DIR, S5_CB, gpb, S5_P, S5_CH)
        m = jnp.einsum('dbgpc,gh->dbgchp', b, eye)
        return m.reshape(N_DIR, S5_CB, gpb * S5_CH, gpb * S5_P).astype(BF16)

    def pack_c(c):
        c = c.reshape(N_DIR, S5_CB, gpb, S5_CH, S5_P)
        m = jnp.einsum('dbgcp,gh->dbgphc', c, eye)
        return m.reshape(N_DIR, S5_CB, gpb * S5_P, gpb * S5_CH).astype(BF16)

    pr, pi = [lbr], [lbi]
    for _ in range(S5_SLEN - 1):
        pr, pi = pr + [pr[-1] * lbr - pi[-1] * lbi], pi + [pr[-1] * lbi + pi[-1] * lbr]
    for _ in range(S5_SEG.bit_length() - 2):
        pr, pi = pr + [pr[-1] * pr[-1] - pi[-1] * pi[-1]], pi + [2.0 * pr[-1] * pi[-1]]
    tab = lambda p: jnp.broadcast_to(jnp.stack(p, axis=1).reshape(N_DIR, len(p), 1, S5_ST),
                                     (N_DIR, len(p), S5_SEG, S5_ST))
    return pack_b(bbr), pack_b(bbi), pack_c(c_re), pack_c(-c_im), tab(pr), tab(pi)


def _s5_part(proj, prm, h0r, h0i, path):
    nc = path.nc
    bre, bim, cre, cim, pwr, pwi = prm
    yshape = jax.ShapeDtypeStruct((path.rows, GROUP_W), F32)
    stshape = jax.ShapeDtypeStruct((path.n_seq, N_DIR, 1, S5_ST), F32)
    yspec = lambda di: pl.BlockSpec((CHUNK, GROUP_W), lambda s, n: (s * nc + _chunk_of(di, n, nc), 0))
    dir_scr = lambda rows: pltpu.VMEM((N_DIR, S5_CB, rows, S5_BW), F32)
    return _Part(
        functools.partial(_s5_kernel, nc=nc),
        in_specs=[_seq_row_spec(path, 0, GROUP_W, COL_S5), _seq_row_spec(path, 1, GROUP_W, COL_S5),
                  _full_spec(bre.shape), _full_spec(bim.shape), _full_spec(cre.shape),
                  _full_spec(cim.shape), _full_spec(pwr.shape), _full_spec(pwi.shape),
                  _seq_state_spec((1, S5_ST)), _seq_state_spec((1, S5_ST))],
        out_specs=[yspec(0), yspec(1), _seq_state_spec((1, S5_ST)), _seq_state_spec((1, S5_ST))],
        out_shape=[yshape, yshape, stshape, stshape],
        scratch=[pltpu.VMEM((N_DIR, S5_CB, CHUNK, LANES), F32)] * 2
                + [dir_scr(CHUNK)] * 2 + [pltpu.VMEM((N_DIR, S5_SEG, S5_ST), F32)] * 2,
        args=[proj, proj, bre, bim, cre, cim, pwr, pwi, h0r, h0i])


def _gate_act(g):
    lane = lax.broadcasted_iota(jnp.int32, g.shape, 1)
    return jnp.where(lane < GATE_F, g, jnp.where(lane < GATE_DT, jax.nn.log_sigmoid(g), 0.0))


def _mlstm_state_kernel(kf_ref, vf_ref, gf_ref, kb_ref, vb_ref, gb_ref, bias_ref, c0_ref, n0_ref, m0_ref,
                        cf_ref, mf_ref, cb_ref, mb_ref, cn_ref, nn_ref, mn_ref, caug, mscr, *, nc):
    n = pl.program_id(1)
    dh = MLSTM_DH

    @pl.when(n == 0)
    def _():
        for di in range(N_DIR):
            for h in range(MLSTM_HEADS):
                caug[di, h, 0:dh, :] = c0_ref[di, h].T
                caug[di, h, dh:2 * dh, :] = jnp.broadcast_to(n0_ref[di, h], (dh, dh))
                mscr[di, h] = m0_ref[di, h]

    row, col = _iota2()
    ones = jnp.ones((CHUNK, dh), F32)
    dirs = ((kf_ref, vf_ref, gf_ref, cf_ref, mf_ref), (kb_ref, vb_ref, gb_ref, cb_ref, mb_ref))
    for di, (k_ref, v_ref, g_ref, c_out, m_out) in enumerate(dirs):
        p = _gate_act(g_ref[...] + bias_ref[di])
        bcol = _dot01_left(_to01(_dir_mask(di, row, col)), p)
        btot = jnp.sum(p, axis=0, keepdims=True)
        for h in range(MLSTM_HEADS):
            sl = slice(h * dh, (h + 1) * dh)
            k = k_ref[:, sl] * (dh ** -0.5)
            v = v_ref[:, sl]
            bc = bcol[:, GATE_F + h:GATE_F + h + 1]
            bt = btot[:, GATE_F + h:GATE_F + h + 1]
            igc = p[:, h:h + 1]
            mrow = mscr[di, h]
            mm = mrow[:, 0:1]
            cst = caug[di, h]
            c_out[h] = cst.astype(BF16)
            m_out[h] = mrow
            gcol = bt - bc + igc
            m_new = jnp.maximum(bt + mm, jnp.max(gcol, axis=0, keepdims=True))
            w_s = jnp.exp(gcol - m_new)
            decay_old = jnp.exp(bt + mm - m_new)
            vaug = jnp.concatenate([v, ones], axis=1)
            caug[di, h] = decay_old * cst + _bdot_tn(vaug, w_s * k)
            mscr[di, h] = jnp.broadcast_to(m_new, (1, LANES))

    @pl.when(n == nc - 1)
    def _():
        for di in range(N_DIR):
            for h in range(MLSTM_HEADS):
                cn_ref[di, h] = caug[di, h, 0:dh, :].T
                nn_ref[di, h] = caug[di, h, dh:dh + 1, :]
                mn_ref[di, h] = mscr[di, h]


def _mlstm_out_kernel(q_ref, k_ref, v_ref, o_ref, g0_ref, g1_ref, bias_ref, norm_ref,
                      cf_ref, mf_ref, cb_ref, mb_ref, y_ref):
    dh = MLSTM_DH
    row, col = _iota2()
    masks = [_dir_mask(di, row, col) for di in range(N_DIR)]
    masks_t = [masks[1 - di] for di in range(N_DIR)]
    m01 = [_to01(m) for m in masks]
    mt01 = [_to01(m) for m in masks_t]
    g_refs = (g0_ref, g1_ref)
    c_refs = (cf_ref, cb_ref)
    m_refs = (mf_ref, mb_ref)
    for ci in range(CPS):
        rs = slice(ci * CHUNK, (ci + 1) * CHUNK)
        ps, bcols, brows = [], [], []
        for di in range(N_DIR):
            p = _gate_act(g_refs[di][rs, :] + bias_ref[di])
            ps.append(p)
            bcols.append(_dot01_left(m01[di], p))
            brows.append(_dot01_right(p.T, mt01[di]))
        outs = []
        for h in range(MLSTM_HEADS):
            sl = slice(h * dh, (h + 1) * dh)
            q = q_ref[rs, sl].astype(BF16)
            k = (k_ref[rs, sl] * (dh ** -0.5)).astype(BF16)
            v = v_ref[rs, sl].astype(BF16)
            qk = _bdot_nt(k, q)
            wsum = None
            extra = None
            for di in range(N_DIR):
                fl = GATE_F + h
                a_s = ps[di][:, h:h + 1] - bcols[di][:, fl:fl + 1]
                b_t = brows[di][fl:fl + 1, :]
                mm = m_refs[di][ci, h]
                am = jnp.where(masks_t[di], a_s, -jnp.inf)
                big = jnp.maximum(mm, jnp.max(am, axis=0, keepdims=True))
                sc = qk * jnp.exp(am - big)
                w_inter = jnp.exp(mm - big)
                qc = _bdot_nt(c_refs[di][ci, h], q)
                den = jnp.sum(sc, axis=0, keepdims=True) + w_inter * qc[dh:dh + 1, :]
                r = 1.0 / jnp.maximum(jnp.abs(den), jnp.exp(-(b_t + big)))
                wsum = sc * r if wsum is None else wsum + sc * r
                e = qc[0:dh, :] * (w_inter * r)
                extra = e if extra is None else extra + e
            ht = _bdot_tn(v, wsum) + extra
            ht = ht * lax.rsqrt(jnp.mean(ht * ht, axis=0, keepdims=True) + EPS)
            outs.append(ht.T)
        hs = jnp.concatenate(outs, axis=1) * norm_ref[...]
        y_ref[rs, :] = (jax.nn.sigmoid(o_ref[rs, :]) * hs).astype(BF16)


def _mlstm_state_part(proj, bias, c0, n0, m0, path):
    nc = path.nc
    hh, dh = MLSTM_HEADS, MLSTM_DH
    csnap = jax.ShapeDtypeStruct((path.n_chunks, hh, 2 * dh, dh), BF16)
    msnap = jax.ShapeDtypeStruct((path.n_chunks, hh, 1, LANES), F32)
    seq_in = lambda di: [_seq_row_spec(path, di, GROUP_W, COL_MK), _seq_row_spec(path, di, GROUP_W, COL_MV),
                         _seq_gate_spec(path, di)]
    snap_out = lambda di: [_seq_snap_spec(path, di, (hh, 2 * dh, dh)), _seq_snap_spec(path, di, (hh, 1, LANES))]
    return _Part(
        functools.partial(_mlstm_state_kernel, nc=nc),
        in_specs=seq_in(0) + seq_in(1) + [_full_spec((N_DIR, 1, LANES)), _seq_state_spec((hh, dh, dh)),
                                         _seq_state_spec((hh, 1, dh)), _seq_state_spec((hh, 1, LANES))],
        out_specs=snap_out(0) + snap_out(1) + [_seq_state_spec((hh, dh, dh)), _seq_state_spec((hh, 1, dh)),
                                               _seq_state_spec((hh, 1, LANES))],
        out_shape=[csnap, msnap, csnap, msnap,
                   jax.ShapeDtypeStruct((path.n_seq, N_DIR, hh, dh, dh), F32),
                   jax.ShapeDtypeStruct((path.n_seq, N_DIR, hh, 1, dh), F32),
                   jax.ShapeDtypeStruct((path.n_seq, N_DIR, hh, 1, LANES), F32)],
        scratch=[pltpu.VMEM((N_DIR, hh, 2 * dh, dh), F32), pltpu.VMEM((N_DIR, hh, 1, LANES), F32)],
        args=[proj, proj, proj, proj, proj, proj, bias, c0, n0, m0])


def _mlstm_out_part(proj, bias, norm, snaps, path):
    hh, dh = MLSTM_HEADS, MLSTM_DH
    return _Part(
        _mlstm_out_kernel,
        in_specs=[_out_row_spec(GROUP_W, COL_MQ), _out_row_spec(GROUP_W, COL_MK),
                  _out_row_spec(GROUP_W, COL_MV), _out_row_spec(GROUP_W, COL_MO),
                  _out_row_spec(LANES, COL_GATE), _out_row_spec(LANES, COL_GATE + 1),
                  _full_spec((N_DIR, 1, LANES)), _full_spec((1, GROUP_W)),
                  _out_snap_spec((hh, 2 * dh, dh)), _out_snap_spec((hh, 1, LANES)),
                  _out_snap_spec((hh, 2 * dh, dh)), _out_snap_spec((hh, 1, LANES))],
        out_specs=[_out_row_spec(GROUP_W, 0)],
        out_shape=[jax.ShapeDtypeStruct((path.rows, GROUP_W), BF16)],
        scratch=[],
        args=[proj, proj, proj, proj, proj, proj, bias, norm] + list(snaps))


def _rope(a, cos, sin):
    return a * cos + pltpu.roll(a, RET_DH // 2, 1) * sin


def _ret_state_kernel(*refs, nc, rope):
    if rope:
        (kf_ref, vf_ref, kb_ref, vb_ref, lg_ref, s0_ref, cosf_ref, sinf_ref, cosb_ref, sinb_ref,
         sf_ref, sb_ref, sn_ref, sscr) = refs
        tabs = ((cosf_ref, sinf_ref), (cosb_ref, sinb_ref))
    else:
        kf_ref, vf_ref, kb_ref, vb_ref, lg_ref, s0_ref, sf_ref, sb_ref, sn_ref, sscr = refs
    n = pl.program_id(1)
    dh = RET_DH

    @pl.when(n == 0)
    def _():
        sscr[...] = s0_ref[...]

    t = lax.broadcasted_iota(jnp.int32, (CHUNK, dh), 0).astype(F32)
    for di, (k_ref, v_ref, s_out) in enumerate(((kf_ref, vf_ref, sf_ref), (kb_ref, vb_ref, sb_ref))):
        left = CHUNK - 1.0 - t if di == 0 else t
        for h in range(RET_HEADS):
            sl = slice(h * dh, (h + 1) * dh)
            k = k_ref[:, sl]
            if rope:
                k = _rope(k, tabs[di][0][...], tabs[di][1][...])
            k = k * (dh ** -0.5)
            lg = -jnp.exp(lg_ref[di, :, sl])
            st = sscr[di, h]
            s_out[h] = st.astype(BF16)
            sscr[di, h] = jnp.exp(CHUNK * lg) * st + _bdot((k * jnp.exp(left * lg)).T, v_ref[:, sl])

    @pl.when(n == nc - 1)
    def _():
        sn_ref[...] = sscr[...]


def _ret_out_kernel(*refs, rope):
    if rope:
        (q_ref, k_ref, v_ref, g_ref, lg_ref, norm_ref, sf_ref, sb_ref, cos_ref, sin_ref,
         y_ref, dsum, qdec) = refs
    else:
        q_ref, k_ref, v_ref, g_ref, lg_ref, norm_ref, sf_ref, sb_ref, y_ref, dsum, qdec = refs
    dh = RET_DH

    @pl.when(pl.program_id(0) == 0)
    def _():
        row, col = _iota2()
        absdiff = jnp.abs(row - col).astype(F32)
        t = row.astype(F32)
        for h in range(RET_HEADS):
            sl = slice(h * dh, (h + 1) * dh)
            lgf = -jnp.exp(lg_ref[0, :, sl])
            lgb = -jnp.exp(lg_ref[1, :, sl])
            both = jnp.exp(absdiff * jnp.where(row >= col, lgf, lgb))
            dsum[h] = jnp.where(row == col, 2.0, both)
            qdec[0, h] = jnp.exp((t + 1.0) * lgf)
            qdec[1, h] = jnp.exp((CHUNK - t) * lgb)

    s_refs = (sf_ref, sb_ref)
    for ci in range(CPS):
        rs = slice(ci * CHUNK, (ci + 1) * CHUNK)
        outs = []
        for h in range(RET_HEADS):
            sl = slice(h * dh, (h + 1) * dh)
            q = q_ref[rs, sl]
            k = k_ref[rs, sl]
            if rope:
                cos, sin = cos_ref[rs, :], sin_ref[rs, :]
                q = _rope(q, cos, sin)
                k = _rope(k, cos, sin)
            q = q.astype(BF16)
            k = (k * (dh ** -0.5)).astype(BF16)
            y = _bdot(_bdot_nt(q, k) * dsum[h], v_ref[rs, sl])
            for di in range(N_DIR):
                y = y + qdec[di, h] * jnp.dot(q, s_refs[di][ci, h], preferred_element_type=F32)
            outs.append(y)
        ys = _head_norm(jnp.concatenate(outs, axis=1), RET_HEADS, dh) * norm_ref[...]
        gate = g_ref[rs, :]
        y_ref[rs, :] = (ys * (gate * jax.nn.sigmoid(gate))).astype(BF16)


def _rope_tables(seq_len):
    n_rows = seq_len // GRID_W
    rows = jnp.repeat(jnp.arange(n_rows, dtype=F32), GRID_W)
    cols = jnp.tile(jnp.arange(GRID_W, dtype=F32), n_rows)
    quarter = RET_DH // 4
    freqs = ROPE_BASE ** (-jnp.arange(quarter, dtype=F32) / quarter)
    ang = jnp.concatenate([rows[:, None] * freqs, cols[:, None] * freqs], axis=-1)
    cos, sin = jnp.cos(ang), jnp.sin(ang)
    return jnp.concatenate([cos, cos], axis=-1), jnp.concatenate([-sin, sin], axis=-1)


def _ret_state_part(proj, lg, s0, rope_tabs, path):
    nc = path.nc
    hh, dh = RET_HEADS, RET_DH
    snap = jax.ShapeDtypeStruct((path.n_chunks, hh, dh, dh), BF16)
    seq_in = lambda di: [_seq_row_spec(path, di, GROUP_W, COL_RK), _seq_row_spec(path, di, GROUP_W, COL_RV)]
    in_specs = seq_in(0) + seq_in(1) + [_full_spec((N_DIR, 1, GROUP_W)), _seq_state_spec((hh, dh, dh))]
    args = [proj, proj, proj, proj, lg, s0]
    if path.rope:
        for di in range(N_DIR):
            tab = pl.BlockSpec((CHUNK, dh), lambda s, n, di=di: (_chunk_of(di, n, nc), 0))
            in_specs += [tab, tab]
            args += list(rope_tabs)
    return _Part(
        functools.partial(_ret_state_kernel, nc=nc, rope=path.rope),
        in_specs=in_specs,
        out_specs=[_seq_snap_spec(path, 0, (hh, dh, dh)), _seq_snap_spec(path, 1, (hh, dh, dh)),
                   _seq_state_spec((hh, dh, dh))],
        out_shape=[snap, snap, jax.ShapeDtypeStruct((path.n_seq, N_DIR, hh, dh, dh), F32)],
        scratch=[pltpu.VMEM((N_DIR, hh, dh, dh), F32)],
        args=args)


def _ret_out_part(proj, lg, norm, snaps, rope_tabs, path):
    hh, dh = RET_HEADS, RET_DH
    in_specs = [_out_row_spec(GROUP_W, COL_RQ), _out_row_spec(GROUP_W, COL_RK),
                _out_row_spec(GROUP_W, COL_RV), _out_row_spec(GROUP_W, COL_RG),
                _full_spec((N_DIR, 1, GROUP_W)), _full_spec((1, GROUP_W)),
                _out_snap_spec((hh, dh, dh)), _out_snap_spec((hh, dh, dh))]
    args = [proj, proj, proj, proj, lg, norm] + list(snaps)
    if path.rope:
        per_seq = path.nc // CPS
        tab = pl.BlockSpec((CPS * CHUNK, dh), lambda i: (i % per_seq, 0))
        in_specs += [tab, tab]
        args += list(rope_tabs)
    return _Part(
        functools.partial(_ret_out_kernel, rope=path.rope),
        in_specs=in_specs,
        out_specs=[_out_row_spec(GROUP_W, 0)],
        out_shape=[jax.ShapeDtypeStruct((path.rows, GROUP_W), BF16)],
        scratch=[pltpu.VMEM((hh, CHUNK, CHUNK), F32), pltpu.VMEM((N_DIR, hh, CHUNK, dh), F32)],
        args=args)


def _conv_kernel(x_ref, xp_ref, xn_ref, cw_ref, cb_ref, o_ref, ext, *, bps, tr):
    ib = pl.program_id(0) % bps
    ext[0:HALO, :] = jnp.where(ib > 0, xp_ref[...], 0.0)
    ext[HALO:HALO + tr, :] = x_ref[...]
    ext[HALO + tr:, :] = jnp.where(ib < bps - 1, xn_ref[...], 0.0)
    pad = (SSD_CONV - 1) // 2
    acc = cb_ref[...] + cw_ref[0:1, :] * ext[HALO - pad:HALO - pad + tr, :]
    for j in range(1, SSD_CONV):
        acc = acc + cw_ref[j:j + 1, :] * ext[HALO - pad + j:HALO - pad + j + tr, :]
    o_ref[...] = acc * jax.nn.sigmoid(acc)


def _ssd_conv(proj, conv_w, conv_b, path, tr=256):
    per = tr // HALO
    n_halo = path.rows // HALO
    bps = path.seq_len // tr
    return pl.pallas_call(
        functools.partial(_conv_kernel, bps=bps, tr=tr),
        grid=(path.rows // tr,),
        in_specs=[pl.BlockSpec((tr, SSD_XBC), lambda i: (i, COL_XBC)),
                  pl.BlockSpec((HALO, SSD_XBC), lambda i: (jnp.maximum(i * per - 1, 0), COL_XBC)),
                  pl.BlockSpec((HALO, SSD_XBC), lambda i: (jnp.minimum((i + 1) * per, n_halo - 1), COL_XBC)),
                  _full_spec((SSD_CONV, SSD_XBC)), _full_spec((1, SSD_XBC))],
        out_specs=pl.BlockSpec((tr, SSD_XBC), lambda i: (i, 0)),
        out_shape=jax.ShapeDtypeStruct((path.rows, SSD_XBC), F32),
        scratch_shapes=[pltpu.VMEM((tr + 2 * HALO, SSD_XBC), F32)],
        compiler_params=_cparams(1),
        name="ssd_conv",
    )(proj, proj, proj, conv_w, conv_b)


def _ssd_rates(g, bias, alog):
    lane = lax.broadcasted_iota(jnp.int32, g.shape, 1)
    dt_lanes = (lane >= GATE_DT) & (lane < GATE_DT + SSD_HEADS)
    dt = jnp.where(dt_lanes, jax.nn.softplus(g + bias), 0.0)
    return dt, dt * -jnp.exp(alog)


def _ssd_state_kernel(xf_ref, bf_ref, gf_ref, xb_ref, bb_ref, gb_ref, bias_ref, alog_ref, h0_ref,
                      hf_ref, hb_ref, hn_ref, hscr, *, nc):
    n = pl.program_id(1)

    @pl.when(n == 0)
    def _():
        for di in range(N_DIR):
            for gi in range(SSD_GROUPS):
                hscr[di, gi] = h0_ref[di, gi * SSD_HPG:(gi + 1) * SSD_HPG].reshape(SSD_GW, SSD_N).T

    row, col = _iota2()
    dirs = ((xf_ref, bf_ref, gf_ref, hf_ref), (xb_ref, bb_ref, gb_ref, hb_ref))
    for di, (x_ref, b_ref, g_ref, h_out) in enumerate(dirs):
        dt, a = _ssd_rates(g_ref[...], bias_ref[di], alog_ref[di])
        ccol = _dot01_left(_to01(_dir_mask(di, row, col)), a)
        ctot = jnp.sum(a, axis=0, keepdims=True)
        for gi in range(SSD_GROUPS):
            hst = hscr[di, gi]
            h_out[gi] = hst.astype(BF16)
            wx_parts, tot_parts = [], []
            for r in range(SSD_HPG):
                hd = gi * SSD_HPG + r
                ln = GATE_DT + hd
                tot = ctot[:, ln:ln + 1]
                w = jnp.exp(tot - ccol[:, ln:ln + 1]) * dt[:, ln:ln + 1]
                wx_parts.append(w * x_ref[:, hd * SSD_P:(hd + 1) * SSD_P])
                tot_parts.append(jnp.broadcast_to(jnp.exp(tot), (1, SSD_P)))
            wx = jnp.concatenate(wx_parts, axis=1)
            bm = b_ref[:, gi * SSD_N:(gi + 1) * SSD_N]
            hscr[di, gi] = jnp.concatenate(tot_parts, axis=1) * hst + _bdot(bm.T, wx)

    @pl.when(n == nc - 1)
    def _():
        for di in range(N_DIR):
            for gi in range(SSD_GROUPS):
                hn_ref[di, gi * SSD_HPG:(gi + 1) * SSD_HPG] = hscr[di, gi].T.reshape(SSD_HPG, SSD_P, SSD_N)


def _ssd_out_kernel(z_ref, xbc_ref, g0_ref, g1_ref, bias_ref, alog_ref, dsk_ref, norm_ref,
                    hf_ref, hb_ref, y_ref):
    row, col = _iota2()
    masks = [_dir_mask(di, row, col) for di in range(N_DIR)]
    m01 = [_to01(m) for m in masks]
    mt01 = [_to01(masks[1 - di]) for di in range(N_DIR)]
    g_refs = (g0_ref, g1_ref)
    h_refs = (hf_ref, hb_ref)
    for ci in range(CPS):
        rs = slice(ci * CHUNK, (ci + 1) * CHUNK)
        dtts, ccols, crows = [], [], []
        for di in range(N_DIR):
            dt, a = _ssd_rates(g_refs[di][rs, :], bias_ref[di], alog_ref[di])
            dtts.append(dt.T)
            ccols.append(_dot01_left(m01[di], a))
            crows.append(_dot01_right(a.T, mt01[di]))
        xs = xbc_ref[rs, 0:GROUP_W]
        y_parts = []
        for gi in range(SSD_GROUPS):
            bm = xbc_ref[rs, GROUP_W + gi * SSD_N:GROUP_W + (gi + 1) * SSD_N].astype(BF16)
            c0 = GROUP_W + (SSD_GROUPS + gi) * SSD_N
            cm = xbc_ref[rs, c0:c0 + SSD_N].astype(BF16)
            cb = _bdot_nt(cm, bm)
            inter = [jnp.dot(cm, h_refs[di][ci, gi], preferred_element_type=F32) for di in range(N_DIR)]
            for r in range(SSD_HPG):
                hd = gi * SSD_HPG + r
                ln = GATE_DT + hd
                hs = slice(r * SSD_P, (r + 1) * SSD_P)
                wmat = None
                y = None
                for di in range(N_DIR):
                    cc = ccols[di][:, ln:ln + 1]
                    decay = jnp.exp(jnp.where(masks[di], cc - crows[di][ln:ln + 1, :], -jnp.inf))
                    wd = decay * dtts[di][ln:ln + 1, :]
                    wmat = wd if wmat is None else wmat + wd
                    e = jnp.exp(cc) * inter[di][:, hs]
                    y = e if y is None else y + e
                y_parts.append(y + _bdot(cb * wmat, xs[:, hd * SSD_P:(hd + 1) * SSD_P]))
        z = z_ref[rs, :]
        yy = (dsk_ref[...] * xs + jnp.concatenate(y_parts, axis=1)) * (z * jax.nn.sigmoid(z))
        y_ref[rs, :] = _rms(yy, norm_ref[...]).astype(BF16)


SSD_SNAP = (SSD_GROUPS, SSD_N, SSD_GW)


def _ssd_state_part(proj, xbc, bias, alog, h0, path):
    nc = path.nc
    st_shape = (SSD_HEADS, SSD_P, SSD_N)
    snap_shape = SSD_SNAP
    snap = jax.ShapeDtypeStruct((path.n_chunks,) + snap_shape, BF16)

    def seq_in(di):
        return [pl.BlockSpec((CHUNK, GROUP_W), lambda s, n: (s * nc + _chunk_of(di, n, nc), 0)),
                pl.BlockSpec((CHUNK, SSD_GROUPS * SSD_N),
                             lambda s, n: (s * nc + _chunk_of(di, n, nc), GROUP_W // (SSD_GROUPS * SSD_N))),
                _seq_gate_spec(path, di)]

    return _Part(
        functools.partial(_ssd_state_kernel, nc=nc),
        in_specs=seq_in(0) + seq_in(1) + [_full_spec((N_DIR, 1, LANES)), _full_spec((N_DIR, 1, LANES)),
                                         _seq_state_spec(st_shape)],
        out_specs=[_seq_snap_spec(path, 0, snap_shape), _seq_snap_spec(path, 1, snap_shape),
                   _seq_state_spec(st_shape)],
        out_shape=[snap, snap, jax.ShapeDtypeStruct((path.n_seq, N_DIR) + st_shape, F32)],
        scratch=[pltpu.VMEM((N_DIR,) + snap_shape, F32)],
        args=[xbc, xbc, proj, xbc, xbc, proj, bias, alog, h0])


def _ssd_out_part(proj, xbc, bias, alog, dskip, norm, snaps, path):
    return _Part(
        _ssd_out_kernel,
        in_specs=[_out_row_spec(GROUP_W, COL_DZ), _out_row_spec(SSD_XBC, 0),
                  _out_row_spec(LANES, COL_GATE), _out_row_spec(LANES, COL_GATE + 1),
                  _full_spec((N_DIR, 1, LANES)), _full_spec((N_DIR, 1, LANES)),
                  _full_spec((1, GROUP_W)), _full_spec((1, GROUP_W)),
                  _out_snap_spec(SSD_SNAP), _out_snap_spec(SSD_SNAP)],
        out_specs=[_out_row_spec(GROUP_W, 0)],
        out_shape=[jax.ShapeDtypeStruct((path.rows, GROUP_W), BF16)],
        scratch=[],
        args=[proj, xbc, proj, proj, bias, alog, dskip, norm] + list(snaps))


def _out_proj_kernel(u_ref, yf_ref, yb_ref, dsk_ref, wglu_ref, yb2_ref, yc_ref, yd_ref, w_ref, x_ref,
                     g1_ref, g_ref, sc_ref, sh_ref, x1_ref, h2_ref):
    t = jax.nn.gelu(dsk_ref[...] * u_ref[...] + yf_ref[...] + yb_ref[...])
    ya = (t * jax.nn.sigmoid(_bdot(t, wglu_ref[...]))).astype(BF16)
    acc = jnp.dot(ya, w_ref[0], preferred_element_type=F32)
    for i, y_ref in enumerate((yb2_ref, yc_ref, yd_ref)):
        acc = acc + jnp.dot(y_ref[...], w_ref[i + 1], preferred_element_type=F32)
    x1 = x_ref[...] + g1_ref[...] * acc
    x1_ref[...] = x1
    h2_ref[...] = (_rms(x1, g_ref[...]) * (1.0 + sc_ref[...]) + sh_ref[...]).astype(BF16)


def _out_proj(proj, s5_y, s5_post, ys, x, mod3, norm2, w_out_p, layer, path, tm=512):
    rows = x.shape[0]
    yspec = pl.BlockSpec((tm, GROUP_W), lambda i: (i, 0))
    xspec = pl.BlockSpec((tm, D_MODEL), lambda i: (i, 0))
    dskip, wglu = s5_post
    return pl.pallas_call(
        _out_proj_kernel,
        grid=(rows // tm,),
        in_specs=[pl.BlockSpec((tm, GROUP_W), lambda i: (i, COL_S5)), yspec, yspec,
                  _full_spec((1, GROUP_W)), _full_spec((GROUP_W, GROUP_W)),
                  yspec, yspec, yspec,
                  pl.BlockSpec((None, 4, GROUP_W, D_MODEL), lambda i: (layer, 0, 0, 0)),
                  xspec,
                  _mod_spec(layer, path, tm, 2),
                  pl.BlockSpec((None, 1, D_MODEL), lambda i: (layer, 0, 0)),
                  _mod_spec(layer, path, tm, 4),
                  _mod_spec(layer, path, tm, 3)],
        out_specs=(xspec, xspec),
        out_shape=(jax.ShapeDtypeStruct((rows, D_MODEL), F32),
                   jax.ShapeDtypeStruct((rows, D_MODEL), BF16)),
        compiler_params=_cparams(1),
        name="out_proj",
    )(proj, s5_y[0], s5_y[1], dskip, wglu, *ys, w_out_p, x, mod3, norm2.reshape(DEPTH, 1, D_MODEL),
      mod3, mod3)


def _mlp_kernel(h2_ref, x1_ref, w1_ref, w2_ref, g2_ref, fn_ref, o_ref, acc, *, nk, final):
    k = pl.program_id(1)
    a = jnp.dot(h2_ref[...], w1_ref[...], preferred_element_type=F32)
    a = jnp.square(jnp.maximum(a, 0.0))
    part = jnp.dot(a.astype(BF16), w2_ref[...], preferred_element_type=F32)

    @pl.when(k == 0)
    def _():
        acc[...] = part

    @pl.when(k > 0)
    def _():
        acc[...] += part

    @pl.when(k == nk - 1)
    def _():
        out = x1_ref[...] + g2_ref[...] * acc[...]
        o_ref[...] = _rms(out, fn_ref[...]) if final else out


def _mlp(h2, x1, mod3, w1, w2, final_norm, layer, path, final, tm=512, tf=1024):
    rows = x1.shape[0]
    nk = D_FF // tf
    xspec = pl.BlockSpec((tm, D_MODEL), lambda i, k: (i, 0))
    return pl.pallas_call(
        functools.partial(_mlp_kernel, nk=nk, final=final),
        grid=(rows // tm, nk),
        in_specs=[xspec, xspec,
                  pl.BlockSpec((None, D_MODEL, tf), lambda i, k: (layer, 0, k)),
                  pl.BlockSpec((None, tf, D_MODEL), lambda i, k: (layer, k, 0)),
                  _mod_spec(layer, path, tm, 5),
                  pl.BlockSpec((1, D_MODEL), lambda i, k: (0, 0))],
        out_specs=xspec,
        out_shape=jax.ShapeDtypeStruct((rows, D_MODEL), F32),
        scratch_shapes=[pltpu.VMEM((tm, D_MODEL), F32)],
        compiler_params=_cparams(2),
        name="mlp",
    )(h2, x1, w1, w2, mod3, final_norm.reshape(1, D_MODEL))


def _permute_w_in(w_in):
    o_mi = 5 * GROUP_W
    o_mf = o_mi + N_DIR * MLSTM_HEADS
    o_r = o_mf + N_DIR * MLSTM_HEADS
    o_dt = D_IN - N_DIR * SSD_HEADS
    zeros = jnp.zeros(w_in.shape[:2] + (LANES - 2 * MLSTM_HEADS - SSD_HEADS,), w_in.dtype)
    parts = [w_in[..., :o_mi], w_in[..., o_r:o_dt]]
    for di in range(N_DIR):
        parts += [w_in[..., o_mi + di * MLSTM_HEADS:o_mi + (di + 1) * MLSTM_HEADS],
                  w_in[..., o_mf + di * MLSTM_HEADS:o_mf + (di + 1) * MLSTM_HEADS],
                  w_in[..., o_dt + di * SSD_HEADS:o_dt + (di + 1) * SSD_HEADS], zeros]
    return jnp.concatenate(parts, axis=-1).astype(BF16)


def _gate_rows(i_part, f_part, dt_part):
    pad = jnp.zeros(i_part.shape[:2] + (LANES - 2 * MLSTM_HEADS - SSD_HEADS,), F32)
    return jnp.concatenate([i_part, f_part, dt_part, pad], axis=-1)[:, :, None, :]


def kernel(x_prompt, x_sample, c, state_s5_re, state_s5_im, state_mlstm_c, state_mlstm_n, state_mlstm_m, state_ret, state_ssd, c_ctx, norm1, w_ada, b_ada, w_in, s5_lambda_re, s5_lambda_im, s5_log_dt, s5_b_re, s5_b_im, s5_c_re, s5_c_im, s5_d, s5_w_glu, mlstm_i_bias, mlstm_f_bias, mlstm_norm, ret_log_rate, ret_norm, ssd_conv_w, ssd_conv_b, ssd_dt_bias, ssd_a_log, ssd_d, ssd_norm, w_out, norm2, w_mlp1, w_mlp2, final_norm):
    bp, seq = x_prompt.shape[0], x_prompt.shape[1]
    bd, dseq = x_sample.shape[0], x_sample.shape[1]
    ctx = _Path(bp, seq, 0, 0, False)
    lat = _Path(bd, dseq, 1, 1, True)

    cvec = jnp.concatenate([c_ctx[None], c, jnp.zeros((8 - 1 - bd, D_MODEL), F32)], axis=0)
    mod3 = _ada(cvec, w_ada, b_ada).reshape(DEPTH * 8 * 6, 1, D_MODEL)

    w_in_p = _permute_w_in(w_in)
    w_out_p = w_out.astype(BF16).reshape(DEPTH, 4, GROUP_W, D_MODEL)
    w1 = w_mlp1.astype(BF16)
    w2 = w_mlp2.astype(BF16)
    wglu = s5_w_glu.astype(BF16)
    zeros4 = jnp.zeros((DEPTH, N_DIR, MLSTM_HEADS), F32)
    gate_bias = _gate_rows(mlstm_i_bias, mlstm_f_bias, ssd_dt_bias)
    alog_rows = _gate_rows(zeros4, zeros4, ssd_a_log)
    ret_lg = jnp.repeat(ret_log_rate, RET_DH, axis=-1)[:, :, None, :]
    ssd_d_row = jnp.repeat(ssd_d, SSD_P, axis=-1)[:, None, :]
    rope_tabs = _rope_tables(dseq)

    def states(path, layer):
        if path is ctx:
            n = path.n_seq
            z = lambda *s: jnp.zeros((n, N_DIR) + s, F32)
            return (z(1, S5_ST), z(1, S5_ST), z(MLSTM_HEADS, MLSTM_DH, MLSTM_DH),
                    z(MLSTM_HEADS, 1, MLSTM_DH), z(MLSTM_HEADS, 1, LANES),
                    z(RET_HEADS, RET_DH, RET_DH), z(SSD_HEADS, SSD_P, SSD_N))
        return (state_s5_re[:, layer].reshape(bd, N_DIR, 1, S5_ST),
                state_s5_im[:, layer].reshape(bd, N_DIR, 1, S5_ST),
                state_mlstm_c[:, layer],
                state_mlstm_n[:, layer][..., None, :],
                jnp.broadcast_to(state_mlstm_m[:, layer][..., None, None],
                                 (bd, N_DIR, MLSTM_HEADS, 1, LANES)),
                state_ret[:, layer], state_ssd[:, layer])

    xs = {ctx: x_prompt.reshape(ctx.rows, D_MODEL), lat: x_sample.reshape(lat.rows, D_MODEL)}
    new_states = []
    for layer in range(DEPTH):
        s5_prm = _s5_params(s5_lambda_re[layer], s5_lambda_im[layer], s5_log_dt[layer],
                            s5_b_re[layer], s5_b_im[layer], s5_c_re[layer], s5_c_im[layer])
        s5_post = (s5_d[layer][None], wglu[layer])
        for path in (ctx, lat):
            x = xs[path]
            st = states(path, layer)
            proj = _in_proj(x, mod3, norm1, w_in_p, layer, path)
            xbc = _ssd_conv(proj, ssd_conv_w[layer], ssd_conv_b[layer][None], path)
            seq_out = _fused_call(
                [_s5_part(proj, s5_prm, st[0], st[1], path),
                 _mlstm_state_part(proj, gate_bias[layer], st[2], st[3], st[4], path),
                 _ret_state_part(proj, ret_lg[layer], st[5], rope_tabs, path),
                 _ssd_state_part(proj, xbc, gate_bias[layer], alog_rows[layer], st[6], path)],
                (path.n_seq, path.nc), "mix_seq")
            (s5f, s5b, s5r, s5i), m_out, r_out, d_out = seq_out
            mc, mn, mm = m_out[4:]
            rs, dh = r_out[2], d_out[2]
            (yb,), (yc,), (yd,) = _fused_call(
                [_mlstm_out_part(proj, gate_bias[layer], mlstm_norm[layer][None], m_out[:4], path),
                 _ret_out_part(proj, ret_lg[layer], ret_norm[layer][None], r_out[:2], rope_tabs, path),
                 _ssd_out_part(proj, xbc, gate_bias[layer], alog_rows[layer], ssd_d_row[layer],
                               ssd_norm[layer][None], d_out[:2], path)],
                (path.n_chunks // CPS,), "mix_out")
            x1, h2 = _out_proj(proj, (s5f, s5b), s5_post, (yb, yc, yd), x, mod3, norm2, w_out_p,
                               layer, path)
            xs[path] = _mlp(h2, x1, mod3, w1, w2, final_norm, layer, path, layer == DEPTH - 1)
            if path is ctx:
                new_states.append((s5r.reshape(bp, N_DIR, S5_GROUPS, S5_P),
                                   s5i.reshape(bp, N_DIR, S5_GROUPS, S5_P),
                                   mc, mn[..., 0, :], mm[..., 0, 0], rs, dh))
    y_prompt = xs[ctx].reshape(bp, seq, D_MODEL)
    y_sample = xs[lat].reshape(bd, dseq, D_MODEL)
    stacked = tuple(jnp.stack([s[i] for s in new_states], axis=1) for i in range(7))
    return (y_prompt, y_sample) + stacked
```

```python
import functools
from typing import Callable, NamedTuple

import jax
import jax.numpy as jnp
from jax import lax
from jax.experimental import pallas as pl
from jax.experimental.pallas import tpu as pltpu

D_MODEL = 2048
DEPTH = 4
GRID_W = 64
N_DIR = 2
GROUP_W = D_MODEL // 4
S5_CH = 16
S5_GROUPS = GROUP_W // S5_CH
S5_P = 64
S5_ST = S5_GROUPS * S5_P
S5_CB = 4
S5_BW = S5_ST // S5_CB
S5_SEG = 8
S5_SLEN = 16
S5_ILV = 2
MLSTM_HEADS = 4
MLSTM_DH = GROUP_W // MLSTM_HEADS
RET_HEADS = 4
RET_DH = GROUP_W // RET_HEADS
SSD_HEADS = 8
SSD_P = GROUP_W // SSD_HEADS
SSD_GROUPS = 2
SSD_HPG = SSD_HEADS // SSD_GROUPS
SSD_N = 128
SSD_GW = SSD_HPG * SSD_P
SSD_CONV = 5
SSD_XBC = GROUP_W + 2 * SSD_GROUPS * SSD_N
D_FF = 4 * D_MODEL
CHUNK = 128
EPS = 1e-6
ROPE_BASE = 10000.0
D_IN = 10 * GROUP_W + 2 * N_DIR * MLSTM_HEADS + SSD_XBC + N_DIR * SSD_HEADS
LANES = 128
HALO = 8
D_IN_P = 10 * GROUP_W + SSD_XBC + N_DIR * LANES
COL_S5, COL_MQ, COL_MK, COL_MV, COL_MO, COL_RQ, COL_RK, COL_RV, COL_RG, COL_DZ = range(10)
COL_XBC = 10 * GROUP_W // SSD_XBC
COL_GATE = (10 * GROUP_W + SSD_XBC) // LANES
GATE_I, GATE_F, GATE_DT = 0, MLSTM_HEADS, 2 * MLSTM_HEADS
CPS = 2
CONV_ROWS = 512
VMEM_LIMIT = 56 * 1024 * 1024

F32 = jnp.float32
BF16 = jnp.bfloat16


def _bdot(a, b):
    return jnp.dot(a.astype(BF16), b.astype(BF16), preferred_element_type=F32)


def _bdot_nt(a, b):
    return lax.dot_general(a.astype(BF16), b.astype(BF16), (((1,), (1,)), ((), ())),
                           preferred_element_type=F32)


def _bdot_tn(a, b):
    return lax.dot_general(a.astype(BF16), b.astype(BF16), (((0,), (0,)), ((), ())),
                           preferred_element_type=F32)


def _split3(x):
    x1 = x.astype(BF16)
    r1 = x - x1.astype(F32)
    x2 = r1.astype(BF16)
    x3 = (r1 - x2.astype(F32)).astype(BF16)
    return x1, x2, x3


def _dot01_left(m01, x):
    return sum(jnp.dot(m01, p, preferred_element_type=F32) for p in _split3(x))


def _dot01_right(x, m01):
    return sum(jnp.dot(p, m01, preferred_element_type=F32) for p in _split3(x))


def _iota2():
    row = lax.broadcasted_iota(jnp.int32, (CHUNK, CHUNK), 0)
    col = lax.broadcasted_iota(jnp.int32, (CHUNK, CHUNK), 1)
    return row, col


def _dir_mask(di, row, col):
    return row >= col if di == 0 else row <= col


def _to01(mask):
    return jnp.where(mask, 1.0, 0.0).astype(BF16)


def _cparams(n_axes):
    return pltpu.CompilerParams(dimension_semantics=("arbitrary",) * n_axes,
                                vmem_limit_bytes=VMEM_LIMIT)


def _rms(x, g):
    return x * lax.rsqrt(jnp.mean(x * x, axis=-1, keepdims=True) + EPS) * g


def _head_norm(y, heads, dh):
    parts = []
    for h in range(heads):
        seg = y[:, h * dh:(h + 1) * dh]
        parts.append(seg * lax.rsqrt(jnp.mean(seg * seg, axis=-1, keepdims=True) + EPS))
    return jnp.concatenate(parts, axis=1)


class _Path:
    def __init__(self, n_seq, seq_len, mod_base, mod_per_seq, rope, state_layers=1):
        self.n_seq, self.seq_len, self.state_layers = n_seq, seq_len, state_layers
        self.mod_base, self.mod_per_seq, self.rope = mod_base, mod_per_seq, rope
        self.rows = n_seq * seq_len
        self.nc = seq_len // CHUNK
        self.n_chunks = self.rows // CHUNK


def _ada_kernel(c_ref, w_ref, b_ref, o_ref):
    c = c_ref[...]
    a = c * jax.nn.sigmoid(c)
    o_ref[...] = _bdot(a, w_ref[...]) + b_ref[...]


def _ada(cvec, w_ada, b_ada, tn=1024):
    n = w_ada.shape[-1]
    return pl.pallas_call(
        _ada_kernel,
        grid=(DEPTH, n // tn),
        in_specs=[pl.BlockSpec((8, D_MODEL), lambda l, j: (0, 0)),
                  pl.BlockSpec((None, D_MODEL, tn), lambda l, j: (l, 0, j)),
                  pl.BlockSpec((None, 1, tn), lambda l, j: (l, 0, j))],
        out_specs=pl.BlockSpec((None, 8, tn), lambda l, j: (l, 0, j)),
        out_shape=jax.ShapeDtypeStruct((DEPTH, 8, n), F32),
        compiler_params=_cparams(2),
        name="ada",
    )(cvec, w_ada, b_ada.reshape(DEPTH, 1, n))


def _mod_spec(layer, path, tm, which):
    def index(i, *_):
        r = path.mod_base + path.mod_per_seq * ((i * tm) // path.seq_len)
        return ((layer * 8 + r) * 6 + which, 0, 0)
    return pl.BlockSpec((None, 1, D_MODEL), index)


def _in_proj_kernel(x_ref, g_ref, sc_ref, sh_ref, w_ref, o_ref, h_scr):
    @pl.when(pl.program_id(1) == 0)
    def _():
        h = _rms(x_ref[...], g_ref[...]) * (1.0 + sc_ref[...]) + sh_ref[...]
        h_scr[...] = h.astype(BF16)
    o_ref[...] = jnp.dot(h_scr[...], w_ref[...], preferred_element_type=F32)


def _in_proj(x, mod3, norm1, w_in_p, layer, path, tm=1024, tn=1280):
    rows = x.shape[0]
    return pl.pallas_call(
        _in_proj_kernel,
        grid=(rows // tm, D_IN_P // tn),
        in_specs=[pl.BlockSpec((tm, D_MODEL), lambda i, j: (i, 0)),
                  pl.BlockSpec((None, 1, D_MODEL), lambda i, j: (layer, 0, 0)),
                  _mod_spec(layer, path, tm, 1),
                  _mod_spec(layer, path, tm, 0),
                  pl.BlockSpec((None, D_MODEL, tn), lambda i, j: (layer, 0, j))],
        out_specs=pl.BlockSpec((tm, tn), lambda i, j: (i, j)),
        out_shape=jax.ShapeDtypeStruct((rows, D_IN_P), F32),
        scratch_shapes=[pltpu.VMEM((tm, D_MODEL), BF16)],
        compiler_params=_cparams(2),
        name="in_proj",
    )(x, norm1.reshape(DEPTH, 1, D_MODEL), mod3, mod3, w_in_p)


def _chunk_of(di, n, nc):
    return n if di == 0 else nc - 1 - n


def _seq_row_spec(path, di, width, col_block):
    nc = path.nc
    return pl.BlockSpec((CHUNK, width), lambda s, n: (s * nc + _chunk_of(di, n, nc), col_block))


def _seq_gate_spec(path, di):
    return _seq_row_spec(path, di, LANES, COL_GATE + di)


def _seq_snap_spec(path, di, shape):
    nc = path.nc
    zeros = (0,) * len(shape)
    return pl.BlockSpec((None,) + shape, lambda s, n: (s * nc + _chunk_of(di, n, nc),) + zeros)


def _seq_state_spec(shape):
    zeros = (0,) * (len(shape) + 1)
    return pl.BlockSpec((None, N_DIR) + shape, lambda s, n: (s,) + zeros)


def _final_state_spec(shape, layer):
    zeros = (0,) * (len(shape) + 1)
    return pl.BlockSpec((None, None, N_DIR) + shape, lambda s, n: (s, layer) + zeros)


def _final_state_shape(path, shape):
    return jax.ShapeDtypeStruct((path.n_seq, path.state_layers, N_DIR) + shape, F32)


def _full_spec(shape, n_axes=2):
    zeros = (0,) * len(shape)
    return pl.BlockSpec(shape, lambda *_: zeros)


def _out_row_spec(width, col_block):
    return pl.BlockSpec((CPS * CHUNK, width), lambda i: (i, col_block))


def _out_snap_spec(shape):
    zeros = (0,) * len(shape)
    return pl.BlockSpec((CPS,) + shape, lambda i: (i,) + zeros)


class _Part(NamedTuple):
    kernel: Callable
    in_specs: list
    out_specs: list
    out_shape: list
    scratch: list
    args: list
    carry: tuple = ()


def _fused_kernel(*refs, layout, rounds):
    n_in = sum(ni + nc for _, ni, nc, _, _ in layout)
    n_out = sum(no for _, _, _, no, _ in layout)
    ins, outs, scr = refs[:n_in], refs[n_in:n_in + n_out], refs[n_in + n_out:]
    for r, chunks in enumerate(rounds or (None,)):
        i = o = s = 0
        for body, ni, nc, no, ns in layout:
            kw = {} if chunks is None else dict(chunks=chunks, first_round=r == 0)
            body(*ins[i:i + ni], *outs[o:o + no], *scr[s:s + ns], **kw)
            i, o, s = i + ni + nc, o + no, s + ns


def _fused_call(parts, grid, name, rounds=None):
    layout = tuple((p.kernel, len(p.in_specs), len(p.carry), len(p.out_specs), len(p.scratch))
                   for p in parts)
    in_specs, args, aliases, n_out = [], [], {}, 0
    for p in parts:
        in_specs += p.in_specs
        args += p.args
        first_carried_out = n_out + len(p.out_specs) - len(p.carry)
        for c, arr in enumerate(p.carry):
            aliases[len(args)] = first_carried_out + c
            in_specs.append(pl.BlockSpec(memory_space=pl.ANY))
            args.append(arr)
        n_out += len(p.out_specs)
    outs = pl.pallas_call(
        functools.partial(_fused_kernel, layout=layout, rounds=rounds),
        grid=grid,
        in_specs=in_specs,
        out_specs=tuple(s for p in parts for s in p.out_specs),
        out_shape=tuple(s for p in parts for s in p.out_shape),
        scratch_shapes=[s for p in parts for s in p.scratch],
        input_output_aliases=aliases,
        compiler_params=_cparams(len(grid)),
        name=name,
    )(*args)
    split, o = [], 0
    for p in parts:
        split.append(outs[o:o + len(p.out_specs)])
        o += len(p.out_specs)
    return split


def _s5_kernel(uf_ref, ub_ref, bre_ref, bim_ref, cre_ref, cim_ref, pwr_ref, pwi_ref, h0r_ref, h0i_ref,
               yf_ref, yb_ref, hnr_ref, hni_ref,
               ut, yt, hlr, hli, st_r, st_i, *, nc):
    n = pl.program_id(1)

    @pl.when(n == 0)
    def _():
        st_r[...] = jnp.broadcast_to(h0r_ref[...], st_r.shape)
        st_i[...] = jnp.broadcast_to(h0i_ref[...], st_i.shape)

    u_refs = (uf_ref, ub_ref)
    y_refs = (yf_ref, yb_ref)
    seg = lax.broadcasted_iota(jnp.int32, (S5_SEG, S5_BW), 0)
    pos = (seg, S5_SEG - 1 - seg)

    def step_rows(di, kk):
        k = kk if di == 0 else S5_SLEN - 1 - kk
        return slice(k * S5_SEG, (k + 1) * S5_SEG)

    def lanes_of(cb):
        return slice(cb * S5_BW, (cb + 1) * S5_BW)

    for cb0 in range(0, S5_CB, S5_ILV):
        chains = [(di, cb) for cb in range(cb0, cb0 + S5_ILV) for di in range(N_DIR)]
        for di, cb in chains:
            ut[di, cb] = u_refs[di][:, cb * LANES:(cb + 1) * LANES]
            ucb = jnp.concatenate([ut[di, cb, pl.ds(k, S5_SEG, stride=S5_SLEN), :] for k in range(S5_SLEN)],
                                  axis=0).astype(BF16)
            hlr[di, cb] = jnp.dot(ucb, bre_ref[di, cb], preferred_element_type=F32)
            hli[di, cb] = jnp.dot(ucb, bim_ref[di, cb], preferred_element_type=F32)
        hr = {c: jnp.zeros((S5_SEG, S5_BW), F32) for c in chains}
        hi = {c: jnp.zeros((S5_SEG, S5_BW), F32) for c in chains}
        for kk in range(S5_SLEN):
            for c in chains:
                di, cb = c
                rows = step_rows(di, kk)
                lr = pwr_ref[di, 0, :, lanes_of(cb)]
                li = pwi_ref[di, 0, :, lanes_of(cb)]
                hr[c], hi[c] = (lr * hr[c] - li * hi[c] + hlr[di, cb, rows, :],
                                lr * hi[c] + li * hr[c] + hli[di, cb, rows, :])
                hlr[di, cb, rows, :] = hr[c]
                hli[di, cb, rows, :] = hi[c]
        cr, ci = {}, {}
        for c in chains:
            di, cb = c
            one = 1 if di == 0 else S5_SEG - 1
            first = pos[di] == 0
            cr[c] = jnp.where(first, pltpu.roll(st_r[di, :, lanes_of(cb)], one, 0),
                              pltpu.roll(hr[c], one, 0))
            ci[c] = jnp.where(first, pltpu.roll(st_i[di, :, lanes_of(cb)], one, 0),
                              pltpu.roll(hi[c], one, 0))
        for lvl in range(S5_SEG.bit_length() - 1):
            dist = 1 << lvl
            for c in chains:
                di, cb = c
                mr = pwr_ref[di, S5_SLEN - 1 + lvl, :, lanes_of(cb)]
                mi = pwi_ref[di, S5_SLEN - 1 + lvl, :, lanes_of(cb)]
                shift = dist if di == 0 else S5_SEG - dist
                sr = jnp.where(pos[di] >= dist, pltpu.roll(cr[c], shift, 0), 0.0)
                si = jnp.where(pos[di] >= dist, pltpu.roll(ci[c], shift, 0), 0.0)
                cr[c], ci[c] = cr[c] + mr * sr - mi * si, ci[c] + mr * si + mi * sr
        for c in chains:
            di, cb = c
            l16r = pwr_ref[di, S5_SLEN - 1, :, lanes_of(cb)]
            l16i = pwi_ref[di, S5_SLEN - 1, :, lanes_of(cb)]
            st_r[di, :, lanes_of(cb)] = hr[c] + l16r * cr[c] - l16i * ci[c]
            st_i[di, :, lanes_of(cb)] = hi[c] + l16r * ci[c] + l16i * cr[c]
        for kk in range(S5_SLEN):
            for c in chains:
                di, cb = c
                pr = pwr_ref[di, kk, :, lanes_of(cb)]
                pi = pwi_ref[di, kk, :, lanes_of(cb)]
                rows = step_rows(di, kk)
                hlr[di, cb, rows, :] = hlr[di, cb, rows, :] + pr * cr[c] - pi * ci[c]
                hli[di, cb, rows, :] = hli[di, cb, rows, :] + pr * ci[c] + pi * cr[c]
        for di, cb in chains:
            yt[di, cb] = (jnp.dot(hlr[di, cb].astype(BF16), cre_ref[di, cb], preferred_element_type=F32)
                          + jnp.dot(hli[di, cb].astype(BF16), cim_ref[di, cb], preferred_element_type=F32))
            per = S5_SLEN // S5_SEG
            for r in range(CHUNK // S5_SEG):
                start = (r % per) * S5_SEG * S5_SEG + r // per
                y_refs[di][r * S5_SEG:(r + 1) * S5_SEG, cb * LANES:(cb + 1) * LANES] = (
                    yt[di, cb, pl.ds(start, S5_SEG, stride=S5_SEG), :])

    @pl.when(n == nc - 1)
    def _():
        for di in range(N_DIR):
            last = S5_SEG - 1 if di == 0 else 0
            hnr_ref[di] = st_r[di, last:last + 1, :]
            hni_ref[di] = st_i[di, last:last + 1, :]


def _s5_params(lam_re, lam_im, log_dt, b_re, b_im, c_re, c_im):
    dt = jnp.exp(log_dt)[..., None]
    mag = jnp.exp(lam_re * dt)
    lbr, lbi = mag * jnp.cos(lam_im * dt), mag * jnp.sin(lam_im * dt)
    den = lam_re * lam_re + lam_im * lam_im
    fr = ((lbr - 1.0) * lam_re + lbi * lam_im) / den
    fi = (lbi * lam_re - (lbr - 1.0) * lam_im) / den
    bbr = fr[..., None] * b_re - fi[..., None] * b_im
    bbi = fr[..., None] * b_im + fi[..., None] * b_re
    gpb = S5_GROUPS // S5_CB
    eye = jnp.eye(gpb, dtype=F32)

    def pack_b(b):
        b = b.reshape(N_DIR, S5_CB, gpb, S5_P, S5_CH)
        m = jnp.einsum('dbgpc,gh->dbgchp', b, eye)
        return m.reshape(N_DIR, S5_CB, gpb * S5_CH, gpb * S5_P).astype(BF16)

    def pack_c(c):
        c = c.reshape(N_DIR, S5_CB, gpb, S5_CH, S5_P)
        m = jnp.einsum('dbgcp,gh->dbgphc', c, eye)
        return m.reshape(N_DIR, S5_CB, gpb * S5_P, gpb * S5_CH).astype(BF16)

    pr, pi = [lbr], [lbi]
    for _ in range(S5_SLEN - 1):
        pr, pi = pr + [pr[-1] * lbr - pi[-1] * lbi], pi + [pr[-1] * lbi + pi[-1] * lbr]
    for _ in range(S5_SEG.bit_length() - 2):
        pr, pi = pr + [pr[-1] * pr[-1] - pi[-1] * pi[-1]], pi + [2.0 * pr[-1] * pi[-1]]
    tab = lambda p: jnp.broadcast_to(jnp.stack(p, axis=1).reshape(N_DIR, len(p), 1, S5_ST),
                                     (N_DIR, len(p), S5_SEG, S5_ST))
    return pack_b(bbr), pack_b(bbi), pack_c(c_re), pack_c(-c_im), tab(pr), tab(pi)


def _s5_part(proj, prm, h0r, h0i, path, layer, carry):
    nc = path.nc
    bre, bim, cre, cim, pwr, pwi = prm
    yshape = jax.ShapeDtypeStruct((path.rows, GROUP_W), F32)
    stshape = _final_state_shape(path, (1, S5_ST))
    final = _final_state_spec((1, S5_ST), layer)
    yspec = lambda di: pl.BlockSpec((CHUNK, GROUP_W), lambda s, n: (s * nc + _chunk_of(di, n, nc), 0))
    dir_scr = lambda rows: pltpu.VMEM((N_DIR, S5_CB, rows, S5_BW), F32)
    return _Part(
        functools.partial(_s5_kernel, nc=nc),
        in_specs=[_seq_row_spec(path, 0, GROUP_W, COL_S5), _seq_row_spec(path, 1, GROUP_W, COL_S5),
                  _full_spec(bre.shape), _full_spec(bim.shape), _full_spec(cre.shape),
                  _full_spec(cim.shape), _full_spec(pwr.shape), _full_spec(pwi.shape),
                  _seq_state_spec((1, S5_ST)), _seq_state_spec((1, S5_ST))],
        out_specs=[yspec(0), yspec(1), final, final],
        out_shape=[yshape, yshape, stshape, stshape],
        scratch=[pltpu.VMEM((N_DIR, S5_CB, CHUNK, LANES), F32)] * 2
                + [dir_scr(CHUNK)] * 2 + [pltpu.VMEM((N_DIR, S5_SEG, S5_ST), F32)] * 2,
        args=[proj, proj, bre, bim, cre, cim, pwr, pwi, h0r, h0i],
        carry=carry)


def _gate_act(g):
    lane = lax.broadcasted_iota(jnp.int32, g.shape, 1)
    return jnp.where(lane < GATE_F, g, jnp.where(lane < GATE_DT, jax.nn.log_sigmoid(g), 0.0))


def _mlstm_state_kernel(kf_ref, vf_ref, gf_ref, kb_ref, vb_ref, gb_ref, bias_ref, c0_ref, n0_ref, m0_ref,
                        cf_ref, mf_ref, cb_ref, mb_ref, cn_ref, nn_ref, mn_ref, caug, mscr, *, nc):
    n = pl.program_id(1)
    dh = MLSTM_DH

    @pl.when(n == 0)
    def _():
        for di in range(N_DIR):
            for h in range(MLSTM_HEADS):
                caug[di, h, 0:dh, :] = c0_ref[di, h].T
                caug[di, h, dh:2 * dh, :] = jnp.broadcast_to(n0_ref[di, h], (dh, dh))
                mscr[di, h] = m0_ref[di, h]

    row, col = _iota2()
    ones = jnp.ones((CHUNK, dh), F32)
    dirs = ((kf_ref, vf_ref, gf_ref, cf_ref, mf_ref), (kb_ref, vb_ref, gb_ref, cb_ref, mb_ref))
    for di, (k_ref, v_ref, g_ref, c_out, m_out) in enumerate(dirs):
        p = _gate_act(g_ref[...] + bias_ref[di])
        bcol = _dot01_left(_to01(_dir_mask(di, row, col)), p)
        btot = jnp.sum(p, axis=0, keepdims=True)
        for h in range(MLSTM_HEADS):
            sl = slice(h * dh, (h + 1) * dh)
            k = k_ref[:, sl] * (dh ** -0.5)
            v = v_ref[:, sl]
            bc = bcol[:, GATE_F + h:GATE_F + h + 1]
            bt = btot[:, GATE_F + h:GATE_F + h + 1]
            igc = p[:, h:h + 1]
            mrow = mscr[di, h]
            mm = mrow[:, 0:1]
            cst = caug[di, h]
            c_out[h] = cst.astype(BF16)
            m_out[h] = mrow
            gcol = bt - bc + igc
            m_new = jnp.maximum(bt + mm, jnp.max(gcol, axis=0, keepdims=True))
            w_s = jnp.exp(gcol - m_new)
            decay_old = jnp.exp(bt + mm - m_new)
            vaug = jnp.concatenate([v, ones], axis=1)
            caug[di, h] = decay_old * cst + _bdot_tn(vaug, w_s * k)
            mscr[di, h] = jnp.broadcast_to(m_new, (1, LANES))

    @pl.when(n == nc - 1)
    def _():
        for di in range(N_DIR):
            for h in range(MLSTM_HEADS):
                cn_ref[di, h] = caug[di, h, 0:dh, :].T
                nn_ref[di, h] = caug[di, h, dh:dh + 1, :]
                mn_ref[di, h] = mscr[di, h]


def _mlstm_out_kernel(q_ref, k_ref, v_ref, o_ref, g0_ref, g1_ref, bias_ref, norm_ref,
                      cf_ref, mf_ref, cb_ref, mb_ref, y_ref, *, chunks, first_round):
    dh = MLSTM_DH
    row, col = _iota2()
    masks = [_dir_mask(di, row, col) for di in range(N_DIR)]
    masks_t = [masks[1 - di] for di in range(N_DIR)]
    m01 = [_to01(m) for m in masks]
    mt01 = [_to01(m) for m in masks_t]
    g_refs = (g0_ref, g1_ref)
    c_refs = (cf_ref, cb_ref)
    m_refs = (mf_ref, mb_ref)
    for ci in chunks:
        rs = slice(ci * CHUNK, (ci + 1) * CHUNK)
        ps, bcols, brows = [], [], []
        for di in range(N_DIR):
            p = _gate_act(g_refs[di][rs, :] + bias_ref[di])
            ps.append(p)
            bcols.append(_dot01_left(m01[di], p))
            brows.append(_dot01_right(p.T, mt01[di]))
        outs = []
        for h in range(MLSTM_HEADS):
            sl = slice(h * dh, (h + 1) * dh)
            q = q_ref[rs, sl].astype(BF16)
            k = (k_ref[rs, sl] * (dh ** -0.5)).astype(BF16)
            v = v_ref[rs, sl].astype(BF16)
            qk = _bdot_nt(k, q)
            wsum = None
            extra = None
            for di in range(N_DIR):
                fl = GATE_F + h
                a_s = ps[di][:, h:h + 1] - bcols[di][:, fl:fl + 1]
                b_t = brows[di][fl:fl + 1, :]
                mm = m_refs[di][ci, h]
                am = jnp.where(masks_t[di], a_s, -jnp.inf)
                big = jnp.maximum(mm, jnp.max(am, axis=0, keepdims=True))
                sc = qk * jnp.exp(am - big)
                w_inter = jnp.exp(mm - big)
                qc = _bdot_nt(c_refs[di][ci, h], q)
                den = jnp.sum(sc, axis=0, keepdims=True) + w_inter * qc[dh:dh + 1, :]
                r = 1.0 / jnp.maximum(jnp.abs(den), jnp.exp(-(b_t + big)))
                wsum = sc * r if wsum is None else wsum + sc * r
                e = qc[0:dh, :] * (w_inter * r)
                extra = e if extra is None else extra + e
            ht = _bdot_tn(v, wsum) + extra
            ht = ht * lax.rsqrt(jnp.mean(ht * ht, axis=0, keepdims=True) + EPS)
            outs.append(ht.T)
        hs = jnp.concatenate(outs, axis=1) * norm_ref[...]
        y_ref[rs, :] = (jax.nn.sigmoid(o_ref[rs, :]) * hs).astype(BF16)


def _mlstm_state_part(proj, bias, c0, n0, m0, path, layer, carry):
    nc = path.nc
    hh, dh = MLSTM_HEADS, MLSTM_DH
    csnap = jax.ShapeDtypeStruct((path.n_chunks, hh, 2 * dh, dh), BF16)
    msnap = jax.ShapeDtypeStruct((path.n_chunks, hh, 1, LANES), F32)
    seq_in = lambda di: [_seq_row_spec(path, di, GROUP_W, COL_MK), _seq_row_spec(path, di, GROUP_W, COL_MV),
                         _seq_gate_spec(path, di)]
    snap_out = lambda di: [_seq_snap_spec(path, di, (hh, 2 * dh, dh)), _seq_snap_spec(path, di, (hh, 1, LANES))]
    finals = [(hh, dh, dh), (hh, 1, dh), (hh, 1, LANES)]
    return _Part(
        functools.partial(_mlstm_state_kernel, nc=nc),
        in_specs=seq_in(0) + seq_in(1) + [_full_spec((N_DIR, 1, LANES)), _seq_state_spec((hh, dh, dh)),
                                         _seq_state_spec((hh, 1, dh)), _seq_state_spec((hh, 1, LANES))],
        out_specs=snap_out(0) + snap_out(1) + [_final_state_spec(s, layer) for s in finals],
        out_shape=[csnap, msnap, csnap, msnap] + [_final_state_shape(path, s) for s in finals],
        scratch=[pltpu.VMEM((N_DIR, hh, 2 * dh, dh), F32), pltpu.VMEM((N_DIR, hh, 1, LANES), F32)],
        args=[proj, proj, proj, proj, proj, proj, bias, c0, n0, m0],
        carry=carry)


def _mlstm_out_part(proj, bias, norm, snaps, path):
    hh, dh = MLSTM_HEADS, MLSTM_DH
    return _Part(
        _mlstm_out_kernel,
        in_specs=[_out_row_spec(GROUP_W, COL_MQ), _out_row_spec(GROUP_W, COL_MK),
                  _out_row_spec(GROUP_W, COL_MV), _out_row_spec(GROUP_W, COL_MO),
                  _out_row_spec(LANES, COL_GATE), _out_row_spec(LANES, COL_GATE + 1),
                  _full_spec((N_DIR, 1, LANES)), _full_spec((1, GROUP_W)),
                  _out_snap_spec((hh, 2 * dh, dh)), _out_snap_spec((hh, 1, LANES)),
                  _out_snap_spec((hh, 2 * dh, dh)), _out_snap_spec((hh, 1, LANES))],
        out_specs=[_out_row_spec(GROUP_W, 0)],
        out_shape=[jax.ShapeDtypeStruct((path.rows, GROUP_W), BF16)],
        scratch=[],
        args=[proj, proj, proj, proj, proj, proj, bias, norm] + list(snaps))


def _rope(a, cos, sin):
    return a * cos + pltpu.roll(a, RET_DH // 2, 1) * sin


def _ret_state_kernel(*refs, nc, rope):
    if rope:
        (kf_ref, vf_ref, kb_ref, vb_ref, lg_ref, s0_ref, cosf_ref, sinf_ref, cosb_ref, sinb_ref,
         sf_ref, sb_ref, sn_ref, sscr) = refs
        tabs = ((cosf_ref, sinf_ref), (cosb_ref, sinb_ref))
    else:
        kf_ref, vf_ref, kb_ref, vb_ref, lg_ref, s0_ref, sf_ref, sb_ref, sn_ref, sscr = refs
    n = pl.program_id(1)
    dh = RET_DH

    @pl.when(n == 0)
    def _():
        sscr[...] = s0_ref[...]

    t = lax.broadcasted_iota(jnp.int32, (CHUNK, dh), 0).astype(F32)
    for di, (k_ref, v_ref, s_out) in enumerate(((kf_ref, vf_ref, sf_ref), (kb_ref, vb_ref, sb_ref))):
        left = CHUNK - 1.0 - t if di == 0 else t
        for h in range(RET_HEADS):
            sl = slice(h * dh, (h + 1) * dh)
            k = k_ref[:, sl]
            if rope:
                k = _rope(k, tabs[di][0][...], tabs[di][1][...])
            k = k * (dh ** -0.5)
            lg = -jnp.exp(lg_ref[di, :, sl])
            st = sscr[di, h]
            s_out[h] = st.astype(BF16)
            sscr[di, h] = jnp.exp(CHUNK * lg) * st + _bdot((k * jnp.exp(left * lg)).T, v_ref[:, sl])

    @pl.when(n == nc - 1)
    def _():
        sn_ref[...] = sscr[...]


def _ret_out_kernel(*refs, rope, chunks, first_round):
    if rope:
        (q_ref, k_ref, v_ref, g_ref, lg_ref, norm_ref, sf_ref, sb_ref, cos_ref, sin_ref,
         y_ref, dsum, qdec) = refs
    else:
        q_ref, k_ref, v_ref, g_ref, lg_ref, norm_ref, sf_ref, sb_ref, y_ref, dsum, qdec = refs
    dh = RET_DH

    def build_tables():
        row, col = _iota2()
        absdiff = jnp.abs(row - col).astype(F32)
        t = row.astype(F32)
        for h in range(RET_HEADS):
            sl = slice(h * dh, (h + 1) * dh)
            lgf = -jnp.exp(lg_ref[0, :, sl])
            lgb = -jnp.exp(lg_ref[1, :, sl])
            both = jnp.exp(absdiff * jnp.where(row >= col, lgf, lgb))
            dsum[h] = jnp.where(row == col, 2.0, both)
            qdec[0, h] = jnp.exp((t + 1.0) * lgf)
            qdec[1, h] = jnp.exp((CHUNK - t) * lgb)

    if first_round:
        pl.when(pl.program_id(0) == 0)(build_tables)

    s_refs = (sf_ref, sb_ref)
    for ci in chunks:
        rs = slice(ci * CHUNK, (ci + 1) * CHUNK)
        outs = []
        for h in range(RET_HEADS):
            sl = slice(h * dh, (h + 1) * dh)
            q = q_ref[rs, sl]
            k = k_ref[rs, sl]
            if rope:
                cos, sin = cos_ref[rs, :], sin_ref[rs, :]
                q = _rope(q, cos, sin)
                k = _rope(k, cos, sin)
            q = q.astype(BF16)
            k = (k * (dh ** -0.5)).astype(BF16)
            y = _bdot(_bdot_nt(q, k) * dsum[h], v_ref[rs, sl])
            for di in range(N_DIR):
                y = y + qdec[di, h] * jnp.dot(q, s_refs[di][ci, h], preferred_element_type=F32)
            outs.append(y)
        ys = _head_norm(jnp.concatenate(outs, axis=1), RET_HEADS, dh) * norm_ref[...]
        gate = g_ref[rs, :]
        y_ref[rs, :] = (ys * (gate * jax.nn.sigmoid(gate))).astype(BF16)


def _rope_tables(seq_len):
    n_rows = seq_len // GRID_W
    rows = jnp.repeat(jnp.arange(n_rows, dtype=F32), GRID_W)
    cols = jnp.tile(jnp.arange(GRID_W, dtype=F32), n_rows)
    quarter = RET_DH // 4
    freqs = ROPE_BASE ** (-jnp.arange(quarter, dtype=F32) / quarter)
    ang = jnp.concatenate([rows[:, None] * freqs, cols[:, None] * freqs], axis=-1)
    cos, sin = jnp.cos(ang), jnp.sin(ang)
    return jnp.concatenate([cos, cos], axis=-1), jnp.concatenate([-sin, sin], axis=-1)


def _ret_state_part(proj, lg, s0, rope_tabs, path, layer, carry):
    nc = path.nc
    hh, dh = RET_HEADS, RET_DH
    snap = jax.ShapeDtypeStruct((path.n_chunks, hh, dh, dh), BF16)
    seq_in = lambda di: [_seq_row_spec(path, di, GROUP_W, COL_RK), _seq_row_spec(path, di, GROUP_W, COL_RV)]
    in_specs = seq_in(0) + seq_in(1) + [_full_spec((N_DIR, 1, GROUP_W)), _seq_state_spec((hh, dh, dh))]
    args = [proj, proj, proj, proj, lg, s0]
    if path.rope:
        for di in range(N_DIR):
            tab = pl.BlockSpec((CHUNK, dh), lambda s, n, di=di: (_chunk_of(di, n, nc), 0))
            in_specs += [tab, tab]
            args += list(rope_tabs)
    return _Part(
        functools.partial(_ret_state_kernel, nc=nc, rope=path.rope),
        in_specs=in_specs,
        out_specs=[_seq_snap_spec(path, 0, (hh, dh, dh)), _seq_snap_spec(path, 1, (hh, dh, dh)),
                   _final_state_spec((hh, dh, dh), layer)],
        out_shape=[snap, snap, _final_state_shape(path, (hh, dh, dh))],
        scratch=[pltpu.VMEM((N_DIR, hh, dh, dh), F32)],
        args=args,
        carry=carry)


def _ret_out_part(proj, lg, norm, snaps, rope_tabs, path):
    hh, dh = RET_HEADS, RET_DH
    in_specs = [_out_row_spec(GROUP_W, COL_RQ), _out_row_spec(GROUP_W, COL_RK),
                _out_row_spec(GROUP_W, COL_RV), _out_row_spec(GROUP_W, COL_RG),
                _full_spec((N_DIR, 1, GROUP_W)), _full_spec((1, GROUP_W)),
                _out_snap_spec((hh, dh, dh)), _out_snap_spec((hh, dh, dh))]
    args = [proj, proj, proj, proj, lg, norm] + list(snaps)
    if path.rope:
        per_seq = path.nc // CPS
        tab = pl.BlockSpec((CPS * CHUNK, dh), lambda i: (i % per_seq, 0))
        in_specs += [tab, tab]
        args += list(rope_tabs)
    return _Part(
        functools.partial(_ret_out_kernel, rope=path.rope),
        in_specs=in_specs,
        out_specs=[_out_row_spec(GROUP_W, 0)],
        out_shape=[jax.ShapeDtypeStruct((path.rows, GROUP_W), BF16)],
        scratch=[pltpu.VMEM((hh, CHUNK, CHUNK), F32), pltpu.VMEM((N_DIR, hh, CHUNK, dh), F32)],
        args=args)


def _conv_kernel(x_ref, xp_ref, xn_ref, cw_ref, cb_ref, o_ref, ext, *, bps, tr):
    ib = pl.program_id(0) % bps
    ext[0:HALO, :] = jnp.where(ib > 0, xp_ref[...], 0.0)
    ext[HALO:HALO + tr, :] = x_ref[...]
    ext[HALO + tr:, :] = jnp.where(ib < bps - 1, xn_ref[...], 0.0)
    pad = (SSD_CONV - 1) // 2
    acc = cb_ref[...] + cw_ref[0:1, :] * ext[HALO - pad:HALO - pad + tr, :]
    for j in range(1, SSD_CONV):
        acc = acc + cw_ref[j:j + 1, :] * ext[HALO - pad + j:HALO - pad + j + tr, :]
    o_ref[...] = acc * jax.nn.sigmoid(acc)


def _ssd_conv(proj, conv_w, conv_b, path):
    tr = min(CONV_ROWS, path.seq_len)
    per = tr // HALO
    n_halo = path.rows // HALO
    bps = path.seq_len // tr
    return pl.pallas_call(
        functools.partial(_conv_kernel, bps=bps, tr=tr),
        grid=(path.rows // tr,),
        in_specs=[pl.BlockSpec((tr, SSD_XBC), lambda i: (i, COL_XBC)),
                  pl.BlockSpec((HALO, SSD_XBC), lambda i: (jnp.maximum(i * per - 1, 0), COL_XBC)),
                  pl.BlockSpec((HALO, SSD_XBC), lambda i: (jnp.minimum((i + 1) * per, n_halo - 1), COL_XBC)),
                  _full_spec((SSD_CONV, SSD_XBC)), _full_spec((1, SSD_XBC))],
        out_specs=pl.BlockSpec((tr, SSD_XBC), lambda i: (i, 0)),
        out_shape=jax.ShapeDtypeStruct((path.rows, SSD_XBC), F32),
        scratch_shapes=[pltpu.VMEM((tr + 2 * HALO, SSD_XBC), F32)],
        compiler_params=_cparams(1),
        name="ssd_conv",
    )(proj, proj, proj, conv_w, conv_b)


def _ssd_rates(g, bias, alog):
    lane = lax.broadcasted_iota(jnp.int32, g.shape, 1)
    dt_lanes = (lane >= GATE_DT) & (lane < GATE_DT + SSD_HEADS)
    dt = jnp.where(dt_lanes, jax.nn.softplus(g + bias), 0.0)
    return dt, dt * -jnp.exp(alog)


def _ssd_state_kernel(xf_ref, bf_ref, gf_ref, xb_ref, bb_ref, gb_ref, bias_ref, alog_ref, h0_ref,
                      hf_ref, hb_ref, hn_ref, hscr, *, nc):
    n = pl.program_id(1)

    @pl.when(n == 0)
    def _():
        for di in range(N_DIR):
            for gi in range(SSD_GROUPS):
                hscr[di, gi] = h0_ref[di, gi * SSD_HPG:(gi + 1) * SSD_HPG].reshape(SSD_GW, SSD_N).T

    row, col = _iota2()
    dirs = ((xf_ref, bf_ref, gf_ref, hf_ref), (xb_ref, bb_ref, gb_ref, hb_ref))
    for di, (x_ref, b_ref, g_ref, h_out) in enumerate(dirs):
        dt, a = _ssd_rates(g_ref[...], bias_ref[di], alog_ref[di])
        ccol = _dot01_left(_to01(_dir_mask(di, row, col)), a)
        ctot = jnp.sum(a, axis=0, keepdims=True)
        for gi in range(SSD_GROUPS):
            hst = hscr[di, gi]
            h_out[gi] = hst.astype(BF16)
            wx_parts, tot_parts = [], []
            for r in range(SSD_HPG):
                hd = gi * SSD_HPG + r
                ln = GATE_DT + hd
                tot = ctot[:, ln:ln + 1]
                w = jnp.exp(tot - ccol[:, ln:ln + 1]) * dt[:, ln:ln + 1]
                wx_parts.append(w * x_ref[:, hd * SSD_P:(hd + 1) * SSD_P])
                tot_parts.append(jnp.broadcast_to(jnp.exp(tot), (1, SSD_P)))
            wx = jnp.concatenate(wx_parts, axis=1)
            bm = b_ref[:, gi * SSD_N:(gi + 1) * SSD_N]
            hscr[di, gi] = jnp.concatenate(tot_parts, axis=1) * hst + _bdot(bm.T, wx)

    @pl.when(n == nc - 1)
    def _():
        for di in range(N_DIR):
            for gi in range(SSD_GROUPS):
                hn_ref[di, gi * SSD_HPG:(gi + 1) * SSD_HPG] = hscr[di, gi].T.reshape(SSD_HPG, SSD_P, SSD_N)


def _ssd_out_kernel(z_ref, xbc_ref, g0_ref, g1_ref, bias_ref, alog_ref, dsk_ref, norm_ref,
                    hf_ref, hb_ref, y_ref, *, chunks, first_round):
    row, col = _iota2()
    masks = [_dir_mask(di, row, col) for di in range(N_DIR)]
    m01 = [_to01(m) for m in masks]
    mt01 = [_to01(masks[1 - di]) for di in range(N_DIR)]
    g_refs = (g0_ref, g1_ref)
    h_refs = (hf_ref, hb_ref)
    for ci in chunks:
        rs = slice(ci * CHUNK, (ci + 1) * CHUNK)
        dtts, ccols, crows = [], [], []
        for di in range(N_DIR):
            dt, a = _ssd_rates(g_refs[di][rs, :], bias_ref[di], alog_ref[di])
            dtts.append(dt.T)
            ccols.append(_dot01_left(m01[di], a))
            crows.append(_dot01_right(a.T, mt01[di]))
        xs = xbc_ref[rs, 0:GROUP_W]
        y_parts = []
        for gi in range(SSD_GROUPS):
            bm = xbc_ref[rs, GROUP_W + gi * SSD_N:GROUP_W + (gi + 1) * SSD_N].astype(BF16)
            c0 = GROUP_W + (SSD_GROUPS + gi) * SSD_N
            cm = xbc_ref[rs, c0:c0 + SSD_N].astype(BF16)
            cb = _bdot_nt(cm, bm)
            inter = [jnp.dot(cm, h_refs[di][ci, gi], preferred_element_type=F32) for di in range(N_DIR)]
            for r in range(SSD_HPG):
                hd = gi * SSD_HPG + r
                ln = GATE_DT + hd
                hs = slice(r * SSD_P, (r + 1) * SSD_P)
                wmat = None
                y = None
                for di in range(N_DIR):
                    cc = ccols[di][:, ln:ln + 1]
                    decay = jnp.exp(jnp.where(masks[di], cc - crows[di][ln:ln + 1, :], -jnp.inf))
                    wd = decay * dtts[di][ln:ln + 1, :]
                    wmat = wd if wmat is None else wmat + wd
                    e = jnp.exp(cc) * inter[di][:, hs]
                    y = e if y is None else y + e
                y_parts.append(y + _bdot(cb * wmat, xs[:, hd * SSD_P:(hd + 1) * SSD_P]))
        z = z_ref[rs, :]
        yy = (dsk_ref[...] * xs + jnp.concatenate(y_parts, axis=1)) * (z * jax.nn.sigmoid(z))
        y_ref[rs, :] = _rms(yy, norm_ref[...]).astype(BF16)


SSD_SNAP = (SSD_GROUPS, SSD_N, SSD_GW)


def _ssd_state_part(proj, xbc, bias, alog, h0, path, layer, carry):
    nc = path.nc
    st_shape = (SSD_HEADS, SSD_P, SSD_N)
    snap_shape = SSD_SNAP
    snap = jax.ShapeDtypeStruct((path.n_chunks,) + snap_shape, BF16)

    def seq_in(di):
        return [pl.BlockSpec((CHUNK, GROUP_W), lambda s, n: (s * nc + _chunk_of(di, n, nc), 0)),
                pl.BlockSpec((CHUNK, SSD_GROUPS * SSD_N),
                             lambda s, n: (s * nc + _chunk_of(di, n, nc), GROUP_W // (SSD_GROUPS * SSD_N))),
                _seq_gate_spec(path, di)]

    return _Part(
        functools.partial(_ssd_state_kernel, nc=nc),
        in_specs=seq_in(0) + seq_in(1) + [_full_spec((N_DIR, 1, LANES)), _full_spec((N_DIR, 1, LANES)),
                                         _seq_state_spec(st_shape)],
        out_specs=[_seq_snap_spec(path, 0, snap_shape), _seq_snap_spec(path, 1, snap_shape),
                   _final_state_spec(st_shape, layer)],
        out_shape=[snap, snap, _final_state_shape(path, st_shape)],
        scratch=[pltpu.VMEM((N_DIR,) + snap_shape, F32)],
        args=[xbc, xbc, proj, xbc, xbc, proj, bias, alog, h0],
        carry=carry)


def _ssd_out_part(proj, xbc, bias, alog, dskip, norm, snaps, path):
    return _Part(
        _ssd_out_kernel,
        in_specs=[_out_row_spec(GROUP_W, COL_DZ), _out_row_spec(SSD_XBC, 0),
                  _out_row_spec(LANES, COL_GATE), _out_row_spec(LANES, COL_GATE + 1),
                  _full_spec((N_DIR, 1, LANES)), _full_spec((N_DIR, 1, LANES)),
                  _full_spec((1, GROUP_W)), _full_spec((1, GROUP_W)),
                  _out_snap_spec(SSD_SNAP), _out_snap_spec(SSD_SNAP)],
        out_specs=[_out_row_spec(GROUP_W, 0)],
        out_shape=[jax.ShapeDtypeStruct((path.rows, GROUP_W), BF16)],
        scratch=[],
        args=[proj, xbc, proj, proj, bias, alog, dskip, norm] + list(snaps))


def _out_proj_kernel(u_ref, yf_ref, yb_ref, dsk_ref, wglu_ref, yb2_ref, yc_ref, yd_ref, w_ref, x_ref,
                     g1_ref, g_ref, sc_ref, sh_ref, x1_ref, h2_ref):
    t = jax.nn.gelu(dsk_ref[...] * u_ref[...] + yf_ref[...] + yb_ref[...])
    ya = (t * jax.nn.sigmoid(_bdot(t, wglu_ref[...]))).astype(BF16)
    acc = jnp.dot(ya, w_ref[0], preferred_element_type=F32)
    for i, y_ref in enumerate((yb2_ref, yc_ref, yd_ref)):
        acc = acc + jnp.dot(y_ref[...], w_ref[i + 1], preferred_element_type=F32)
    x1 = x_ref[...] + g1_ref[...] * acc
    x1_ref[...] = x1
    h2_ref[...] = (_rms(x1, g_ref[...]) * (1.0 + sc_ref[...]) + sh_ref[...]).astype(BF16)


def _out_proj(proj, s5_y, s5_post, ys, x, mod3, norm2, w_out_p, layer, path, tm=512):
    rows = x.shape[0]
    yspec = pl.BlockSpec((tm, GROUP_W), lambda i: (i, 0))
    xspec = pl.BlockSpec((tm, D_MODEL), lambda i: (i, 0))
    dskip, wglu = s5_post
    return pl.pallas_call(
        _out_proj_kernel,
        grid=(rows // tm,),
        in_specs=[pl.BlockSpec((tm, GROUP_W), lambda i: (i, COL_S5)), yspec, yspec,
                  _full_spec((1, GROUP_W)), _full_spec((GROUP_W, GROUP_W)),
                  yspec, yspec, yspec,
                  pl.BlockSpec((None, 4, GROUP_W, D_MODEL), lambda i: (layer, 0, 0, 0)),
                  xspec,
                  _mod_spec(layer, path, tm, 2),
                  pl.BlockSpec((None, 1, D_MODEL), lambda i: (layer, 0, 0)),
                  _mod_spec(layer, path, tm, 4),
                  _mod_spec(layer, path, tm, 3)],
        out_specs=(xspec, xspec),
        out_shape=(jax.ShapeDtypeStruct((rows, D_MODEL), F32),
                   jax.ShapeDtypeStruct((rows, D_MODEL), BF16)),
        compiler_params=_cparams(1),
        name="out_proj",
    )(proj, s5_y[0], s5_y[1], dskip, wglu, *ys, w_out_p, x, mod3, norm2.reshape(DEPTH, 1, D_MODEL),
      mod3, mod3)


def _mlp_kernel(h2_ref, x1_ref, w1_ref, w2_ref, g2_ref, fn_ref, o_ref, *, nk, final):
    k = pl.program_id(1)
    a = jnp.dot(h2_ref[...], w1_ref[...], preferred_element_type=F32)
    a = jnp.square(jnp.maximum(a, 0.0))
    part = g2_ref[...] * jnp.dot(a.astype(BF16), w2_ref[...], preferred_element_type=F32)

    @pl.when(k == 0)
    def _():
        o_ref[...] = x1_ref[...] + part

    @pl.when(k > 0)
    def _():
        o_ref[...] += part

    if final:
        @pl.when(k == nk - 1)
        def _():
            o_ref[...] = _rms(o_ref[...], fn_ref[...])


def _mlp(h2, x1, mod3, w1, w2, final_norm, layer, path, final, tm=512, tf=1024):
    rows = x1.shape[0]
    nk = D_FF // tf
    xspec = pl.BlockSpec((tm, D_MODEL), lambda i, k: (i, 0))
    return pl.pallas_call(
        functools.partial(_mlp_kernel, nk=nk, final=final),
        grid=(rows // tm, nk),
        in_specs=[xspec, xspec,
                  pl.BlockSpec((None, D_MODEL, tf), lambda i, k: (layer, 0, k)),
                  pl.BlockSpec((None, tf, D_MODEL), lambda i, k: (layer, k, 0)),
                  _mod_spec(layer, path, tm, 5),
                  pl.BlockSpec((1, D_MODEL), lambda i, k: (0, 0))],
        out_specs=xspec,
        out_shape=jax.ShapeDtypeStruct((rows, D_MODEL), F32),
        compiler_params=_cparams(2),
        name="mlp",
    )(h2, x1, w1, w2, mod3, final_norm.reshape(1, D_MODEL))


def _permute_w_in(w_in):
    o_mi = 5 * GROUP_W
    o_mf = o_mi + N_DIR * MLSTM_HEADS
    o_r = o_mf + N_DIR * MLSTM_HEADS
    o_dt = D_IN - N_DIR * SSD_HEADS
    zeros = jnp.zeros(w_in.shape[:2] + (LANES - 2 * MLSTM_HEADS - SSD_HEADS,), w_in.dtype)
    parts = [w_in[..., :o_mi], w_in[..., o_r:o_dt]]
    for di in range(N_DIR):
        parts += [w_in[..., o_mi + di * MLSTM_HEADS:o_mi + (di + 1) * MLSTM_HEADS],
                  w_in[..., o_mf + di * MLSTM_HEADS:o_mf + (di + 1) * MLSTM_HEADS],
                  w_in[..., o_dt + di * SSD_HEADS:o_dt + (di + 1) * SSD_HEADS], zeros]
    return jnp.concatenate(parts, axis=-1).astype(BF16)


def _gate_rows(i_part, f_part, dt_part):
    pad = jnp.zeros(i_part.shape[:2] + (LANES - 2 * MLSTM_HEADS - SSD_HEADS,), F32)
    return jnp.concatenate([i_part, f_part, dt_part, pad], axis=-1)[:, :, None, :]


def kernel(x_prompt, x_sample, c, state_s5_re, state_s5_im, state_mlstm_c, state_mlstm_n, state_mlstm_m, state_ret, state_ssd, c_ctx, norm1, w_ada, b_ada, w_in, s5_lambda_re, s5_lambda_im, s5_log_dt, s5_b_re, s5_b_im, s5_c_re, s5_c_im, s5_d, s5_w_glu, mlstm_i_bias, mlstm_f_bias, mlstm_norm, ret_log_rate, ret_norm, ssd_conv_w, ssd_conv_b, ssd_dt_bias, ssd_a_log, ssd_d, ssd_norm, w_out, norm2, w_mlp1, w_mlp2, final_norm):
    bp, seq = x_prompt.shape[0], x_prompt.shape[1]
    bd, dseq = x_sample.shape[0], x_sample.shape[1]
    ctx = _Path(bp, seq, 0, 0, False, state_layers=DEPTH)
    lat = _Path(bd, dseq, 1, 1, True)

    cvec = jnp.concatenate([c_ctx[None], c, jnp.zeros((8 - 1 - bd, D_MODEL), F32)], axis=0)
    mod3 = _ada(cvec, w_ada, b_ada).reshape(DEPTH * 8 * 6, 1, D_MODEL)

    w_in_p = _permute_w_in(w_in)
    w_out_p = w_out.astype(BF16).reshape(DEPTH, 4, GROUP_W, D_MODEL)
    w1 = w_mlp1.astype(BF16)
    w2 = w_mlp2.astype(BF16)
    wglu = s5_w_glu.astype(BF16)
    zeros4 = jnp.zeros((DEPTH, N_DIR, MLSTM_HEADS), F32)
    gate_bias = _gate_rows(mlstm_i_bias, mlstm_f_bias, ssd_dt_bias)
    alog_rows = _gate_rows(zeros4, zeros4, ssd_a_log)
    ret_lg = jnp.repeat(ret_log_rate, RET_DH, axis=-1)[:, :, None, :]
    ssd_d_row = jnp.repeat(ssd_d, SSD_P, axis=-1)[:, None, :]
    rope_tabs = _rope_tables(dseq)

    def states(path, layer):
        if path is ctx:
            n = path.n_seq
            z = lambda *s: jnp.zeros((n, N_DIR) + s, F32)
            return (z(1, S5_ST), z(1, S5_ST), z(MLSTM_HEADS, MLSTM_DH, MLSTM_DH),
                    z(MLSTM_HEADS, 1, MLSTM_DH), z(MLSTM_HEADS, 1, LANES),
                    z(RET_HEADS, RET_DH, RET_DH), z(SSD_HEADS, SSD_P, SSD_N))
        return (state_s5_re[:, layer].reshape(bd, N_DIR, 1, S5_ST),
                state_s5_im[:, layer].reshape(bd, N_DIR, 1, S5_ST),
                state_mlstm_c[:, layer],
                state_mlstm_n[:, layer][..., None, :],
                jnp.broadcast_to(state_mlstm_m[:, layer][..., None, None],
                                 (bd, N_DIR, MLSTM_HEADS, 1, LANES)),
                state_ret[:, layer], state_ssd[:, layer])

    xs = {ctx: x_prompt.reshape(ctx.rows, D_MODEL), lat: x_sample.reshape(lat.rows, D_MODEL)}
    final_shapes = [(1, S5_ST), (1, S5_ST), (MLSTM_HEADS, MLSTM_DH, MLSTM_DH), (MLSTM_HEADS, 1, MLSTM_DH),
                    (MLSTM_HEADS, 1, LANES), (RET_HEADS, RET_DH, RET_DH), (SSD_HEADS, SSD_P, SSD_N)]
    acc = {ctx: [jnp.zeros((bp, DEPTH, N_DIR) + s, F32) for s in final_shapes], lat: None}
    for layer in range(DEPTH):
        s5_prm = _s5_params(s5_lambda_re[layer], s5_lambda_im[layer], s5_log_dt[layer],
                            s5_b_re[layer], s5_b_im[layer], s5_c_re[layer], s5_c_im[layer])
        s5_post = (s5_d[layer][None], wglu[layer])
        for path in (ctx, lat):
            x = xs[path]
            st = states(path, layer)
            proj = _in_proj(x, mod3, norm1, w_in_p, layer, path)
            xbc = _ssd_conv(proj, ssd_conv_w[layer], ssd_conv_b[layer][None], path)
            a = acc[path]
            carry = (lambda lo, hi: tuple(a[lo:hi])) if a else (lambda lo, hi: ())
            slot = layer if a else 0
            seq_out = _fused_call(
                [_s5_part(proj, s5_prm, st[0], st[1], path, slot, carry(0, 2)),
                 _mlstm_state_part(proj, gate_bias[layer], st[2], st[3], st[4], path, slot, carry(2, 5)),
                 _ret_state_part(proj, ret_lg[layer], st[5], rope_tabs, path, slot, carry(5, 6)),
                 _ssd_state_part(proj, xbc, gate_bias[layer], alog_rows[layer], st[6], path, slot,
                                 carry(6, 7))],
                (path.n_seq, path.nc), "mix_seq")
            (s5f, s5b, s5r, s5i), m_out, r_out, d_out = seq_out
            if a:
                acc[path] = [s5r, s5i, *m_out[4:], r_out[2], d_out[2]]
            (yb,), (yc,), (yd,) = _fused_call(
                [_mlstm_out_part(proj, gate_bias[layer], mlstm_norm[layer][None], m_out[:4], path),
                 _ret_out_part(proj, ret_lg[layer], ret_norm[layer][None], r_out[:2], rope_tabs, path),
                 _ssd_out_part(proj, xbc, gate_bias[layer], alog_rows[layer], ssd_d_row[layer],
                               ssd_norm[layer][None], d_out[:2], path)],
                (path.n_chunks // CPS,), "mix_out", rounds=tuple((ci,) for ci in range(CPS)))
            x1, h2 = _out_proj(proj, (s5f, s5b), s5_post, (yb, yc, yd), x, mod3, norm2, w_out_p,
                               layer, path)
            xs[path] = _mlp(h2, x1, mod3, w1, w2, final_norm, layer, path, layer == DEPTH - 1)
    y_prompt = xs[ctx].reshape(bp, seq, D_MODEL)
    y_sample = xs[lat].reshape(bd, dseq, D_MODEL)
    s5r, s5i, mc, mn, mm, rs, dh = acc[ctx]
    return (y_prompt, y_sample,
            s5r.reshape(bp, DEPTH, N_DIR, S5_GROUPS, S5_P), s5i.reshape(bp, DEPTH, N_DIR, S5_GROUPS, S5_P),
            mc, mn[..., 0, :], mm[..., 0, 0], rs, dh)
```

```python
import functools
from typing import Callable, NamedTuple

import jax
import jax.numpy as jnp
from jax import lax
from jax.experimental import pallas as pl
from jax.experimental.pallas import tpu as pltpu

D_MODEL = 2048
DEPTH = 4
GRID_W = 64
N_DIR = 2
GROUP_W = D_MODEL // 4
S5_CH = 16
S5_GROUPS = GROUP_W // S5_CH
S5_P = 64
S5_ST = S5_GROUPS * S5_P
S5_CB = 4
S5_BW = S5_ST // S5_CB
S5_SEG = 8
S5_SLEN = 16
S5_ILV = 2
MLSTM_HEADS = 4
MLSTM_DH = GROUP_W // MLSTM_HEADS
RET_HEADS = 4
RET_DH = GROUP_W // RET_HEADS
SSD_HEADS = 8
SSD_P = GROUP_W // SSD_HEADS
SSD_GROUPS = 2
SSD_HPG = SSD_HEADS // SSD_GROUPS
SSD_N = 128
SSD_GW = SSD_HPG * SSD_P
SSD_CONV = 5
SSD_XBC = GROUP_W + 2 * SSD_GROUPS * SSD_N
D_FF = 4 * D_MODEL
CHUNK = 128
EPS = 1e-6
ROPE_BASE = 10000.0
D_IN = 10 * GROUP_W + 2 * N_DIR * MLSTM_HEADS + SSD_XBC + N_DIR * SSD_HEADS
LANES = 128
HALO = 8
D_IN_P = 10 * GROUP_W + SSD_XBC + N_DIR * LANES
COL_S5, COL_MQ, COL_MK, COL_MV, COL_MO, COL_RQ, COL_RK, COL_RV, COL_RG, COL_DZ = range(10)
COL_XBC = 10 * GROUP_W // SSD_XBC
COL_GATE = (10 * GROUP_W + SSD_XBC) // LANES
GATE_I, GATE_F, GATE_DT = 0, MLSTM_HEADS, 2 * MLSTM_HEADS
CPS = 4
ADA_TN = 1024
IN_PROJ_TM, IN_PROJ_TN = 1024, 1280
OUT_PROJ_TM = 512
MLP_TM, MLP_TF = 512, 1024
CONV_ROWS = 512
V7X_VMEM_BYTES = 64 * 1024 * 1024
VMEM_LIMIT = V7X_VMEM_BYTES // 8 * 7

F32 = jnp.float32
BF16 = jnp.bfloat16


def _bdot(a, b):
    return jnp.dot(a.astype(BF16), b.astype(BF16), preferred_element_type=F32)


def _bdot_nt(a, b):
    return lax.dot_general(a.astype(BF16), b.astype(BF16), (((1,), (1,)), ((), ())),
                           preferred_element_type=F32)


def _bdot_tn(a, b):
    return lax.dot_general(a.astype(BF16), b.astype(BF16), (((0,), (0,)), ((), ())),
                           preferred_element_type=F32)


def _split3(x):
    x1 = x.astype(BF16)
    r1 = x - x1.astype(F32)
    x2 = r1.astype(BF16)
    x3 = (r1 - x2.astype(F32)).astype(BF16)
    return x1, x2, x3


def _dot01_left(m01, x):
    return sum(jnp.dot(m01, p, preferred_element_type=F32) for p in _split3(x))


def _dot01_right(x, m01):
    return sum(jnp.dot(p, m01, preferred_element_type=F32) for p in _split3(x))


def _iota2():
    row = lax.broadcasted_iota(jnp.int32, (CHUNK, CHUNK), 0)
    col = lax.broadcasted_iota(jnp.int32, (CHUNK, CHUNK), 1)
    return row, col


def _dir_mask(di, row, col):
    return row >= col if di == 0 else row <= col


def _to01(mask):
    return jnp.where(mask, 1.0, 0.0).astype(BF16)


def _cparams(n_axes):
    return pltpu.CompilerParams(dimension_semantics=("arbitrary",) * n_axes,
                                vmem_limit_bytes=VMEM_LIMIT)


def _rms(x, g):
    return x * lax.rsqrt(jnp.mean(x * x, axis=-1, keepdims=True) + EPS) * g


def _head_norm(y, heads, dh):
    parts = []
    for h in range(heads):
        seg = y[:, h * dh:(h + 1) * dh]
        parts.append(seg * lax.rsqrt(jnp.mean(seg * seg, axis=-1, keepdims=True) + EPS))
    return jnp.concatenate(parts, axis=1)


class _Path:
    def __init__(self, n_seq, seq_len, mod_base, mod_per_seq, rope, state_layers=1):
        self.n_seq, self.seq_len, self.state_layers = n_seq, seq_len, state_layers
        self.mod_base, self.mod_per_seq, self.rope = mod_base, mod_per_seq, rope
        self.rows = n_seq * seq_len
        self.nc = seq_len // CHUNK
        self.n_chunks = self.rows // CHUNK


def _ada_kernel(c_ref, w_ref, b_ref, o_ref):
    c = c_ref[...]
    a = c * jax.nn.sigmoid(c)
    o_ref[...] = _bdot(a, w_ref[...]) + b_ref[...]


def _ada(cvec, w_ada, b_ada, tn=ADA_TN):
    n = w_ada.shape[-1]
    return pl.pallas_call(
        _ada_kernel,
        grid=(DEPTH, n // tn),
        in_specs=[pl.BlockSpec((8, D_MODEL), lambda l, j: (0, 0)),
                  pl.BlockSpec((None, D_MODEL, tn), lambda l, j: (l, 0, j)),
                  pl.BlockSpec((None, 1, tn), lambda l, j: (l, 0, j))],
        out_specs=pl.BlockSpec((None, 8, tn), lambda l, j: (l, 0, j)),
        out_shape=jax.ShapeDtypeStruct((DEPTH, 8, n), F32),
        compiler_params=_cparams(2),
        name="ada",
    )(cvec, w_ada, b_ada.reshape(DEPTH, 1, n))


def _mod_spec(layer, path, tm, which):
    def index(i, *_):
        r = path.mod_base + path.mod_per_seq * ((i * tm) // path.seq_len)
        return ((layer * 8 + r) * 6 + which, 0, 0)
    return pl.BlockSpec((None, 1, D_MODEL), index)


def _in_proj_kernel(x_ref, g_ref, sc_ref, sh_ref, w_ref, o_ref, h_scr):
    @pl.when(pl.program_id(1) == 0)
    def _():
        h = _rms(x_ref[...], g_ref[...]) * (1.0 + sc_ref[...]) + sh_ref[...]
        h_scr[...] = h.astype(BF16)
    o_ref[...] = jnp.dot(h_scr[...], w_ref[...], preferred_element_type=F32)


def _in_proj(x, mod3, norm1, w_in_p, layer, path, tm=IN_PROJ_TM, tn=IN_PROJ_TN):
    rows = x.shape[0]
    return pl.pallas_call(
        _in_proj_kernel,
        grid=(rows // tm, D_IN_P // tn),
        in_specs=[pl.BlockSpec((tm, D_MODEL), lambda i, j: (i, 0)),
                  pl.BlockSpec((None, 1, D_MODEL), lambda i, j: (layer, 0, 0)),
                  _mod_spec(layer, path, tm, 1),
                  _mod_spec(layer, path, tm, 0),
                  pl.BlockSpec((None, D_MODEL, tn), lambda i, j: (layer, 0, j))],
        out_specs=pl.BlockSpec((tm, tn), lambda i, j: (i, j)),
        out_shape=jax.ShapeDtypeStruct((rows, D_IN_P), F32),
        scratch_shapes=[pltpu.VMEM((tm, D_MODEL), BF16)],
        compiler_params=_cparams(2),
        name="in_proj",
    )(x, norm1.reshape(DEPTH, 1, D_MODEL), mod3, mod3, w_in_p)


def _chunk_of(di, n, nc):
    return n if di == 0 else nc - 1 - n


def _seq_row_spec(path, di, width, col_block):
    nc = path.nc
    return pl.BlockSpec((CHUNK, width), lambda s, n: (s * nc + _chunk_of(di, n, nc), col_block))


def _seq_gate_spec(path, di):
    return _seq_row_spec(path, di, LANES, COL_GATE + di)


def _seq_snap_spec(path, di, shape):
    nc = path.nc
    zeros = (0,) * len(shape)
    return pl.BlockSpec((None,) + shape, lambda s, n: (s * nc + _chunk_of(di, n, nc),) + zeros)


def _seq_state_spec(shape):
    zeros = (0,) * (len(shape) + 1)
    return pl.BlockSpec((None, N_DIR) + shape, lambda s, n: (s,) + zeros)


def _final_state_spec(shape, layer):
    zeros = (0,) * (len(shape) + 1)
    return pl.BlockSpec((None, None, N_DIR) + shape, lambda s, n: (s, layer) + zeros)


def _final_state_shape(path, shape):
    return jax.ShapeDtypeStruct((path.n_seq, path.state_layers, N_DIR) + shape, F32)


def _full_spec(shape):
    zeros = (0,) * len(shape)
    return pl.BlockSpec(shape, lambda *_: zeros)


def _out_row_spec(width, col_block):
    return pl.BlockSpec((CPS * CHUNK, width), lambda i: (i, col_block))


def _out_snap_spec(shape):
    zeros = (0,) * len(shape)
    return pl.BlockSpec((CPS,) + shape, lambda i: (i,) + zeros)


class _Part(NamedTuple):
    kernel: Callable
    in_specs: list
    out_specs: list
    out_shape: list
    scratch: list
    args: list
    carry: tuple = ()


def _fused_kernel(*refs, layout, rounds):
    n_in = sum(ni + nc for _, ni, nc, _, _ in layout)
    n_out = sum(no for _, _, _, no, _ in layout)
    ins, outs, scr = refs[:n_in], refs[n_in:n_in + n_out], refs[n_in + n_out:]
    for r, chunks in enumerate(rounds or (None,)):
        i = o = s = 0
        for body, ni, nc, no, ns in layout:
            kw = {} if chunks is None else dict(chunks=chunks, first_round=r == 0)
            body(*ins[i:i + ni], *outs[o:o + no], *scr[s:s + ns], **kw)
            i, o, s = i + ni + nc, o + no, s + ns


def _fused_call(parts, grid, name, rounds=None):
    layout = tuple((p.kernel, len(p.in_specs), len(p.carry), len(p.out_specs), len(p.scratch))
                   for p in parts)
    in_specs, args, aliases, n_out = [], [], {}, 0
    for p in parts:
        in_specs += p.in_specs
        args += p.args
        first_carried_out = n_out + len(p.out_specs) - len(p.carry)
        for c, arr in enumerate(p.carry):
            aliases[len(args)] = first_carried_out + c
            in_specs.append(pl.BlockSpec(memory_space=pl.ANY))
            args.append(arr)
        n_out += len(p.out_specs)
    outs = pl.pallas_call(
        functools.partial(_fused_kernel, layout=layout, rounds=rounds),
        grid=grid,
        in_specs=in_specs,
        out_specs=tuple(s for p in parts for s in p.out_specs),
        out_shape=tuple(s for p in parts for s in p.out_shape),
        scratch_shapes=[s for p in parts for s in p.scratch],
        input_output_aliases=aliases,
        compiler_params=_cparams(len(grid)),
        name=name,
    )(*args)
    split, o = [], 0
    for p in parts:
        split.append(outs[o:o + len(p.out_specs)])
        o += len(p.out_specs)
    return split


def _s5_kernel(uf_ref, ub_ref, bre_ref, bim_ref, cre_ref, cim_ref, pwr_ref, pwi_ref, h0r_ref, h0i_ref,
               yf_ref, yb_ref, hnr_ref, hni_ref,
               ut, yt, hlr, hli, st_r, st_i, *, nc):
    n = pl.program_id(1)

    @pl.when(n == 0)
    def _():
        st_r[...] = jnp.broadcast_to(h0r_ref[...], st_r.shape)
        st_i[...] = jnp.broadcast_to(h0i_ref[...], st_i.shape)

    u_refs = (uf_ref, ub_ref)
    y_refs = (yf_ref, yb_ref)
    seg = lax.broadcasted_iota(jnp.int32, (S5_SEG, S5_BW), 0)
    pos = (seg, S5_SEG - 1 - seg)

    def step_rows(di, kk):
        k = kk if di == 0 else S5_SLEN - 1 - kk
        return slice(k * S5_SEG, (k + 1) * S5_SEG)

    def lanes_of(cb):
        return slice(cb * S5_BW, (cb + 1) * S5_BW)

    for cb0 in range(0, S5_CB, S5_ILV):
        chains = [(di, cb) for cb in range(cb0, cb0 + S5_ILV) for di in range(N_DIR)]
        for di, cb in chains:
            ut[di, cb] = u_refs[di][:, cb * LANES:(cb + 1) * LANES]
            ucb = jnp.concatenate([ut[di, cb, pl.ds(k, S5_SEG, stride=S5_SLEN), :] for k in range(S5_SLEN)],
                                  axis=0).astype(BF16)
            hlr[di, cb] = jnp.dot(ucb, bre_ref[di, cb], preferred_element_type=F32)
            hli[di, cb] = jnp.dot(ucb, bim_ref[di, cb], preferred_element_type=F32)
        hr = {c: jnp.zeros((S5_SEG, S5_BW), F32) for c in chains}
        hi = {c: jnp.zeros((S5_SEG, S5_BW), F32) for c in chains}
        for kk in range(S5_SLEN):
            for c in chains:
                di, cb = c
                rows = step_rows(di, kk)
                lr = pwr_ref[di, 0, :, lanes_of(cb)]
                li = pwi_ref[di, 0, :, lanes_of(cb)]
                hr[c], hi[c] = (lr * hr[c] - li * hi[c] + hlr[di, cb, rows, :],
                                lr * hi[c] + li * hr[c] + hli[di, cb, rows, :])
                hlr[di, cb, rows, :] = hr[c]
                hli[di, cb, rows, :] = hi[c]
        cr, ci = {}, {}
        for c in chains:
            di, cb = c
            one = 1 if di == 0 else S5_SEG - 1
            first = pos[di] == 0
            cr[c] = jnp.where(first, pltpu.roll(st_r[di, :, lanes_of(cb)], one, 0),
                              pltpu.roll(hr[c], one, 0))
            ci[c] = jnp.where(first, pltpu.roll(st_i[di, :, lanes_of(cb)], one, 0),
                              pltpu.roll(hi[c], one, 0))
        for lvl in range(S5_SEG.bit_length() - 1):
            dist = 1 << lvl
            for c in chains:
                di, cb = c
                mr = pwr_ref[di, S5_SLEN - 1 + lvl, :, lanes_of(cb)]
                mi = pwi_ref[di, S5_SLEN - 1 + lvl, :, lanes_of(cb)]
                shift = dist if di == 0 else S5_SEG - dist
                sr = jnp.where(pos[di] >= dist, pltpu.roll(cr[c], shift, 0), 0.0)
                si = jnp.where(pos[di] >= dist, pltpu.roll(ci[c], shift, 0), 0.0)
                cr[c], ci[c] = cr[c] + mr * sr - mi * si, ci[c] + mr * si + mi * sr
        for c in chains:
            di, cb = c
            l16r = pwr_ref[di, S5_SLEN - 1, :, lanes_of(cb)]
            l16i = pwi_ref[di, S5_SLEN - 1, :, lanes_of(cb)]
            st_r[di, :, lanes_of(cb)] = hr[c] + l16r * cr[c] - l16i * ci[c]
            st_i[di, :, lanes_of(cb)] = hi[c] + l16r * ci[c] + l16i * cr[c]
        for kk in range(S5_SLEN):
            for c in chains:
                di, cb = c
                pr = pwr_ref[di, kk, :, lanes_of(cb)]
                pi = pwi_ref[di, kk, :, lanes_of(cb)]
                rows = step_rows(di, kk)
                hlr[di, cb, rows, :] = hlr[di, cb, rows, :] + pr * cr[c] - pi * ci[c]
                hli[di, cb, rows, :] = hli[di, cb, rows, :] + pr * ci[c] + pi * cr[c]
        for di, cb in chains:
            yt[di, cb] = (jnp.dot(hlr[di, cb].astype(BF16), cre_ref[di, cb], preferred_element_type=F32)
                          + jnp.dot(hli[di, cb].astype(BF16), cim_ref[di, cb], preferred_element_type=F32))
            per = S5_SLEN // S5_SEG
            for r in range(CHUNK // S5_SEG):
                start = (r % per) * S5_SEG * S5_SEG + r // per
                y_refs[di][r * S5_SEG:(r + 1) * S5_SEG, cb * LANES:(cb + 1) * LANES] = (
                    yt[di, cb, pl.ds(start, S5_SEG, stride=S5_SEG), :])

    @pl.when(n == nc - 1)
    def _():
        for di in range(N_DIR):
            last = S5_SEG - 1 if di == 0 else 0
            hnr_ref[di] = st_r[di, last:last + 1, :]
            hni_ref[di] = st_i[di, last:last + 1, :]


def _s5_params(lam_re, lam_im, log_dt, b_re, b_im, c_re, c_im):
    dt = jnp.exp(log_dt)[..., None]
    mag = jnp.exp(lam_re * dt)
    lbr, lbi = mag * jnp.cos(lam_im * dt), mag * jnp.sin(lam_im * dt)
    den = lam_re * lam_re + lam_im * lam_im
    fr = ((lbr - 1.0) * lam_re + lbi * lam_im) / den
    fi = (lbi * lam_re - (lbr - 1.0) * lam_im) / den
    bbr = fr[..., None] * b_re - fi[..., None] * b_im
    bbi = fr[..., None] * b_im + fi[..., None] * b_re
    gpb = S5_GROUPS // S5_CB
    eye = jnp.eye(gpb, dtype=F32)

    def pack_b(b):
        b = b.reshape(N_DIR, S5_CB, gpb, S5_P, S5_CH)
        m = jnp.einsum('dbgpc,gh->dbgchp', b, eye)
        return m.reshape(N_DIR, S5_CB, gpb * S5_CH, gpb * S5_P).astype(BF16)

    def pack_c(c):
        c = c.reshape(N_DIR, S5_CB, gpb, S5_CH, S5_P)
        m = jnp.einsum('dbgcp,gh->dbgphc', c, eye)
        return m.reshape(N_DIR, S5_CB, gpb * S5_P, gpb * S5_CH).astype(BF16)

    pr, pi = [lbr], [lbi]
    for _ in range(S5_SLEN - 1):
        pr, pi = pr + [pr[-1] * lbr - pi[-1] * lbi], pi + [pr[-1] * lbi + pi[-1] * lbr]
    for _ in range(S5_SEG.bit_length() - 2):
        pr, pi = pr + [pr[-1] * pr[-1] - pi[-1] * pi[-1]], pi + [2.0 * pr[-1] * pi[-1]]
    tab = lambda p: jnp.broadcast_to(jnp.stack(p, axis=1).reshape(N_DIR, len(p), 1, S5_ST),
                                     (N_DIR, len(p), S5_SEG, S5_ST))
    return pack_b(bbr), pack_b(bbi), pack_c(c_re), pack_c(-c_im), tab(pr), tab(pi)


def _s5_part(proj, prm, h0r, h0i, path, layer, carry):
    nc = path.nc
    bre, bim, cre, cim, pwr, pwi = prm
    yshape = jax.ShapeDtypeStruct((path.rows, GROUP_W), F32)
    stshape = _final_state_shape(path, (1, S5_ST))
    final = _final_state_spec((1, S5_ST), layer)
    yspec = lambda di: pl.BlockSpec((CHUNK, GROUP_W), lambda s, n: (s * nc + _chunk_of(di, n, nc), 0))
    dir_scr = lambda rows: pltpu.VMEM((N_DIR, S5_CB, rows, S5_BW), F32)
    return _Part(
        functools.partial(_s5_kernel, nc=nc),
        in_specs=[_seq_row_spec(path, 0, GROUP_W, COL_S5), _seq_row_spec(path, 1, GROUP_W, COL_S5),
                  _full_spec(bre.shape), _full_spec(bim.shape), _full_spec(cre.shape),
                  _full_spec(cim.shape), _full_spec(pwr.shape), _full_spec(pwi.shape),
                  _seq_state_spec((1, S5_ST)), _seq_state_spec((1, S5_ST))],
        out_specs=[yspec(0), yspec(1), final, final],
        out_shape=[yshape, yshape, stshape, stshape],
        scratch=[pltpu.VMEM((N_DIR, S5_CB, CHUNK, LANES), F32)] * 2
                + [dir_scr(CHUNK)] * 2 + [pltpu.VMEM((N_DIR, S5_SEG, S5_ST), F32)] * 2,
        args=[proj, proj, bre, bim, cre, cim, pwr, pwi, h0r, h0i],
        carry=carry)


def _gate_act(g):
    lane = lax.broadcasted_iota(jnp.int32, g.shape, 1)
    return jnp.where(lane < GATE_F, g, jnp.where(lane < GATE_DT, jax.nn.log_sigmoid(g), 0.0))


def _mlstm_state_kernel(kf_ref, vf_ref, gf_ref, kb_ref, vb_ref, gb_ref, bias_ref, c0_ref, n0_ref, m0_ref,
                        cf_ref, mf_ref, cb_ref, mb_ref, cn_ref, nn_ref, mn_ref, caug, mscr, *, nc):
    n = pl.program_id(1)
    dh = MLSTM_DH

    @pl.when(n == 0)
    def _():
        for di in range(N_DIR):
            for h in range(MLSTM_HEADS):
                caug[di, h, 0:dh, :] = c0_ref[di, h].T
                caug[di, h, dh:2 * dh, :] = jnp.broadcast_to(n0_ref[di, h], (dh, dh))
                mscr[di, h] = m0_ref[di, h]

    row, col = _iota2()
    ones = jnp.ones((CHUNK, dh), F32)
    dirs = ((kf_ref, vf_ref, gf_ref, cf_ref, mf_ref), (kb_ref, vb_ref, gb_ref, cb_ref, mb_ref))
    for di, (k_ref, v_ref, g_ref, c_out, m_out) in enumerate(dirs):
        p = _gate_act(g_ref[...] + bias_ref[di])
        bcol = _dot01_left(_to01(_dir_mask(di, row, col)), p)
        btot = jnp.sum(p, axis=0, keepdims=True)
        for h in range(MLSTM_HEADS):
            sl = slice(h * dh, (h + 1) * dh)
            k = k_ref[:, sl] * (dh ** -0.5)
            v = v_ref[:, sl]
            bc = bcol[:, GATE_F + h:GATE_F + h + 1]
            bt = btot[:, GATE_F + h:GATE_F + h + 1]
            igc = p[:, h:h + 1]
            mrow = mscr[di, h]
            mm = mrow[:, 0:1]
            cst = caug[di, h]
            c_out[h] = cst.astype(BF16)
            m_out[h] = mrow
            gcol = bt - bc + igc
            m_new = jnp.maximum(bt + mm, jnp.max(gcol, axis=0, keepdims=True))
            w_s = jnp.exp(gcol - m_new)
            decay_old = jnp.exp(bt + mm - m_new)
            vaug = jnp.concatenate([v, ones], axis=1)
            caug[di, h] = decay_old * cst + _bdot_tn(vaug, w_s * k)
            mscr[di, h] = jnp.broadcast_to(m_new, (1, LANES))

    @pl.when(n == nc - 1)
    def _():
        for di in range(N_DIR):
            for h in range(MLSTM_HEADS):
                cn_ref[di, h] = caug[di, h, 0:dh, :].T
                nn_ref[di, h] = caug[di, h, dh:dh + 1, :]
                mn_ref[di, h] = mscr[di, h]


def _mlstm_out_kernel(q_ref, k_ref, v_ref, o_ref, g0_ref, g1_ref, bias_ref, norm_ref,
                      cf_ref, mf_ref, cb_ref, mb_ref, y_ref, *, chunks, first_round):
    dh = MLSTM_DH
    row, col = _iota2()
    masks = [_dir_mask(di, row, col) for di in range(N_DIR)]
    masks_t = [masks[1 - di] for di in range(N_DIR)]
    m01 = [_to01(m) for m in masks]
    mt01 = [_to01(m) for m in masks_t]
    g_refs = (g0_ref, g1_ref)
    c_refs = (cf_ref, cb_ref)
    m_refs = (mf_ref, mb_ref)
    for ci in chunks:
        rs = slice(ci * CHUNK, (ci + 1) * CHUNK)
        ps, bcols, brows = [], [], []
        for di in range(N_DIR):
            p = _gate_act(g_refs[di][rs, :] + bias_ref[di])
            ps.append(p)
            bcols.append(_dot01_left(m01[di], p))
            brows.append(_dot01_right(p.T, mt01[di]))
        outs = []
        for h in range(MLSTM_HEADS):
            sl = slice(h * dh, (h + 1) * dh)
            q = q_ref[rs, sl].astype(BF16)
            k = (k_ref[rs, sl] * (dh ** -0.5)).astype(BF16)
            v = v_ref[rs, sl].astype(BF16)
            qk = _bdot_nt(k, q)
            wsum = None
            extra = None
            for di in range(N_DIR):
                fl = GATE_F + h
                a_s = ps[di][:, h:h + 1] - bcols[di][:, fl:fl + 1]
                b_t = brows[di][fl:fl + 1, :]
                mm = m_refs[di][ci, h]
                am = jnp.where(masks_t[di], a_s, -jnp.inf)
                big = jnp.maximum(mm, jnp.max(am, axis=0, keepdims=True))
                sc = qk * jnp.exp(am - big)
                w_inter = jnp.exp(mm - big)
                qc = _bdot_nt(c_refs[di][ci, h], q)
                den = jnp.sum(sc, axis=0, keepdims=True) + w_inter * qc[dh:dh + 1, :]
                r = 1.0 / jnp.maximum(jnp.abs(den), jnp.exp(-(b_t + big)))
                wsum = sc * r if wsum is None else wsum + sc * r
                e = qc[0:dh, :] * (w_inter * r)
                extra = e if extra is None else extra + e
            ht = _bdot_tn(v, wsum) + extra
            ht = ht * lax.rsqrt(jnp.mean(ht * ht, axis=0, keepdims=True) + EPS)
            outs.append(ht.T)
        hs = jnp.concatenate(outs, axis=1) * norm_ref[...]
        y_ref[rs, :] = (jax.nn.sigmoid(o_ref[rs, :]) * hs).astype(BF16)


def _mlstm_state_part(proj, bias, c0, n0, m0, path, layer, carry):
    nc = path.nc
    hh, dh = MLSTM_HEADS, MLSTM_DH
    csnap = jax.ShapeDtypeStruct((path.n_chunks, hh, 2 * dh, dh), BF16)
    msnap = jax.ShapeDtypeStruct((path.n_chunks, hh, 1, LANES), F32)
    seq_in = lambda di: [_seq_row_spec(path, di, GROUP_W, COL_MK), _seq_row_spec(path, di, GROUP_W, COL_MV),
                         _seq_gate_spec(path, di)]
    snap_out = lambda di: [_seq_snap_spec(path, di, (hh, 2 * dh, dh)), _seq_snap_spec(path, di, (hh, 1, LANES))]
    finals = [(hh, dh, dh), (hh, 1, dh), (hh, 1, LANES)]
    return _Part(
        functools.partial(_mlstm_state_kernel, nc=nc),
        in_specs=seq_in(0) + seq_in(1) + [_full_spec((N_DIR, 1, LANES)), _seq_state_spec((hh, dh, dh)),
                                         _seq_state_spec((hh, 1, dh)), _seq_state_spec((hh, 1, LANES))],
        out_specs=snap_out(0) + snap_out(1) + [_final_state_spec(s, layer) for s in finals],
        out_shape=[csnap, msnap, csnap, msnap] + [_final_state_shape(path, s) for s in finals],
        scratch=[pltpu.VMEM((N_DIR, hh, 2 * dh, dh), F32), pltpu.VMEM((N_DIR, hh, 1, LANES), F32)],
        args=[proj, proj, proj, proj, proj, proj, bias, c0, n0, m0],
        carry=carry)


def _mlstm_out_part(proj, bias, norm, snaps, path):
    hh, dh = MLSTM_HEADS, MLSTM_DH
    return _Part(
        _mlstm_out_kernel,
        in_specs=[_out_row_spec(GROUP_W, COL_MQ), _out_row_spec(GROUP_W, COL_MK),
                  _out_row_spec(GROUP_W, COL_MV), _out_row_spec(GROUP_W, COL_MO),
                  _out_row_spec(LANES, COL_GATE), _out_row_spec(LANES, COL_GATE + 1),
                  _full_spec((N_DIR, 1, LANES)), _full_spec((1, GROUP_W)),
                  _out_snap_spec((hh, 2 * dh, dh)), _out_snap_spec((hh, 1, LANES)),
                  _out_snap_spec((hh, 2 * dh, dh)), _out_snap_spec((hh, 1, LANES))],
        out_specs=[_out_row_spec(GROUP_W, 0)],
        out_shape=[jax.ShapeDtypeStruct((path.rows, GROUP_W), BF16)],
        scratch=[],
        args=[proj, proj, proj, proj, proj, proj, bias, norm] + list(snaps))


def _rope(a, cos, sin):
    return a * cos + pltpu.roll(a, RET_DH // 2, 1) * sin


def _ret_state_kernel(*refs, nc, rope):
    if rope:
        (kf_ref, vf_ref, kb_ref, vb_ref, lg_ref, s0_ref, cosf_ref, sinf_ref, cosb_ref, sinb_ref,
         sf_ref, sb_ref, sn_ref, sscr) = refs
        tabs = ((cosf_ref, sinf_ref), (cosb_ref, sinb_ref))
    else:
        kf_ref, vf_ref, kb_ref, vb_ref, lg_ref, s0_ref, sf_ref, sb_ref, sn_ref, sscr = refs
    n = pl.program_id(1)
    dh = RET_DH

    @pl.when(n == 0)
    def _():
        sscr[...] = s0_ref[...]

    t = lax.broadcasted_iota(jnp.int32, (CHUNK, dh), 0).astype(F32)
    for di, (k_ref, v_ref, s_out) in enumerate(((kf_ref, vf_ref, sf_ref), (kb_ref, vb_ref, sb_ref))):
        left = CHUNK - 1.0 - t if di == 0 else t
        for h in range(RET_HEADS):
            sl = slice(h * dh, (h + 1) * dh)
            k = k_ref[:, sl]
            if rope:
                k = _rope(k, tabs[di][0][...], tabs[di][1][...])
            k = k * (dh ** -0.5)
            lg = -jnp.exp(lg_ref[di, :, sl])
            st = sscr[di, h]
            s_out[h] = st.astype(BF16)
            sscr[di, h] = jnp.exp(CHUNK * lg) * st + _bdot((k * jnp.exp(left * lg)).T, v_ref[:, sl])

    @pl.when(n == nc - 1)
    def _():
        sn_ref[...] = sscr[...]


def _ret_out_kernel(*refs, rope, chunks, first_round):
    if rope:
        (q_ref, k_ref, v_ref, g_ref, lg_ref, norm_ref, sf_ref, sb_ref, cos_ref, sin_ref,
         y_ref, dsum, qdec) = refs
    else:
        q_ref, k_ref, v_ref, g_ref, lg_ref, norm_ref, sf_ref, sb_ref, y_ref, dsum, qdec = refs
    dh = RET_DH

    def build_tables():
        row, col = _iota2()
        absdiff = jnp.abs(row - col).astype(F32)
        t = row.astype(F32)
        for h in range(RET_HEADS):
            sl = slice(h * dh, (h + 1) * dh)
            lgf = -jnp.exp(lg_ref[0, :, sl])
            lgb = -jnp.exp(lg_ref[1, :, sl])
            both = jnp.exp(absdiff * jnp.where(row >= col, lgf, lgb))
            dsum[h] = jnp.where(row == col, 2.0, both)
            qdec[0, h] = jnp.exp((t + 1.0) * lgf)
            qdec[1, h] = jnp.exp((CHUNK - t) * lgb)

    if first_round:
        pl.when(pl.program_id(0) == 0)(build_tables)

    s_refs = (sf_ref, sb_ref)
    for ci in chunks:
        rs = slice(ci * CHUNK, (ci + 1) * CHUNK)
        outs = []
        for h in range(RET_HEADS):
            sl = slice(h * dh, (h + 1) * dh)
            q = q_ref[rs, sl]
            k = k_ref[rs, sl]
            if rope:
                cos, sin = cos_ref[rs, :], sin_ref[rs, :]
                q = _rope(q, cos, sin)
                k = _rope(k, cos, sin)
            q = q.astype(BF16)
            k = (k * (dh ** -0.5)).astype(BF16)
            y = _bdot(_bdot_nt(q, k) * dsum[h], v_ref[rs, sl])
            for di in range(N_DIR):
                y = y + qdec[di, h] * jnp.dot(q, s_refs[di][ci, h], preferred_element_type=F32)
            outs.append(y)
        ys = _head_norm(jnp.concatenate(outs, axis=1), RET_HEADS, dh) * norm_ref[...]
        gate = g_ref[rs, :]
        y_ref[rs, :] = (ys * (gate * jax.nn.sigmoid(gate))).astype(BF16)


def _rope_tables(seq_len):
    n_rows = seq_len // GRID_W
    rows = jnp.repeat(jnp.arange(n_rows, dtype=F32), GRID_W)
    cols = jnp.tile(jnp.arange(GRID_W, dtype=F32), n_rows)
    quarter = RET_DH // 4
    freqs = ROPE_BASE ** (-jnp.arange(quarter, dtype=F32) / quarter)
    ang = jnp.concatenate([rows[:, None] * freqs, cols[:, None] * freqs], axis=-1)
    cos, sin = jnp.cos(ang), jnp.sin(ang)
    return jnp.concatenate([cos, cos], axis=-1), jnp.concatenate([-sin, sin], axis=-1)


def _ret_state_part(proj, lg, s0, rope_tabs, path, layer, carry):
    nc = path.nc
    hh, dh = RET_HEADS, RET_DH
    snap = jax.ShapeDtypeStruct((path.n_chunks, hh, dh, dh), BF16)
    seq_in = lambda di: [_seq_row_spec(path, di, GROUP_W, COL_RK), _seq_row_spec(path, di, GROUP_W, COL_RV)]
    in_specs = seq_in(0) + seq_in(1) + [_full_spec((N_DIR, 1, GROUP_W)), _seq_state_spec((hh, dh, dh))]
    args = [proj, proj, proj, proj, lg, s0]
    if path.rope:
        for di in range(N_DIR):
            tab = pl.BlockSpec((CHUNK, dh), lambda s, n, di=di: (_chunk_of(di, n, nc), 0))
            in_specs += [tab, tab]
            args += list(rope_tabs)
    return _Part(
        functools.partial(_ret_state_kernel, nc=nc, rope=path.rope),
        in_specs=in_specs,
        out_specs=[_seq_snap_spec(path, 0, (hh, dh, dh)), _seq_snap_spec(path, 1, (hh, dh, dh)),
                   _final_state_spec((hh, dh, dh), layer)],
        out_shape=[snap, snap, _final_state_shape(path, (hh, dh, dh))],
        scratch=[pltpu.VMEM((N_DIR, hh, dh, dh), F32)],
        args=args,
        carry=carry)


def _ret_out_part(proj, lg, norm, snaps, rope_tabs, path):
    hh, dh = RET_HEADS, RET_DH
    in_specs = [_out_row_spec(GROUP_W, COL_RQ), _out_row_spec(GROUP_W, COL_RK),
                _out_row_spec(GROUP_W, COL_RV), _out_row_spec(GROUP_W, COL_RG),
                _full_spec((N_DIR, 1, GROUP_W)), _full_spec((1, GROUP_W)),
                _out_snap_spec((hh, dh, dh)), _out_snap_spec((hh, dh, dh))]
    args = [proj, proj, proj, proj, lg, norm] + list(snaps)
    if path.rope:
        per_seq = path.nc // CPS
        tab = pl.BlockSpec((CPS * CHUNK, dh), lambda i: (i % per_seq, 0))
        in_specs += [tab, tab]
        args += list(rope_tabs)
    return _Part(
        functools.partial(_ret_out_kernel, rope=path.rope),
        in_specs=in_specs,
        out_specs=[_out_row_spec(GROUP_W, 0)],
        out_shape=[jax.ShapeDtypeStruct((path.rows, GROUP_W), BF16)],
        scratch=[pltpu.VMEM((hh, CHUNK, CHUNK), F32), pltpu.VMEM((N_DIR, hh, CHUNK, dh), F32)],
        args=args)


def _conv_kernel(x_ref, xp_ref, xn_ref, cw_ref, cb_ref, o_ref, ext, *, bps, tr):
    ib = pl.program_id(0) % bps
    ext[0:HALO, :] = jnp.where(ib > 0, xp_ref[...], 0.0)
    ext[HALO:HALO + tr, :] = x_ref[...]
    ext[HALO + tr:, :] = jnp.where(ib < bps - 1, xn_ref[...], 0.0)
    pad = (SSD_CONV - 1) // 2
    acc = cb_ref[...] + cw_ref[0:1, :] * ext[HALO - pad:HALO - pad + tr, :]
    for j in range(1, SSD_CONV):
        acc = acc + cw_ref[j:j + 1, :] * ext[HALO - pad + j:HALO - pad + j + tr, :]
    o_ref[...] = acc * jax.nn.sigmoid(acc)


def _ssd_conv(proj, conv_w, conv_b, path):
    tr = min(CONV_ROWS, path.seq_len)
    per = tr // HALO
    n_halo = path.rows // HALO
    bps = path.seq_len // tr
    return pl.pallas_call(
        functools.partial(_conv_kernel, bps=bps, tr=tr),
        grid=(path.rows // tr,),
        in_specs=[pl.BlockSpec((tr, SSD_XBC), lambda i: (i, COL_XBC)),
                  pl.BlockSpec((HALO, SSD_XBC), lambda i: (jnp.maximum(i * per - 1, 0), COL_XBC)),
                  pl.BlockSpec((HALO, SSD_XBC), lambda i: (jnp.minimum((i + 1) * per, n_halo - 1), COL_XBC)),
                  _full_spec((SSD_CONV, SSD_XBC)), _full_spec((1, SSD_XBC))],
        out_specs=pl.BlockSpec((tr, SSD_XBC), lambda i: (i, 0)),
        out_shape=jax.ShapeDtypeStruct((path.rows, SSD_XBC), F32),
        scratch_shapes=[pltpu.VMEM((tr + 2 * HALO, SSD_XBC), F32)],
        compiler_params=_cparams(1),
        name="ssd_conv",
    )(proj, proj, proj, conv_w, conv_b)


def _ssd_rates(g, bias, alog):
    lane = lax.broadcasted_iota(jnp.int32, g.shape, 1)
    dt_lanes = (lane >= GATE_DT) & (lane < GATE_DT + SSD_HEADS)
    dt = jnp.where(dt_lanes, jax.nn.softplus(g + bias), 0.0)
    return dt, dt * -jnp.exp(alog)


def _ssd_state_kernel(xf_ref, bf_ref, gf_ref, xb_ref, bb_ref, gb_ref, bias_ref, alog_ref, h0_ref,
                      hf_ref, hb_ref, hn_ref, hscr, *, nc):
    n = pl.program_id(1)

    @pl.when(n == 0)
    def _():
        for di in range(N_DIR):
            for gi in range(SSD_GROUPS):
                hscr[di, gi] = h0_ref[di, gi * SSD_HPG:(gi + 1) * SSD_HPG].reshape(SSD_GW, SSD_N).T

    row, col = _iota2()
    dirs = ((xf_ref, bf_ref, gf_ref, hf_ref), (xb_ref, bb_ref, gb_ref, hb_ref))
    for di, (x_ref, b_ref, g_ref, h_out) in enumerate(dirs):
        dt, a = _ssd_rates(g_ref[...], bias_ref[di], alog_ref[di])
        ccol = _dot01_left(_to01(_dir_mask(di, row, col)), a)
        ctot = jnp.sum(a, axis=0, keepdims=True)
        for gi in range(SSD_GROUPS):
            hst = hscr[di, gi]
            h_out[gi] = hst.astype(BF16)
            wx_parts, tot_parts = [], []
            for r in range(SSD_HPG):
                hd = gi * SSD_HPG + r
                ln = GATE_DT + hd
                tot = ctot[:, ln:ln + 1]
                w = jnp.exp(tot - ccol[:, ln:ln + 1]) * dt[:, ln:ln + 1]
                wx_parts.append(w * x_ref[:, hd * SSD_P:(hd + 1) * SSD_P])
                tot_parts.append(jnp.broadcast_to(jnp.exp(tot), (1, SSD_P)))
            wx = jnp.concatenate(wx_parts, axis=1)
            bm = b_ref[:, gi * SSD_N:(gi + 1) * SSD_N]
            hscr[di, gi] = jnp.concatenate(tot_parts, axis=1) * hst + _bdot(bm.T, wx)

    @pl.when(n == nc - 1)
    def _():
        for di in range(N_DIR):
            for gi in range(SSD_GROUPS):
                hn_ref[di, gi * SSD_HPG:(gi + 1) * SSD_HPG] = hscr[di, gi].T.reshape(SSD_HPG, SSD_P, SSD_N)


def _ssd_out_kernel(z_ref, xbc_ref, g0_ref, g1_ref, bias_ref, alog_ref, dsk_ref, norm_ref,
                    hf_ref, hb_ref, y_ref, *, chunks, first_round):
    row, col = _iota2()
    masks = [_dir_mask(di, row, col) for di in range(N_DIR)]
    m01 = [_to01(m) for m in masks]
    mt01 = [_to01(masks[1 - di]) for di in range(N_DIR)]
    g_refs = (g0_ref, g1_ref)
    h_refs = (hf_ref, hb_ref)
    for ci in chunks:
        rs = slice(ci * CHUNK, (ci + 1) * CHUNK)
        dtts, ccols, crows = [], [], []
        for di in range(N_DIR):
            dt, a = _ssd_rates(g_refs[di][rs, :], bias_ref[di], alog_ref[di])
            dtts.append(dt.T)
            ccols.append(_dot01_left(m01[di], a))
            crows.append(_dot01_right(a.T, mt01[di]))
        xs = xbc_ref[rs, 0:GROUP_W]
        y_parts = []
        for gi in range(SSD_GROUPS):
            bm = xbc_ref[rs, GROUP_W + gi * SSD_N:GROUP_W + (gi + 1) * SSD_N].astype(BF16)
            c0 = GROUP_W + (SSD_GROUPS + gi) * SSD_N
            cm = xbc_ref[rs, c0:c0 + SSD_N].astype(BF16)
            cb = _bdot_nt(cm, bm)
            inter = [jnp.dot(cm, h_refs[di][ci, gi], preferred_element_type=F32) for di in range(N_DIR)]
            for r in range(SSD_HPG):
                hd = gi * SSD_HPG + r
                ln = GATE_DT + hd
                hs = slice(r * SSD_P, (r + 1) * SSD_P)
                wmat = None
                y = None
                for di in range(N_DIR):
                    cc = ccols[di][:, ln:ln + 1]
                    decay = jnp.exp(jnp.where(masks[di], cc - crows[di][ln:ln + 1, :], -jnp.inf))
                    wd = decay * dtts[di][ln:ln + 1, :]
                    wmat = wd if wmat is None else wmat + wd
                    e = jnp.exp(cc) * inter[di][:, hs]
                    y = e if y is None else y + e
                y_parts.append(y + _bdot(cb * wmat, xs[:, hd * SSD_P:(hd + 1) * SSD_P]))
        z = z_ref[rs, :]
        yy = (dsk_ref[...] * xs + jnp.concatenate(y_parts, axis=1)) * (z * jax.nn.sigmoid(z))
        y_ref[rs, :] = _rms(yy, norm_ref[...]).astype(BF16)


SSD_SNAP = (SSD_GROUPS, SSD_N, SSD_GW)


def _ssd_state_part(proj, xbc, bias, alog, h0, path, layer, carry):
    nc = path.nc
    st_shape = (SSD_HEADS, SSD_P, SSD_N)
    snap_shape = SSD_SNAP
    snap = jax.ShapeDtypeStruct((path.n_chunks,) + snap_shape, BF16)

    def seq_in(di):
        return [pl.BlockSpec((CHUNK, GROUP_W), lambda s, n: (s * nc + _chunk_of(di, n, nc), 0)),
                pl.BlockSpec((CHUNK, SSD_GROUPS * SSD_N),
                             lambda s, n: (s * nc + _chunk_of(di, n, nc), GROUP_W // (SSD_GROUPS * SSD_N))),
                _seq_gate_spec(path, di)]

    return _Part(
        functools.partial(_ssd_state_kernel, nc=nc),
        in_specs=seq_in(0) + seq_in(1) + [_full_spec((N_DIR, 1, LANES)), _full_spec((N_DIR, 1, LANES)),
                                         _seq_state_spec(st_shape)],
        out_specs=[_seq_snap_spec(path, 0, snap_shape), _seq_snap_spec(path, 1, snap_shape),
                   _final_state_spec(st_shape, layer)],
        out_shape=[snap, snap, _final_state_shape(path, st_shape)],
        scratch=[pltpu.VMEM((N_DIR,) + snap_shape, F32)],
        args=[xbc, xbc, proj, xbc, xbc, proj, bias, alog, h0],
        carry=carry)


def _ssd_out_part(proj, xbc, bias, alog, dskip, norm, snaps, path):
    return _Part(
        _ssd_out_kernel,
        in_specs=[_out_row_spec(GROUP_W, COL_DZ), _out_row_spec(SSD_XBC, 0),
                  _out_row_spec(LANES, COL_GATE), _out_row_spec(LANES, COL_GATE + 1),
                  _full_spec((N_DIR, 1, LANES)), _full_spec((N_DIR, 1, LANES)),
                  _full_spec((1, GROUP_W)), _full_spec((1, GROUP_W)),
                  _out_snap_spec(SSD_SNAP), _out_snap_spec(SSD_SNAP)],
        out_specs=[_out_row_spec(GROUP_W, 0)],
        out_shape=[jax.ShapeDtypeStruct((path.rows, GROUP_W), BF16)],
        scratch=[],
        args=[proj, xbc, proj, proj, bias, alog, dskip, norm] + list(snaps))


def _out_proj_kernel(u_ref, yf_ref, yb_ref, dsk_ref, wglu_ref, yb2_ref, yc_ref, yd_ref, w_ref, x_ref,
                     g1_ref, g_ref, sc_ref, sh_ref, x1_ref, h2_ref):
    t = jax.nn.gelu(dsk_ref[...] * u_ref[...] + yf_ref[...] + yb_ref[...])
    ya = (t * jax.nn.sigmoid(_bdot(t, wglu_ref[...]))).astype(BF16)
    acc = jnp.dot(ya, w_ref[0], preferred_element_type=F32)
    for i, y_ref in enumerate((yb2_ref, yc_ref, yd_ref)):
        acc = acc + jnp.dot(y_ref[...], w_ref[i + 1], preferred_element_type=F32)
    x1 = x_ref[...] + g1_ref[...] * acc
    x1_ref[...] = x1
    h2_ref[...] = (_rms(x1, g_ref[...]) * (1.0 + sc_ref[...]) + sh_ref[...]).astype(BF16)


def _out_proj(proj, s5_y, s5_post, ys, x, mod3, norm2, w_out_p, layer, path, tm=OUT_PROJ_TM):
    rows = x.shape[0]
    yspec = pl.BlockSpec((tm, GROUP_W), lambda i: (i, 0))
    xspec = pl.BlockSpec((tm, D_MODEL), lambda i: (i, 0))
    dskip, wglu = s5_post
    return pl.pallas_call(
        _out_proj_kernel,
        grid=(rows // tm,),
        in_specs=[pl.BlockSpec((tm, GROUP_W), lambda i: (i, COL_S5)), yspec, yspec,
                  _full_spec((1, GROUP_W)), _full_spec((GROUP_W, GROUP_W)),
                  yspec, yspec, yspec,
                  pl.BlockSpec((None, 4, GROUP_W, D_MODEL), lambda i: (layer, 0, 0, 0)),
                  xspec,
                  _mod_spec(layer, path, tm, 2),
                  pl.BlockSpec((None, 1, D_MODEL), lambda i: (layer, 0, 0)),
                  _mod_spec(layer, path, tm, 4),
                  _mod_spec(layer, path, tm, 3)],
        out_specs=(xspec, xspec),
        out_shape=(jax.ShapeDtypeStruct((rows, D_MODEL), F32),
                   jax.ShapeDtypeStruct((rows, D_MODEL), BF16)),
        compiler_params=_cparams(1),
        name="out_proj",
    )(proj, s5_y[0], s5_y[1], dskip, wglu, *ys, w_out_p, x, mod3, norm2.reshape(DEPTH, 1, D_MODEL),
      mod3, mod3)


def _mlp_kernel(h2_ref, x1_ref, w1_ref, w2_ref, g2_ref, fn_ref, o_ref, *, nk, final):
    k = pl.program_id(1)
    a = jnp.dot(h2_ref[...], w1_ref[...], preferred_element_type=F32)
    a = jnp.square(jnp.maximum(a, 0.0))
    part = g2_ref[...] * jnp.dot(a.astype(BF16), w2_ref[...], preferred_element_type=F32)

    @pl.when(k == 0)
    def _():
        o_ref[...] = x1_ref[...] + part

    @pl.when(k > 0)
    def _():
        o_ref[...] += part

    if final:
        @pl.when(k == nk - 1)
        def _():
            o_ref[...] = _rms(o_ref[...], fn_ref[...])


def _mlp(h2, x1, mod3, w1, w2, final_norm, layer, path, final, tm=MLP_TM, tf=MLP_TF):
    rows = x1.shape[0]
    nk = D_FF // tf
    xspec = pl.BlockSpec((tm, D_MODEL), lambda i, k: (i, 0))
    return pl.pallas_call(
        functools.partial(_mlp_kernel, nk=nk, final=final),
        grid=(rows // tm, nk),
        in_specs=[xspec, xspec,
                  pl.BlockSpec((None, D_MODEL, tf), lambda i, k: (layer, 0, k)),
                  pl.BlockSpec((None, tf, D_MODEL), lambda i, k: (layer, k, 0)),
                  _mod_spec(layer, path, tm, 5),
                  pl.BlockSpec((1, D_MODEL), lambda i, k: (0, 0))],
        out_specs=xspec,
        out_shape=jax.ShapeDtypeStruct((rows, D_MODEL), F32),
        compiler_params=_cparams(2),
        name="mlp",
    )(h2, x1, w1, w2, mod3, final_norm.reshape(1, D_MODEL))


def _permute_w_in(w_in):
    o_mi = 5 * GROUP_W
    o_mf = o_mi + N_DIR * MLSTM_HEADS
    o_r = o_mf + N_DIR * MLSTM_HEADS
    o_dt = D_IN - N_DIR * SSD_HEADS
    zeros = jnp.zeros(w_in.shape[:2] + (LANES - 2 * MLSTM_HEADS - SSD_HEADS,), w_in.dtype)
    parts = [w_in[..., :o_mi], w_in[..., o_r:o_dt]]
    for di in range(N_DIR):
        parts += [w_in[..., o_mi + di * MLSTM_HEADS:o_mi + (di + 1) * MLSTM_HEADS],
                  w_in[..., o_mf + di * MLSTM_HEADS:o_mf + (di + 1) * MLSTM_HEADS],
                  w_in[..., o_dt + di * SSD_HEADS:o_dt + (di + 1) * SSD_HEADS], zeros]
    return jnp.concatenate(parts, axis=-1).astype(BF16)


def _gate_rows(i_part, f_part, dt_part):
    pad = jnp.zeros(i_part.shape[:2] + (LANES - 2 * MLSTM_HEADS - SSD_HEADS,), F32)
    return jnp.concatenate([i_part, f_part, dt_part, pad], axis=-1)[:, :, None, :]


def kernel(x_prompt, x_sample, c, state_s5_re, state_s5_im, state_mlstm_c, state_mlstm_n, state_mlstm_m, state_ret, state_ssd, c_ctx, norm1, w_ada, b_ada, w_in, s5_lambda_re, s5_lambda_im, s5_log_dt, s5_b_re, s5_b_im, s5_c_re, s5_c_im, s5_d, s5_w_glu, mlstm_i_bias, mlstm_f_bias, mlstm_norm, ret_log_rate, ret_norm, ssd_conv_w, ssd_conv_b, ssd_dt_bias, ssd_a_log, ssd_d, ssd_norm, w_out, norm2, w_mlp1, w_mlp2, final_norm):
    bp, seq = x_prompt.shape[0], x_prompt.shape[1]
    bd, dseq = x_sample.shape[0], x_sample.shape[1]
    ctx = _Path(bp, seq, 0, 0, False, state_layers=DEPTH)
    lat = _Path(bd, dseq, 1, 1, True)

    cvec = jnp.concatenate([c_ctx[None], c, jnp.zeros((8 - 1 - bd, D_MODEL), F32)], axis=0)
    mod3 = _ada(cvec, w_ada, b_ada).reshape(DEPTH * 8 * 6, 1, D_MODEL)

    w_in_p = _permute_w_in(w_in)
    w_out_p = w_out.astype(BF16).reshape(DEPTH, 4, GROUP_W, D_MODEL)
    w1 = w_mlp1.astype(BF16)
    w2 = w_mlp2.astype(BF16)
    wglu = s5_w_glu.astype(BF16)
    zeros4 = jnp.zeros((DEPTH, N_DIR, MLSTM_HEADS), F32)
    gate_bias = _gate_rows(mlstm_i_bias, mlstm_f_bias, ssd_dt_bias)
    alog_rows = _gate_rows(zeros4, zeros4, ssd_a_log)
    ret_lg = jnp.repeat(ret_log_rate, RET_DH, axis=-1)[:, :, None, :]
    ssd_d_row = jnp.repeat(ssd_d, SSD_P, axis=-1)[:, None, :]
    rope_tabs = _rope_tables(dseq)

    def states(path, layer):
        if path is ctx:
            n = path.n_seq
            z = lambda *s: jnp.zeros((n, N_DIR) + s, F32)
            return (z(1, S5_ST), z(1, S5_ST), z(MLSTM_HEADS, MLSTM_DH, MLSTM_DH),
                    z(MLSTM_HEADS, 1, MLSTM_DH), z(MLSTM_HEADS, 1, LANES),
                    z(RET_HEADS, RET_DH, RET_DH), z(SSD_HEADS, SSD_P, SSD_N))
        return (state_s5_re[:, layer].reshape(bd, N_DIR, 1, S5_ST),
                state_s5_im[:, layer].reshape(bd, N_DIR, 1, S5_ST),
                state_mlstm_c[:, layer],
                state_mlstm_n[:, layer][..., None, :],
                jnp.broadcast_to(state_mlstm_m[:, layer][..., None, None],
                                 (bd, N_DIR, MLSTM_HEADS, 1, LANES)),
                state_ret[:, layer], state_ssd[:, layer])

    xs = {ctx: x_prompt.reshape(ctx.rows, D_MODEL), lat: x_sample.reshape(lat.rows, D_MODEL)}
    final_shapes = [(1, S5_ST), (1, S5_ST), (MLSTM_HEADS, MLSTM_DH, MLSTM_DH), (MLSTM_HEADS, 1, MLSTM_DH),
                    (MLSTM_HEADS, 1, LANES), (RET_HEADS, RET_DH, RET_DH), (SSD_HEADS, SSD_P, SSD_N)]
    acc = {ctx: [jnp.zeros((bp, DEPTH, N_DIR) + s, F32) for s in final_shapes], lat: None}
    for layer in range(DEPTH):
        s5_prm = _s5_params(s5_lambda_re[layer], s5_lambda_im[layer], s5_log_dt[layer],
                            s5_b_re[layer], s5_b_im[layer], s5_c_re[layer], s5_c_im[layer])
        s5_post = (s5_d[layer][None], wglu[layer])
        for path in (ctx, lat):
            x = xs[path]
            st = states(path, layer)
            proj = _in_proj(x, mod3, norm1, w_in_p, layer, path)
            xbc = _ssd_conv(proj, ssd_conv_w[layer], ssd_conv_b[layer][None], path)
            a = acc[path]
            carry = (lambda lo, hi: tuple(a[lo:hi])) if a else (lambda lo, hi: ())
            slot = layer if a else 0
            seq_out = _fused_call(
                [_s5_part(proj, s5_prm, st[0], st[1], path, slot, carry(0, 2)),
                 _mlstm_state_part(proj, gate_bias[layer], st[2], st[3], st[4], path, slot, carry(2, 5)),
                 _ret_state_part(proj, ret_lg[layer], st[5], rope_tabs, path, slot, carry(5, 6)),
                 _ssd_state_part(proj, xbc, gate_bias[layer], alog_rows[layer], st[6], path, slot,
                                 carry(6, 7))],
                (path.n_seq, path.nc), "mix_seq")
            (s5f, s5b, s5r, s5i), m_out, r_out, d_out = seq_out
            if a:
                acc[path] = [s5r, s5i, *m_out[4:], r_out[2], d_out[2]]
            (yb,), (yc,), (yd,) = _fused_call(
                [_mlstm_out_part(proj, gate_bias[layer], mlstm_norm[layer][None], m_out[:4], path),
                 _ret_out_part(proj, ret_lg[layer], ret_norm[layer][None], r_out[:2], rope_tabs, path),
                 _ssd_out_part(proj, xbc, gate_bias[layer], alog_rows[layer], ssd_d_row[layer],
                               ssd_norm[layer][None], d_out[:2], path)],
                (path.n_chunks // CPS,), "mix_out", rounds=tuple((ci,) for ci in range(CPS)))
            x1, h2 = _out_proj(proj, (s5f, s5b), s5_post, (yb, yc, yd), x, mod3, norm2, w_out_p,
                               layer, path)
            xs[path] = _mlp(h2, x1, mod3, w1, w2, final_norm, layer, path, layer == DEPTH - 1)
    y_prompt = xs[ctx].reshape(bp, seq, D_MODEL)
    y_sample = xs[lat].reshape(bd, dseq, D_MODEL)
    s5r, s5i, mc, mn, mm, rs, dh = acc[ctx]
    return (y_prompt, y_sample,
            s5r.reshape(bp, DEPTH, N_DIR, S5_GROUPS, S5_P), s5i.reshape(bp, DEPTH, N_DIR, S5_GROUPS, S5_P),
            mc, mn[..., 0, :], mm[..., 0, 0], rs, dh)
```

```python
import functools
from typing import Callable, NamedTuple

import jax
import jax.numpy as jnp
from jax import lax
from jax.experimental import pallas as pl
from jax.experimental.pallas import tpu as pltpu

D_MODEL = 2048
DEPTH = 4
GRID_W = 64
N_DIR = 2
GROUP_W = D_MODEL // 4
S5_CH = 16
S5_GROUPS = GROUP_W // S5_CH
S5_P = 64
S5_ST = S5_GROUPS * S5_P
S5_CB = 4
S5_BW = S5_ST // S5_CB
S5_SEG = 8
S5_SLEN = 16
S5_ILV = 2
MLSTM_HEADS = 4
MLSTM_DH = GROUP_W // MLSTM_HEADS
RET_HEADS = 4
RET_DH = GROUP_W // RET_HEADS
SSD_HEADS = 8
SSD_P = GROUP_W // SSD_HEADS
SSD_GROUPS = 2
SSD_HPG = SSD_HEADS // SSD_GROUPS
SSD_N = 128
SSD_GW = SSD_HPG * SSD_P
SSD_CONV = 5
SSD_XBC = GROUP_W + 2 * SSD_GROUPS * SSD_N
D_FF = 4 * D_MODEL
CHUNK = 128
EPS = 1e-6
ROPE_BASE = 10000.0
D_IN = 10 * GROUP_W + 2 * N_DIR * MLSTM_HEADS + SSD_XBC + N_DIR * SSD_HEADS
LANES = 128
HALO = 8
D_IN_P = 10 * GROUP_W + SSD_XBC + N_DIR * LANES
COL_S5, COL_MQ, COL_MK, COL_MV, COL_MO, COL_RQ, COL_RK, COL_RV, COL_RG, COL_DZ = range(10)
COL_XBC = 10 * GROUP_W // SSD_XBC
COL_GATE = (10 * GROUP_W + SSD_XBC) // LANES
GATE_I, GATE_F, GATE_DT = 0, MLSTM_HEADS, 2 * MLSTM_HEADS
CPS = 4
ADA_TN = 1024
IN_PROJ_TM, IN_PROJ_TN = 1024, 1280
OUT_PROJ_TM = 512
MLP_TM, MLP_TF = 512, 1024
CONV_ROWS = 512
V7X_VMEM_BYTES = 64 * 1024 * 1024
VMEM_LIMIT = V7X_VMEM_BYTES // 8 * 7

F32 = jnp.float32
BF16 = jnp.bfloat16


def _bdot(a, b):
    return jnp.dot(a.astype(BF16), b.astype(BF16), preferred_element_type=F32)


def _bdot_nt(a, b):
    return lax.dot_general(a.astype(BF16), b.astype(BF16), (((1,), (1,)), ((), ())),
                           preferred_element_type=F32)


def _bdot_tn(a, b):
    return lax.dot_general(a.astype(BF16), b.astype(BF16), (((0,), (0,)), ((), ())),
                           preferred_element_type=F32)


def _split3(x):
    x1 = x.astype(BF16)
    r1 = x - x1.astype(F32)
    x2 = r1.astype(BF16)
    x3 = (r1 - x2.astype(F32)).astype(BF16)
    return x1, x2, x3


def _dot01_left(m01, x):
    return sum(jnp.dot(m01, p, preferred_element_type=F32) for p in _split3(x))


def _dot01_right(x, m01):
    return sum(jnp.dot(p, m01, preferred_element_type=F32) for p in _split3(x))


def _iota2():
    row = lax.broadcasted_iota(jnp.int32, (CHUNK, CHUNK), 0)
    col = lax.broadcasted_iota(jnp.int32, (CHUNK, CHUNK), 1)
    return row, col


def _dir_mask(di, row, col):
    return row >= col if di == 0 else row <= col


def _to01(mask):
    return jnp.where(mask, 1.0, 0.0).astype(BF16)


def _cparams(n_axes):
    return pltpu.CompilerParams(dimension_semantics=("arbitrary",) * n_axes,
                                vmem_limit_bytes=VMEM_LIMIT)


def _rms(x, g):
    return x * lax.rsqrt(jnp.mean(x * x, axis=-1, keepdims=True) + EPS) * g


def _head_norm(y, heads, dh):
    parts = []
    for h in range(heads):
        seg = y[:, h * dh:(h + 1) * dh]
        parts.append(seg * lax.rsqrt(jnp.mean(seg * seg, axis=-1, keepdims=True) + EPS))
    return jnp.concatenate(parts, axis=1)


class _Path:
    def __init__(self, n_seq, seq_len, mod_base, mod_per_seq, rope, state_layers=1):
        self.n_seq, self.seq_len, self.state_layers = n_seq, seq_len, state_layers
        self.mod_base, self.mod_per_seq, self.rope = mod_base, mod_per_seq, rope
        self.rows = n_seq * seq_len
        self.nc = seq_len // CHUNK
        self.n_chunks = self.rows // CHUNK


def _ada_kernel(c_ref, w_ref, b_ref, o_ref):
    c = c_ref[...]
    a = c * jax.nn.sigmoid(c)
    o_ref[...] = _bdot(a, w_ref[...]) + b_ref[...]


def _ada(cvec, w_ada, b_ada, tn=ADA_TN):
    n = w_ada.shape[-1]
    return pl.pallas_call(
        _ada_kernel,
        grid=(DEPTH, n // tn),
        in_specs=[pl.BlockSpec((8, D_MODEL), lambda l, j: (0, 0)),
                  pl.BlockSpec((None, D_MODEL, tn), lambda l, j: (l, 0, j)),
                  pl.BlockSpec((None, 1, tn), lambda l, j: (l, 0, j))],
        out_specs=pl.BlockSpec((None, 8, tn), lambda l, j: (l, 0, j)),
        out_shape=jax.ShapeDtypeStruct((DEPTH, 8, n), F32),
        compiler_params=_cparams(2),
        name="ada",
    )(cvec, w_ada, b_ada.reshape(DEPTH, 1, n))


def _mod_spec(layer, path, tm, which):
    def index(i, *_):
        r = path.mod_base + path.mod_per_seq * ((i * tm) // path.seq_len)
        return ((layer * 8 + r) * 6 + which, 0, 0)
    return pl.BlockSpec((None, 1, D_MODEL), index)


def _in_proj_kernel(x_ref, g_ref, sc_ref, sh_ref, w_ref, o_ref, h_scr):
    @pl.when(pl.program_id(1) == 0)
    def _():
        h = _rms(x_ref[...], g_ref[...]) * (1.0 + sc_ref[...]) + sh_ref[...]
        h_scr[...] = h.astype(BF16)
    o_ref[...] = jnp.dot(h_scr[...], w_ref[...], preferred_element_type=F32)


def _in_proj(x, mod3, norm1, w_in_p, layer, path, tm=IN_PROJ_TM, tn=IN_PROJ_TN):
    rows = x.shape[0]
    return pl.pallas_call(
        _in_proj_kernel,
        grid=(rows // tm, D_IN_P // tn),
        in_specs=[pl.BlockSpec((tm, D_MODEL), lambda i, j: (i, 0)),
                  pl.BlockSpec((None, 1, D_MODEL), lambda i, j: (layer, 0, 0)),
                  _mod_spec(layer, path, tm, 1),
                  _mod_spec(layer, path, tm, 0),
                  pl.BlockSpec((None, D_MODEL, tn), lambda i, j: (layer, 0, j))],
        out_specs=pl.BlockSpec((tm, tn), lambda i, j: (i, j)),
        out_shape=jax.ShapeDtypeStruct((rows, D_IN_P), F32),
        scratch_shapes=[pltpu.VMEM((tm, D_MODEL), BF16)],
        compiler_params=_cparams(2),
        name="in_proj",
    )(x, norm1.reshape(DEPTH, 1, D_MODEL), mod3, mod3, w_in_p)


def _chunk_of(di, n, nc):
    return n if di == 0 else nc - 1 - n


def _seq_row_spec(path, di, width, col_block):
    nc = path.nc
    return pl.BlockSpec((CHUNK, width), lambda s, n: (s * nc + _chunk_of(di, n, nc), col_block))


def _seq_gate_spec(path, di):
    return _seq_row_spec(path, di, LANES, COL_GATE + di)


def _seq_snap_spec(path, di, shape):
    nc = path.nc
    zeros = (0,) * len(shape)
    return pl.BlockSpec((None,) + shape, lambda s, n: (s * nc + _chunk_of(di, n, nc),) + zeros)


def _seq_state_spec(shape):
    zeros = (0,) * (len(shape) + 1)
    return pl.BlockSpec((None, N_DIR) + shape, lambda s, n: (s,) + zeros)


def _final_state_spec(shape, layer):
    zeros = (0,) * (len(shape) + 1)
    return pl.BlockSpec((None, None, N_DIR) + shape, lambda s, n: (s, layer) + zeros)


def _final_state_shape(path, shape):
    return jax.ShapeDtypeStruct((path.n_seq, path.state_layers, N_DIR) + shape, F32)


def _full_spec(shape):
    zeros = (0,) * len(shape)
    return pl.BlockSpec(shape, lambda *_: zeros)


def _out_row_spec(width, col_block):
    return pl.BlockSpec((CPS * CHUNK, width), lambda i: (i, col_block))


def _out_snap_spec(shape):
    zeros = (0,) * len(shape)
    return pl.BlockSpec((CPS,) + shape, lambda i: (i,) + zeros)


class _Part(NamedTuple):
    kernel: Callable
    in_specs: list
    out_specs: list
    out_shape: list
    scratch: list
    args: list
    carry: tuple = ()


def _fused_kernel(*refs, layout, rounds):
    n_in = sum(ni + nc for _, ni, nc, _, _ in layout)
    n_out = sum(no for _, _, _, no, _ in layout)
    ins, outs, scr = refs[:n_in], refs[n_in:n_in + n_out], refs[n_in + n_out:]
    for r, chunks in enumerate(rounds or (None,)):
        i = o = s = 0
        for body, ni, nc, no, ns in layout:
            kw = {} if chunks is None else dict(chunks=chunks, first_round=r == 0)
            body(*ins[i:i + ni], *outs[o:o + no], *scr[s:s + ns], **kw)
            i, o, s = i + ni + nc, o + no, s + ns


def _fused_call(parts, grid, name, rounds=None):
    layout = tuple((p.kernel, len(p.in_specs), len(p.carry), len(p.out_specs), len(p.scratch))
                   for p in parts)
    in_specs, args, aliases, n_out = [], [], {}, 0
    for p in parts:
        in_specs += p.in_specs
        args += p.args
        first_carried_out = n_out + len(p.out_specs) - len(p.carry)
        for c, arr in enumerate(p.carry):
            aliases[len(args)] = first_carried_out + c
            in_specs.append(pl.BlockSpec(memory_space=pl.ANY))
            args.append(arr)
        n_out += len(p.out_specs)
    outs = pl.pallas_call(
        functools.partial(_fused_kernel, layout=layout, rounds=rounds),
        grid=grid,
        in_specs=in_specs,
        out_specs=tuple(s for p in parts for s in p.out_specs),
        out_shape=tuple(s for p in parts for s in p.out_shape),
        scratch_shapes=[s for p in parts for s in p.scratch],
        input_output_aliases=aliases,
        compiler_params=_cparams(len(grid)),
        name=name,
    )(*args)
    split, o = [], 0
    for p in parts:
        split.append(outs[o:o + len(p.out_specs)])
        o += len(p.out_specs)
    return split


def _s5_kernel(uf_ref, ub_ref, bre_ref, bim_ref, cre_ref, cim_ref, pwr_ref, pwi_ref, h0r_ref, h0i_ref,
               yf_ref, yb_ref, hnr_ref, hni_ref,
               ut, yt, hlr, hli, st_r, st_i, *, nc):
    n = pl.program_id(1)

    @pl.when(n == 0)
    def _():
        st_r[...] = jnp.broadcast_to(h0r_ref[...], st_r.shape)
        st_i[...] = jnp.broadcast_to(h0i_ref[...], st_i.shape)

    u_refs = (uf_ref, ub_ref)
    y_refs = (yf_ref, yb_ref)
    seg = lax.broadcasted_iota(jnp.int32, (S5_SEG, S5_BW), 0)
    pos = (seg, S5_SEG - 1 - seg)

    def step_rows(di, kk):
        k = kk if di == 0 else S5_SLEN - 1 - kk
        return slice(k * S5_SEG, (k + 1) * S5_SEG)

    def lanes_of(cb):
        return slice(cb * S5_BW, (cb + 1) * S5_BW)

    for cb0 in range(0, S5_CB, S5_ILV):
        chains = [(di, cb) for cb in range(cb0, cb0 + S5_ILV) for di in range(N_DIR)]
        for di, cb in chains:
            ut[di, cb] = u_refs[di][:, cb * LANES:(cb + 1) * LANES]
            ucb = jnp.concatenate([ut[di, cb, pl.ds(k, S5_SEG, stride=S5_SLEN), :] for k in range(S5_SLEN)],
                                  axis=0).astype(BF16)
            hlr[di, cb] = jnp.dot(ucb, bre_ref[di, cb], preferred_element_type=F32)
            hli[di, cb] = jnp.dot(ucb, bim_ref[di, cb], preferred_element_type=F32)
        hr = {c: jnp.zeros((S5_SEG, S5_BW), F32) for c in chains}
        hi = {c: jnp.zeros((S5_SEG, S5_BW), F32) for c in chains}
        for kk in range(S5_SLEN):
            for c in chains:
                di, cb = c
                rows = step_rows(di, kk)
                lr = pwr_ref[di, 0, :, lanes_of(cb)]
                li = pwi_ref[di, 0, :, lanes_of(cb)]
                hr[c], hi[c] = (lr * hr[c] - li * hi[c] + hlr[di, cb, rows, :],
                                lr * hi[c] + li * hr[c] + hli[di, cb, rows, :])
                hlr[di, cb, rows, :] = hr[c]
                hli[di, cb, rows, :] = hi[c]
        cr, ci = {}, {}
        for c in chains:
            di, cb = c
            one = 1 if di == 0 else S5_SEG - 1
            first = pos[di] == 0
            cr[c] = jnp.where(first, pltpu.roll(st_r[di, :, lanes_of(cb)], one, 0),
                              pltpu.roll(hr[c], one, 0))
            ci[c] = jnp.where(first, pltpu.roll(st_i[di, :, lanes_of(cb)], one, 0),
                              pltpu.roll(hi[c], one, 0))
        for lvl in range(S5_SEG.bit_length() - 1):
            dist = 1 << lvl
            for c in chains:
                di, cb = c
                mr = pwr_ref[di, S5_SLEN - 1 + lvl, :, lanes_of(cb)]
                mi = pwi_ref[di, S5_SLEN - 1 + lvl, :, lanes_of(cb)]
                shift = dist if di == 0 else S5_SEG - dist
                sr = jnp.where(pos[di] >= dist, pltpu.roll(cr[c], shift, 0), 0.0)
                si = jnp.where(pos[di] >= dist, pltpu.roll(ci[c], shift, 0), 0.0)
                cr[c], ci[c] = cr[c] + mr * sr - mi * si, ci[c] + mr * si + mi * sr
        for c in chains:
            di, cb = c
            l16r = pwr_ref[di, S5_SLEN - 1, :, lanes_of(cb)]
            l16i = pwi_ref[di, S5_SLEN - 1, :, lanes_of(cb)]
            st_r[di, :, lanes_of(cb)] = hr[c] + l16r * cr[c] - l16i * ci[c]
            st_i[di, :, lanes_of(cb)] = hi[c] + l16r * ci[c] + l16i * cr[c]
        for kk in range(S5_SLEN):
            for c in chains:
                di, cb = c
                pr = pwr_ref[di, kk, :, lanes_of(cb)]
                pi = pwi_ref[di, kk, :, lanes_of(cb)]
                rows = step_rows(di, kk)
                hlr[di, cb, rows, :] = hlr[di, cb, rows, :] + pr * cr[c] - pi * ci[c]
                hli[di, cb, rows, :] = hli[di, cb, rows, :] + pr * ci[c] + pi * cr[c]
        for di, cb in chains:
            yt[di, cb] = (jnp.dot(hlr[di, cb].astype(BF16), cre_ref[di, cb], preferred_element_type=F32)
                          + jnp.dot(hli[di, cb].astype(BF16), cim_ref[di, cb], preferred_element_type=F32))
            per = S5_SLEN // S5_SEG
            for r in range(CHUNK // S5_SEG):
                start = (r % per) * S5_SEG * S5_SEG + r // per
                y_refs[di][r * S5_SEG:(r + 1) * S5_SEG, cb * LANES:(cb + 1) * LANES] = (
                    yt[di, cb, pl.ds(start, S5_SEG, stride=S5_SEG), :])

    @pl.when(n == nc - 1)
    def _():
        for di in range(N_DIR):
            last = S5_SEG - 1 if di == 0 else 0
            hnr_ref[di] = st_r[di, last:last + 1, :]
            hni_ref[di] = st_i[di, last:last + 1, :]


def _s5_params(lam_re, lam_im, log_dt, b_re, b_im, c_re, c_im):
    dt = jnp.exp(log_dt)[..., None]
    mag = jnp.exp(lam_re * dt)
    lbr, lbi = mag * jnp.cos(lam_im * dt), mag * jnp.sin(lam_im * dt)
    den = lam_re * lam_re + lam_im * lam_im
    fr = ((lbr - 1.0) * lam_re + lbi * lam_im) / den
    fi = (lbi * lam_re - (lbr - 1.0) * lam_im) / den
    bbr = fr[..., None] * b_re - fi[..., None] * b_im
    bbi = fr[..., None] * b_im + fi[..., None] * b_re
    gpb = S5_GROUPS // S5_CB
    eye = jnp.eye(gpb, dtype=F32)

    def pack_b(b):
        b = b.reshape(N_DIR, S5_CB, gpb, S5_P, S5_CH)
        m = jnp.einsum('dbgpc,gh->dbgchp', b, eye)
        return m.reshape(N_DIR, S5_CB, gpb * S5_CH, gpb * S5_P).astype(BF16)

    def pack_c(c):
        c = c.reshape(N_DIR, S5_CB, gpb, S5_CH, S5_P)
        m = jnp.einsum('dbgcp,gh->dbgphc', c, eye)
        return m.reshape(N_DIR, S5_CB, gpb * S5_P, gpb * S5_CH).astype(BF16)

    pr, pi = [lbr], [lbi]
    for _ in range(S5_SLEN - 1):
        pr, pi = pr + [pr[-1] * lbr - pi[-1] * lbi], pi + [pr[-1] * lbi + pi[-1] * lbr]
    for _ in range(S5_SEG.bit_length() - 2):
        pr, pi = pr + [pr[-1] * pr[-1] - pi[-1] * pi[-1]], pi + [2.0 * pr[-1] * pi[-1]]
    tab = lambda p: jnp.broadcast_to(jnp.stack(p, axis=1).reshape(N_DIR, len(p), 1, S5_ST),
                                     (N_DIR, len(p), S5_SEG, S5_ST))
    return pack_b(bbr), pack_b(bbi), pack_c(c_re), pack_c(-c_im), tab(pr), tab(pi)


def _s5_part(proj, prm, h0r, h0i, path, layer, carry):
    nc = path.nc
    bre, bim, cre, cim, pwr, pwi = prm
    yshape = jax.ShapeDtypeStruct((path.rows, GROUP_W), F32)
    stshape = _final_state_shape(path, (1, S5_ST))
    final = _final_state_spec((1, S5_ST), layer)
    yspec = lambda di: pl.BlockSpec((CHUNK, GROUP_W), lambda s, n: (s * nc + _chunk_of(di, n, nc), 0))
    dir_scr = lambda rows: pltpu.VMEM((N_DIR, S5_CB, rows, S5_BW), F32)
    return _Part(
        functools.partial(_s5_kernel, nc=nc),
        in_specs=[_seq_row_spec(path, 0, GROUP_W, COL_S5), _seq_row_spec(path, 1, GROUP_W, COL_S5),
                  _full_spec(bre.shape), _full_spec(bim.shape), _full_spec(cre.shape),
                  _full_spec(cim.shape), _full_spec(pwr.shape), _full_spec(pwi.shape),
                  _seq_state_spec((1, S5_ST)), _seq_state_spec((1, S5_ST))],
        out_specs=[yspec(0), yspec(1), final, final],
        out_shape=[yshape, yshape, stshape, stshape],
        scratch=[pltpu.VMEM((N_DIR, S5_CB, CHUNK, LANES), F32)] * 2
                + [dir_scr(CHUNK)] * 2 + [pltpu.VMEM((N_DIR, S5_SEG, S5_ST), F32)] * 2,
        args=[proj, proj, bre, bim, cre, cim, pwr, pwi, h0r, h0i],
        carry=carry)


def _gate_act(g):
    lane = lax.broadcasted_iota(jnp.int32, g.shape, 1)
    return jnp.where(lane < GATE_F, g, jnp.where(lane < GATE_DT, jax.nn.log_sigmoid(g), 0.0))


def _mlstm_state_kernel(kf_ref, vf_ref, gf_ref, kb_ref, vb_ref, gb_ref, bias_ref, c0_ref, n0_ref, m0_ref,
                        cf_ref, mf_ref, cb_ref, mb_ref, cn_ref, nn_ref, mn_ref, caug, mscr, *, nc):
    n = pl.program_id(1)
    dh = MLSTM_DH

    @pl.when(n == 0)
    def _():
        for di in range(N_DIR):
            for h in range(MLSTM_HEADS):
                caug[di, h, 0:dh, :] = c0_ref[di, h].T
                caug[di, h, dh:2 * dh, :] = jnp.broadcast_to(n0_ref[di, h], (dh, dh))
                mscr[di, h] = m0_ref[di, h]

    row, col = _iota2()
    ones = jnp.ones((CHUNK, dh), F32)
    dirs = ((kf_ref, vf_ref, gf_ref, cf_ref, mf_ref), (kb_ref, vb_ref, gb_ref, cb_ref, mb_ref))
    for di, (k_ref, v_ref, g_ref, c_out, m_out) in enumerate(dirs):
        p = _gate_act(g_ref[...] + bias_ref[di])
        bcol = _dot01_left(_to01(_dir_mask(di, row, col)), p)
        btot = jnp.sum(p, axis=0, keepdims=True)
        for h in range(MLSTM_HEADS):
            sl = slice(h * dh, (h + 1) * dh)
            k = k_ref[:, sl] * (dh ** -0.5)
            v = v_ref[:, sl]
            bc = bcol[:, GATE_F + h:GATE_F + h + 1]
            bt = btot[:, GATE_F + h:GATE_F + h + 1]
            igc = p[:, h:h + 1]
            mrow = mscr[di, h]
            mm = mrow[:, 0:1]
            cst = caug[di, h]
            c_out[h] = cst.astype(BF16)
            m_out[h] = mrow
            gcol = bt - bc + igc
            m_new = jnp.maximum(bt + mm, jnp.max(gcol, axis=0, keepdims=True))
            w_s = jnp.exp(gcol - m_new)
            decay_old = jnp.exp(bt + mm - m_new)
            vaug = jnp.concatenate([v, ones], axis=1)
            caug[di, h] = decay_old * cst + _bdot_tn(vaug, w_s * k)
            mscr[di, h] = jnp.broadcast_to(m_new, (1, LANES))

    @pl.when(n == nc - 1)
    def _():
        for di in range(N_DIR):
            for h in range(MLSTM_HEADS):
                cn_ref[di, h] = caug[di, h, 0:dh, :].T
                nn_ref[di, h] = caug[di, h, dh:dh + 1, :]
                mn_ref[di, h] = mscr[di, h]


def _mlstm_out_kernel(q_ref, k_ref, v_ref, o_ref, g0_ref, g1_ref, bias_ref, norm_ref,
                      cf_ref, mf_ref, cb_ref, mb_ref, y_ref, *, chunks, first_round):
    dh = MLSTM_DH
    row, col = _iota2()
    masks = [_dir_mask(di, row, col) for di in range(N_DIR)]
    masks_t = [masks[1 - di] for di in range(N_DIR)]
    m01 = [_to01(m) for m in masks]
    mt01 = [_to01(m) for m in masks_t]
    g_refs = (g0_ref, g1_ref)
    c_refs = (cf_ref, cb_ref)
    m_refs = (mf_ref, mb_ref)
    for ci in chunks:
        rs = slice(ci * CHUNK, (ci + 1) * CHUNK)
        ps, bcols, brows = [], [], []
        for di in range(N_DIR):
            p = _gate_act(g_refs[di][rs, :] + bias_ref[di])
            ps.append(p)
            bcols.append(_dot01_left(m01[di], p))
            brows.append(_dot01_right(p.T, mt01[di]))
        outs = []
        for h in range(MLSTM_HEADS):
            sl = slice(h * dh, (h + 1) * dh)
            q = q_ref[rs, sl].astype(BF16)
            k = (k_ref[rs, sl] * (dh ** -0.5)).astype(BF16)
            v = v_ref[rs, sl].astype(BF16)
            qk = _bdot_nt(k, q)
            wsum = None
            extra = None
            for di in range(N_DIR):
                fl = GATE_F + h
                a_s = ps[di][:, h:h + 1] - bcols[di][:, fl:fl + 1]
                b_t = brows[di][fl:fl + 1, :]
                mm = m_refs[di][ci, h]
                am = jnp.where(masks_t[di], a_s, -jnp.inf)
                big = jnp.maximum(mm, jnp.max(am, axis=0, keepdims=True))
                sc = qk * jnp.exp(am - big)
                w_inter = jnp.exp(mm - big)
                qc = _bdot_nt(c_refs[di][ci, h], q)
                den = jnp.sum(sc, axis=0, keepdims=True) + w_inter * qc[dh:dh + 1, :]
                r = 1.0 / jnp.maximum(jnp.abs(den), jnp.exp(-(b_t + big)))
                wsum = sc * r if wsum is None else wsum + sc * r
                e = qc[0:dh, :] * (w_inter * r)
                extra = e if extra is None else extra + e
            ht = _bdot_tn(v, wsum) + extra
            ht = ht * lax.rsqrt(jnp.mean(ht * ht, axis=0, keepdims=True) + EPS)
            outs.append(ht.T)
        hs = jnp.concatenate(outs, axis=1) * norm_ref[...]
        y_ref[rs, :] = (jax.nn.sigmoid(o_ref[rs, :]) * hs).astype(BF16)


def _mlstm_state_part(proj, bias, c0, n0, m0, path, layer, carry):
    nc = path.nc
    hh, dh = MLSTM_HEADS, MLSTM_DH
    csnap = jax.ShapeDtypeStruct((path.n_chunks, hh, 2 * dh, dh), BF16)
    msnap = jax.ShapeDtypeStruct((path.n_chunks, hh, 1, LANES), F32)
    seq_in = lambda di: [_seq_row_spec(path, di, GROUP_W, COL_MK), _seq_row_spec(path, di, GROUP_W, COL_MV),
                         _seq_gate_spec(path, di)]
    snap_out = lambda di: [_seq_snap_spec(path, di, (hh, 2 * dh, dh)), _seq_snap_spec(path, di, (hh, 1, LANES))]
    finals = [(hh, dh, dh), (hh, 1, dh), (hh, 1, LANES)]
    return _Part(
        functools.partial(_mlstm_state_kernel, nc=nc),
        in_specs=seq_in(0) + seq_in(1) + [_full_spec((N_DIR, 1, LANES)), _seq_state_spec((hh, dh, dh)),
                                         _seq_state_spec((hh, 1, dh)), _seq_state_spec((hh, 1, LANES))],
        out_specs=snap_out(0) + snap_out(1) + [_final_state_spec(s, layer) for s in finals],
        out_shape=[csnap, msnap, csnap, msnap] + [_final_state_shape(path, s) for s in finals],
        scratch=[pltpu.VMEM((N_DIR, hh, 2 * dh, dh), F32), pltpu.VMEM((N_DIR, hh, 1, LANES), F32)],
        args=[proj, proj, proj, proj, proj, proj, bias, c0, n0, m0],
        carry=carry)


def _mlstm_out_part(proj, bias, norm, snaps, path):
    hh, dh = MLSTM_HEADS, MLSTM_DH
    return _Part(
        _mlstm_out_kernel,
        in_specs=[_out_row_spec(GROUP_W, COL_MQ), _out_row_spec(GROUP_W, COL_MK),
                  _out_row_spec(GROUP_W, COL_MV), _out_row_spec(GROUP_W, COL_MO),
                  _out_row_spec(LANES, COL_GATE), _out_row_spec(LANES, COL_GATE + 1),
                  _full_spec((N_DIR, 1, LANES)), _full_spec((1, GROUP_W)),
                  _out_snap_spec((hh, 2 * dh, dh)), _out_snap_spec((hh, 1, LANES)),
                  _out_snap_spec((hh, 2 * dh, dh)), _out_snap_spec((hh, 1, LANES))],
        out_specs=[_out_row_spec(GROUP_W, 0)],
        out_shape=[jax.ShapeDtypeStruct((path.rows, GROUP_W), BF16)],
        scratch=[],
        args=[proj, proj, proj, proj, proj, proj, bias, norm] + list(snaps))


def _rope(a, cos, sin):
    return a * cos + pltpu.roll(a, RET_DH // 2, 1) * sin


def _ret_state_kernel(*refs, nc, rope):
    if rope:
        (kf_ref, vf_ref, kb_ref, vb_ref, lg_ref, s0_ref, cosf_ref, sinf_ref, cosb_ref, sinb_ref,
         sf_ref, sb_ref, sn_ref, sscr) = refs
        tabs = ((cosf_ref, sinf_ref), (cosb_ref, sinb_ref))
    else:
        kf_ref, vf_ref, kb_ref, vb_ref, lg_ref, s0_ref, sf_ref, sb_ref, sn_ref, sscr = refs
    n = pl.program_id(1)
    dh = RET_DH

    @pl.when(n == 0)
    def _():
        sscr[...] = s0_ref[...]

    t = lax.broadcasted_iota(jnp.int32, (CHUNK, dh), 0).astype(F32)
    for di, (k_ref, v_ref, s_out) in enumerate(((kf_ref, vf_ref, sf_ref), (kb_ref, vb_ref, sb_ref))):
        left = CHUNK - 1.0 - t if di == 0 else t
        for h in range(RET_HEADS):
            sl = slice(h * dh, (h + 1) * dh)
            k = k_ref[:, sl]
            if rope:
                k = _rope(k, tabs[di][0][...], tabs[di][1][...])
            k = k * (dh ** -0.5)
            lg = -jnp.exp(lg_ref[di, :, sl])
            st = sscr[di, h]
            s_out[h] = st.astype(BF16)
            sscr[di, h] = jnp.exp(CHUNK * lg) * st + _bdot((k * jnp.exp(left * lg)).T, v_ref[:, sl])

    @pl.when(n == nc - 1)
    def _():
        sn_ref[...] = sscr[...]


def _ret_out_kernel(*refs, rope, chunks, first_round):
    if rope:
        (q_ref, k_ref, v_ref, g_ref, lg_ref, norm_ref, sf_ref, sb_ref, cos_ref, sin_ref,
         y_ref, dsum, qdec) = refs
    else:
        q_ref, k_ref, v_ref, g_ref, lg_ref, norm_ref, sf_ref, sb_ref, y_ref, dsum, qdec = refs
    dh = RET_DH

    def build_tables():
        row, col = _iota2()
        absdiff = jnp.abs(row - col).astype(F32)
        t = row.astype(F32)
        for h in range(RET_HEADS):
            sl = slice(h * dh, (h + 1) * dh)
            lgf = -jnp.exp(lg_ref[0, :, sl])
            lgb = -jnp.exp(lg_ref[1, :, sl])
            both = jnp.exp(absdiff * jnp.where(row >= col, lgf, lgb))
            dsum[h] = jnp.where(row == col, 2.0, both)
            qdec[0, h] = jnp.exp((t + 1.0) * lgf)
            qdec[1, h] = jnp.exp((CHUNK - t) * lgb)

    if first_round:
        pl.when(pl.program_id(0) == 0)(build_tables)

    s_refs = (sf_ref, sb_ref)
    for ci in chunks:
        rs = slice(ci * CHUNK, (ci + 1) * CHUNK)
        outs = []
        for h in range(RET_HEADS):
            sl = slice(h * dh, (h + 1) * dh)
            q = q_ref[rs, sl]
            k = k_ref[rs, sl]
            if rope:
                cos, sin = cos_ref[rs, :], sin_ref[rs, :]
                q = _rope(q, cos, sin)
                k = _rope(k, cos, sin)
            q = q.astype(BF16)
            k = (k * (dh ** -0.5)).astype(BF16)
            y = _bdot(_bdot_nt(q, k) * dsum[h], v_ref[rs, sl])
            for di in range(N_DIR):
                y = y + qdec[di, h] * jnp.dot(q, s_refs[di][ci, h], preferred_element_type=F32)
            outs.append(y)
        ys = _head_norm(jnp.concatenate(outs, axis=1), RET_HEADS, dh) * norm_ref[...]
        gate = g_ref[rs, :]
        y_ref[rs, :] = (ys * (gate * jax.nn.sigmoid(gate))).astype(BF16)


def _rope_tables(seq_len):
    n_rows = seq_len // GRID_W
    rows = jnp.repeat(jnp.arange(n_rows, dtype=F32), GRID_W)
    cols = jnp.tile(jnp.arange(GRID_W, dtype=F32), n_rows)
    quarter = RET_DH // 4
    freqs = ROPE_BASE ** (-jnp.arange(quarter, dtype=F32) / quarter)
    ang = jnp.concatenate([rows[:, None] * freqs, cols[:, None] * freqs], axis=-1)
    cos, sin = jnp.cos(ang), jnp.sin(ang)
    return jnp.concatenate([cos, cos], axis=-1), jnp.concatenate([-sin, sin], axis=-1)


def _ret_state_part(proj, lg, s0, rope_tabs, path, layer, carry):
    nc = path.nc
    hh, dh = RET_HEADS, RET_DH
    snap = jax.ShapeDtypeStruct((path.n_chunks, hh, dh, dh), BF16)
    seq_in = lambda di: [_seq_row_spec(path, di, GROUP_W, COL_RK), _seq_row_spec(path, di, GROUP_W, COL_RV)]
    in_specs = seq_in(0) + seq_in(1) + [_full_spec((N_DIR, 1, GROUP_W)), _seq_state_spec((hh, dh, dh))]
    args = [proj, proj, proj, proj, lg, s0]
    if path.rope:
        for di in range(N_DIR):
            tab = pl.BlockSpec((CHUNK, dh), lambda s, n, di=di: (_chunk_of(di, n, nc), 0))
            in_specs += [tab, tab]
            args += list(rope_tabs)
    return _Part(
        functools.partial(_ret_state_kernel, nc=nc, rope=path.rope),
        in_specs=in_specs,
        out_specs=[_seq_snap_spec(path, 0, (hh, dh, dh)), _seq_snap_spec(path, 1, (hh, dh, dh)),
                   _final_state_spec((hh, dh, dh), layer)],
        out_shape=[snap, snap, _final_state_shape(path, (hh, dh, dh))],
        scratch=[pltpu.VMEM((N_DIR, hh, dh, dh), F32)],
        args=args,
        carry=carry)


def _ret_out_part(proj, lg, norm, snaps, rope_tabs, path):
    hh, dh = RET_HEADS, RET_DH
    in_specs = [_out_row_spec(GROUP_W, COL_RQ), _out_row_spec(GROUP_W, COL_RK),
                _out_row_spec(GROUP_W, COL_RV), _out_row_spec(GROUP_W, COL_RG),
                _full_spec((N_DIR, 1, GROUP_W)), _full_spec((1, GROUP_W)),
                _out_snap_spec((hh, dh, dh)), _out_snap_spec((hh, dh, dh))]
    args = [proj, proj, proj, proj, lg, norm] + list(snaps)
    if path.rope:
        per_seq = path.nc // CPS
        tab = pl.BlockSpec((CPS * CHUNK, dh), lambda i: (i % per_seq, 0))
        in_specs += [tab, tab]
        args += list(rope_tabs)
    return _Part(
        functools.partial(_ret_out_kernel, rope=path.rope),
        in_specs=in_specs,
        out_specs=[_out_row_spec(GROUP_W, 0)],
        out_shape=[jax.ShapeDtypeStruct((path.rows, GROUP_W), BF16)],
        scratch=[pltpu.VMEM((hh, CHUNK, CHUNK), F32), pltpu.VMEM((N_DIR, hh, CHUNK, dh), F32)],
        args=args)


def _conv_kernel(x_ref, xp_ref, xn_ref, cw_ref, cb_ref, o_ref, ext, *, bps, tr):
    ib = pl.program_id(0) % bps
    ext[0:HALO, :] = jnp.where(ib > 0, xp_ref[...], 0.0)
    ext[HALO:HALO + tr, :] = x_ref[...]
    ext[HALO + tr:, :] = jnp.where(ib < bps - 1, xn_ref[...], 0.0)
    pad = (SSD_CONV - 1) // 2
    acc = cb_ref[...] + cw_ref[0:1, :] * ext[HALO - pad:HALO - pad + tr, :]
    for j in range(1, SSD_CONV):
        acc = acc + cw_ref[j:j + 1, :] * ext[HALO - pad + j:HALO - pad + j + tr, :]
    o_ref[...] = acc * jax.nn.sigmoid(acc)


def _ssd_conv(proj, conv_w, conv_b, path):
    tr = min(CONV_ROWS, path.seq_len)
    per = tr // HALO
    n_halo = path.rows // HALO
    bps = path.seq_len // tr
    return pl.pallas_call(
        functools.partial(_conv_kernel, bps=bps, tr=tr),
        grid=(path.rows // tr,),
        in_specs=[pl.BlockSpec((tr, SSD_XBC), lambda i: (i, COL_XBC)),
                  pl.BlockSpec((HALO, SSD_XBC), lambda i: (jnp.maximum(i * per - 1, 0), COL_XBC)),
                  pl.BlockSpec((HALO, SSD_XBC), lambda i: (jnp.minimum((i + 1) * per, n_halo - 1), COL_XBC)),
                  _full_spec((SSD_CONV, SSD_XBC)), _full_spec((1, SSD_XBC))],
        out_specs=pl.BlockSpec((tr, SSD_XBC), lambda i: (i, 0)),
        out_shape=jax.ShapeDtypeStruct((path.rows, SSD_XBC), F32),
        scratch_shapes=[pltpu.VMEM((tr + 2 * HALO, SSD_XBC), F32)],
        compiler_params=_cparams(1),
        name="ssd_conv",
    )(proj, proj, proj, conv_w, conv_b)


def _ssd_rates(g, bias, alog):
    lane = lax.broadcasted_iota(jnp.int32, g.shape, 1)
    dt_lanes = (lane >= GATE_DT) & (lane < GATE_DT + SSD_HEADS)
    dt = jnp.where(dt_lanes, jax.nn.softplus(g + bias), 0.0)
    return dt, dt * -jnp.exp(alog)


def _ssd_state_kernel(xf_ref, bf_ref, gf_ref, xb_ref, bb_ref, gb_ref, bias_ref, alog_ref, h0_ref,
                      hf_ref, hb_ref, hn_ref, hscr, *, nc):
    n = pl.program_id(1)

    @pl.when(n == 0)
    def _():
        for di in range(N_DIR):
            for gi in range(SSD_GROUPS):
                hscr[di, gi] = h0_ref[di, gi * SSD_HPG:(gi + 1) * SSD_HPG].reshape(SSD_GW, SSD_N).T

    row, col = _iota2()
    dirs = ((xf_ref, bf_ref, gf_ref, hf_ref), (xb_ref, bb_ref, gb_ref, hb_ref))
    for di, (x_ref, b_ref, g_ref, h_out) in enumerate(dirs):
        dt, a = _ssd_rates(g_ref[...], bias_ref[di], alog_ref[di])
        ccol = _dot01_left(_to01(_dir_mask(di, row, col)), a)
        ctot = jnp.sum(a, axis=0, keepdims=True)
        for gi in range(SSD_GROUPS):
            hst = hscr[di, gi]
            h_out[gi] = hst.astype(BF16)
            wx_parts, tot_parts = [], []
            for r in range(SSD_HPG):
                hd = gi * SSD_HPG + r
                ln = GATE_DT + hd
                tot = ctot[:, ln:ln + 1]
                w = jnp.exp(tot - ccol[:, ln:ln + 1]) * dt[:, ln:ln + 1]
                wx_parts.append(w * x_ref[:, hd * SSD_P:(hd + 1) * SSD_P])
                tot_parts.append(jnp.broadcast_to(jnp.exp(tot), (1, SSD_P)))
            wx = jnp.concatenate(wx_parts, axis=1)
            bm = b_ref[:, gi * SSD_N:(gi + 1) * SSD_N]
            hscr[di, gi] = jnp.concatenate(tot_parts, axis=1) * hst + _bdot(bm.T, wx)

    @pl.when(n == nc - 1)
    def _():
        for di in range(N_DIR):
            for gi in range(SSD_GROUPS):
                hn_ref[di, gi * SSD_HPG:(gi + 1) * SSD_HPG] = hscr[di, gi].T.reshape(SSD_HPG, SSD_P, SSD_N)


def _ssd_out_kernel(z_ref, xbc_ref, g0_ref, g1_ref, bias_ref, alog_ref, dsk_ref, norm_ref,
                    hf_ref, hb_ref, y_ref, *, chunks, first_round):
    row, col = _iota2()
    masks = [_dir_mask(di, row, col) for di in range(N_DIR)]
    m01 = [_to01(m) for m in masks]
    mt01 = [_to01(masks[1 - di]) for di in range(N_DIR)]
    g_refs = (g0_ref, g1_ref)
    h_refs = (hf_ref, hb_ref)
    for ci in chunks:
        rs = slice(ci * CHUNK, (ci + 1) * CHUNK)
        dtts, ccols, crows = [], [], []
        for di in range(N_DIR):
            dt, a = _ssd_rates(g_refs[di][rs, :], bias_ref[di], alog_ref[di])
            dtts.append(dt.T)
            ccols.append(_dot01_left(m01[di], a))
            crows.append(_dot01_right(a.T, mt01[di]))
        xs = xbc_ref[rs, 0:GROUP_W]
        y_parts = []
        for gi in range(SSD_GROUPS):
            bm = xbc_ref[rs, GROUP_W + gi * SSD_N:GROUP_W + (gi + 1) * SSD_N].astype(BF16)
            c0 = GROUP_W + (SSD_GROUPS + gi) * SSD_N
            cm = xbc_ref[rs, c0:c0 + SSD_N].astype(BF16)
            cb = _bdot_nt(cm, bm)
            inter = [jnp.dot(cm, h_refs[di][ci, gi], preferred_element_type=F32) for di in range(N_DIR)]
            for r in range(SSD_HPG):
                hd = gi * SSD_HPG + r
                ln = GATE_DT + hd
                hs = slice(r * SSD_P, (r + 1) * SSD_P)
                wmat = None
                y = None
                for di in range(N_DIR):
                    cc = ccols[di][:, ln:ln + 1]
                    decay = jnp.exp(jnp.where(masks[di], cc - crows[di][ln:ln + 1, :], -jnp.inf))
                    wd = decay * dtts[di][ln:ln + 1, :]
                    wmat = wd if wmat is None else wmat + wd
                    e = jnp.exp(cc) * inter[di][:, hs]
                    y = e if y is None else y + e
                y_parts.append(y + _bdot(cb * wmat, xs[:, hd * SSD_P:(hd + 1) * SSD_P]))
        z = z_ref[rs, :]
        yy = (dsk_ref[...] * xs + jnp.concatenate(y_parts, axis=1)) * (z * jax.nn.sigmoid(z))
        y_ref[rs, :] = _rms(yy, norm_ref[...]).astype(BF16)


SSD_SNAP = (SSD_GROUPS, SSD_N, SSD_GW)


def _ssd_state_part(proj, xbc, bias, alog, h0, path, layer, carry):
    nc = path.nc
    st_shape = (SSD_HEADS, SSD_P, SSD_N)
    snap_shape = SSD_SNAP
    snap = jax.ShapeDtypeStruct((path.n_chunks,) + snap_shape, BF16)

    def seq_in(di):
        return [pl.BlockSpec((CHUNK, GROUP_W), lambda s, n: (s * nc + _chunk_of(di, n, nc), 0)),
                pl.BlockSpec((CHUNK, SSD_GROUPS * SSD_N),
                             lambda s, n: (s * nc + _chunk_of(di, n, nc), GROUP_W // (SSD_GROUPS * SSD_N))),
                _seq_gate_spec(path, di)]

    return _Part(
        functools.partial(_ssd_state_kernel, nc=nc),
        in_specs=seq_in(0) + seq_in(1) + [_full_spec((N_DIR, 1, LANES)), _full_spec((N_DIR, 1, LANES)),
                                         _seq_state_spec(st_shape)],
        out_specs=[_seq_snap_spec(path, 0, snap_shape), _seq_snap_spec(path, 1, snap_shape),
                   _final_state_spec(st_shape, layer)],
        out_shape=[snap, snap, _final_state_shape(path, st_shape)],
        scratch=[pltpu.VMEM((N_DIR,) + snap_shape, F32)],
        args=[xbc, xbc, proj, xbc, xbc, proj, bias, alog, h0],
        carry=carry)


def _ssd_out_part(proj, xbc, bias, alog, dskip, norm, snaps, path):
    return _Part(
        _ssd_out_kernel,
        in_specs=[_out_row_spec(GROUP_W, COL_DZ), _out_row_spec(SSD_XBC, 0),
                  _out_row_spec(LANES, COL_GATE), _out_row_spec(LANES, COL_GATE + 1),
                  _full_spec((N_DIR, 1, LANES)), _full_spec((N_DIR, 1, LANES)),
                  _full_spec((1, GROUP_W)), _full_spec((1, GROUP_W)),
                  _out_snap_spec(SSD_SNAP), _out_snap_spec(SSD_SNAP)],
        out_specs=[_out_row_spec(GROUP_W, 0)],
        out_shape=[jax.ShapeDtypeStruct((path.rows, GROUP_W), BF16)],
        scratch=[],
        args=[proj, xbc, proj, proj, bias, alog, dskip, norm] + list(snaps))


def _out_proj_kernel(u_ref, yf_ref, yb_ref, dsk_ref, wglu_ref, yb2_ref, yc_ref, yd_ref, w_ref, x_ref,
                     g1_ref, g_ref, sc_ref, sh_ref, x1_ref, h2_ref):
    t = jax.nn.gelu(dsk_ref[...] * u_ref[...] + yf_ref[...] + yb_ref[...])
    ya = (t * jax.nn.sigmoid(_bdot(t, wglu_ref[...]))).astype(BF16)
    acc = jnp.dot(ya, w_ref[0], preferred_element_type=F32)
    for i, y_ref in enumerate((yb2_ref, yc_ref, yd_ref)):
        acc = acc + jnp.dot(y_ref[...], w_ref[i + 1], preferred_element_type=F32)
    x1 = x_ref[...] + g1_ref[...] * acc
    x1_ref[...] = x1
    h2_ref[...] = (_rms(x1, g_ref[...]) * (1.0 + sc_ref[...]) + sh_ref[...]).astype(BF16)


def _out_proj(proj, s5_y, s5_post, ys, x, mod3, norm2, w_out_p, layer, path, tm=OUT_PROJ_TM):
    rows = x.shape[0]
    yspec = pl.BlockSpec((tm, GROUP_W), lambda i: (i, 0))
    xspec = pl.BlockSpec((tm, D_MODEL), lambda i: (i, 0))
    dskip, wglu = s5_post
    return pl.pallas_call(
        _out_proj_kernel,
        grid=(rows // tm,),
        in_specs=[pl.BlockSpec((tm, GROUP_W), lambda i: (i, COL_S5)), yspec, yspec,
                  _full_spec((1, GROUP_W)), _full_spec((GROUP_W, GROUP_W)),
                  yspec, yspec, yspec,
                  pl.BlockSpec((None, 4, GROUP_W, D_MODEL), lambda i: (layer, 0, 0, 0)),
                  xspec,
                  _mod_spec(layer, path, tm, 2),
                  pl.BlockSpec((None, 1, D_MODEL), lambda i: (layer, 0, 0)),
                  _mod_spec(layer, path, tm, 4),
                  _mod_spec(layer, path, tm, 3)],
        out_specs=(xspec, xspec),
        out_shape=(jax.ShapeDtypeStruct((rows, D_MODEL), F32),
                   jax.ShapeDtypeStruct((rows, D_MODEL), BF16)),
        compiler_params=_cparams(1),
        name="out_proj",
    )(proj, s5_y[0], s5_y[1], dskip, wglu, *ys, w_out_p, x, mod3, norm2.reshape(DEPTH, 1, D_MODEL),
      mod3, mod3)


def _mlp_kernel(h2_ref, x1_ref, w1_ref, w2_ref, g2_ref, fn_ref, o_ref, *, nk, final):
    k = pl.program_id(1)

    @pl.when(k == 0)
    def _():
        o_ref[...] = x1_ref[...]

    a = jnp.dot(h2_ref[...], w1_ref[...], preferred_element_type=F32)
    a = jnp.square(jnp.maximum(a, 0.0))
    o_ref[...] += g2_ref[...] * jnp.dot(a.astype(BF16), w2_ref[...], preferred_element_type=F32)

    if final:
        @pl.when(k == nk - 1)
        def _():
            o_ref[...] = _rms(o_ref[...], fn_ref[...])


def _mlp(h2, x1, mod3, w1, w2, final_norm, layer, path, final, tm=MLP_TM, tf=MLP_TF):
    rows = x1.shape[0]
    nk = D_FF // tf
    xspec = pl.BlockSpec((tm, D_MODEL), lambda i, k: (i, 0))
    return pl.pallas_call(
        functools.partial(_mlp_kernel, nk=nk, final=final),
        grid=(rows // tm, nk),
        in_specs=[xspec, xspec,
                  pl.BlockSpec((None, D_MODEL, tf), lambda i, k: (layer, 0, k)),
                  pl.BlockSpec((None, tf, D_MODEL), lambda i, k: (layer, k, 0)),
                  _mod_spec(layer, path, tm, 5),
                  pl.BlockSpec((1, D_MODEL), lambda i, k: (0, 0))],
        out_specs=xspec,
        out_shape=jax.ShapeDtypeStruct((rows, D_MODEL), F32),
        compiler_params=_cparams(2),
        name="mlp",
    )(h2, x1, w1, w2, mod3, final_norm.reshape(1, D_MODEL))


def _permute_w_in(w_in):
    o_mi = 5 * GROUP_W
    o_mf = o_mi + N_DIR * MLSTM_HEADS
    o_r = o_mf + N_DIR * MLSTM_HEADS
    o_dt = D_IN - N_DIR * SSD_HEADS
    zeros = jnp.zeros(w_in.shape[:2] + (LANES - 2 * MLSTM_HEADS - SSD_HEADS,), w_in.dtype)
    parts = [w_in[..., :o_mi], w_in[..., o_r:o_dt]]
    for di in range(N_DIR):
        parts += [w_in[..., o_mi + di * MLSTM_HEADS:o_mi + (di + 1) * MLSTM_HEADS],
                  w_in[..., o_mf + di * MLSTM_HEADS:o_mf + (di + 1) * MLSTM_HEADS],
                  w_in[..., o_dt + di * SSD_HEADS:o_dt + (di + 1) * SSD_HEADS], zeros]
    return jnp.concatenate(parts, axis=-1).astype(BF16)


def _gate_rows(i_part, f_part, dt_part):
    pad = jnp.zeros(i_part.shape[:2] + (LANES - 2 * MLSTM_HEADS - SSD_HEADS,), F32)
    return jnp.concatenate([i_part, f_part, dt_part, pad], axis=-1)[:, :, None, :]


def kernel(x_prompt, x_sample, c, state_s5_re, state_s5_im, state_mlstm_c, state_mlstm_n, state_mlstm_m, state_ret, state_ssd, c_ctx, norm1, w_ada, b_ada, w_in, s5_lambda_re, s5_lambda_im, s5_log_dt, s5_b_re, s5_b_im, s5_c_re, s5_c_im, s5_d, s5_w_glu, mlstm_i_bias, mlstm_f_bias, mlstm_norm, ret_log_rate, ret_norm, ssd_conv_w, ssd_conv_b, ssd_dt_bias, ssd_a_log, ssd_d, ssd_norm, w_out, norm2, w_mlp1, w_mlp2, final_norm):
    bp, seq = x_prompt.shape[0], x_prompt.shape[1]
    bd, dseq = x_sample.shape[0], x_sample.shape[1]
    ctx = _Path(bp, seq, 0, 0, False, state_layers=DEPTH)
    lat = _Path(bd, dseq, 1, 1, True)

    cvec = jnp.concatenate([c_ctx[None], c, jnp.zeros((8 - 1 - bd, D_MODEL), F32)], axis=0)
    mod3 = _ada(cvec, w_ada, b_ada).reshape(DEPTH * 8 * 6, 1, D_MODEL)

    w_in_p = _permute_w_in(w_in)
    w_out_p = w_out.astype(BF16).reshape(DEPTH, 4, GROUP_W, D_MODEL)
    w1 = w_mlp1.astype(BF16)
    w2 = w_mlp2.astype(BF16)
    wglu = s5_w_glu.astype(BF16)
    zeros4 = jnp.zeros((DEPTH, N_DIR, MLSTM_HEADS), F32)
    gate_bias = _gate_rows(mlstm_i_bias, mlstm_f_bias, ssd_dt_bias)
    alog_rows = _gate_rows(zeros4, zeros4, ssd_a_log)
    ret_lg = jnp.repeat(ret_log_rate, RET_DH, axis=-1)[:, :, None, :]
    ssd_d_row = jnp.repeat(ssd_d, SSD_P, axis=-1)[:, None, :]
    rope_tabs = _rope_tables(dseq)

    def states(path, layer):
        if path is ctx:
            n = path.n_seq
            z = lambda *s: jnp.zeros((n, N_DIR) + s, F32)
            return (z(1, S5_ST), z(1, S5_ST), z(MLSTM_HEADS, MLSTM_DH, MLSTM_DH),
                    z(MLSTM_HEADS, 1, MLSTM_DH), z(MLSTM_HEADS, 1, LANES),
                    z(RET_HEADS, RET_DH, RET_DH), z(SSD_HEADS, SSD_P, SSD_N))
        return (state_s5_re[:, layer].reshape(bd, N_DIR, 1, S5_ST),
                state_s5_im[:, layer].reshape(bd, N_DIR, 1, S5_ST),
                state_mlstm_c[:, layer],
                state_mlstm_n[:, layer][..., None, :],
                jnp.broadcast_to(state_mlstm_m[:, layer][..., None, None],
                                 (bd, N_DIR, MLSTM_HEADS, 1, LANES)),
                state_ret[:, layer], state_ssd[:, layer])

    xs = {ctx: x_prompt.reshape(ctx.rows, D_MODEL), lat: x_sample.reshape(lat.rows, D_MODEL)}
    final_shapes = [(1, S5_ST), (1, S5_ST), (MLSTM_HEADS, MLSTM_DH, MLSTM_DH), (MLSTM_HEADS, 1, MLSTM_DH),
                    (MLSTM_HEADS, 1, LANES), (RET_HEADS, RET_DH, RET_DH), (SSD_HEADS, SSD_P, SSD_N)]
    acc = {ctx: [jnp.zeros((bp, DEPTH, N_DIR) + s, F32) for s in final_shapes], lat: None}
    for layer in range(DEPTH):
        s5_prm = _s5_params(s5_lambda_re[layer], s5_lambda_im[layer], s5_log_dt[layer],
                            s5_b_re[layer], s5_b_im[layer], s5_c_re[layer], s5_c_im[layer])
        s5_post = (s5_d[layer][None], wglu[layer])
        for path in (ctx, lat):
            x = xs[path]
            st = states(path, layer)
            proj = _in_proj(x, mod3, norm1, w_in_p, layer, path)
            xbc = _ssd_conv(proj, ssd_conv_w[layer], ssd_conv_b[layer][None], path)
            a = acc[path]
            carry = (lambda lo, hi: tuple(a[lo:hi])) if a else (lambda lo, hi: ())
            slot = layer if a else 0
            seq_out = _fused_call(
                [_s5_part(proj, s5_prm, st[0], st[1], path, slot, carry(0, 2)),
                 _mlstm_state_part(proj, gate_bias[layer], st[2], st[3], st[4], path, slot, carry(2, 5)),
                 _ret_state_part(proj, ret_lg[layer], st[5], rope_tabs, path, slot, carry(5, 6)),
                 _ssd_state_part(proj, xbc, gate_bias[layer], alog_rows[layer], st[6], path, slot,
                                 carry(6, 7))],
                (path.n_seq, path.nc), "mix_seq")
            (s5f, s5b, s5r, s5i), m_out, r_out, d_out = seq_out
            if a:
                acc[path] = [s5r, s5i, *m_out[4:], r_out[2], d_out[2]]
            (yb,), (yc,), (yd,) = _fused_call(
                [_mlstm_out_part(proj, gate_bias[layer], mlstm_norm[layer][None], m_out[:4], path),
                 _ret_out_part(proj, ret_lg[layer], ret_norm[layer][None], r_out[:2], rope_tabs, path),
                 _ssd_out_part(proj, xbc, gate_bias[layer], alog_rows[layer], ssd_d_row[layer],
                               ssd_norm[layer][None], d_out[:2], path)],
                (path.n_chunks // CPS,), "mix_out", rounds=tuple((ci,) for ci in range(CPS)))
            x1, h2 = _out_proj(proj, (s5f, s5b), s5_post, (yb, yc, yd), x, mod3, norm2, w_out_p,
                               layer, path)
            xs[path] = _mlp(h2, x1, mod3, w1, w2, final_norm, layer, path, layer == DEPTH - 1)
    y_prompt = xs[ctx].reshape(bp, seq, D_MODEL)
    y_sample = xs[lat].reshape(bd, dseq, D_MODEL)
    s5r, s5i, mc, mn, mm, rs, dh = acc[ctx]
    return (y_prompt, y_sample,
            s5r.reshape(bp, DEPTH, N_DIR, S5_GROUPS, S5_P), s5i.reshape(bp, DEPTH, N_DIR, S5_GROUPS, S5_P),
            mc, mn[..., 0, :], mm[..., 0, 0], rs, dh)
```

```python
import functools
from typing import Callable, NamedTuple

import jax
import jax.numpy as jnp
from jax import lax
from jax.experimental import pallas as pl
from jax.experimental.pallas import tpu as pltpu

D_MODEL = 2048
DEPTH = 4
GRID_W = 64
N_DIR = 2
GROUP_W = D_MODEL // 4
S5_CH = 16
S5_GROUPS = GROUP_W // S5_CH
S5_P = 64
S5_ST = S5_GROUPS * S5_P
S5_CB = 4
S5_BW = S5_ST // S5_CB
S5_SEG = 8
S5_SLEN = 16
S5_ILV = 2
MLSTM_HEADS = 4
MLSTM_DH = GROUP_W // MLSTM_HEADS
RET_HEADS = 4
RET_DH = GROUP_W // RET_HEADS
SSD_HEADS = 8
SSD_P = GROUP_W // SSD_HEADS
SSD_GROUPS = 2
SSD_HPG = SSD_HEADS // SSD_GROUPS
SSD_N = 128
SSD_GW = SSD_HPG * SSD_P
SSD_CONV = 5
SSD_XBC = GROUP_W + 2 * SSD_GROUPS * SSD_N
D_FF = 4 * D_MODEL
CHUNK = 128
EPS = 1e-6
ROPE_BASE = 10000.0
D_IN = 10 * GROUP_W + 2 * N_DIR * MLSTM_HEADS + SSD_XBC + N_DIR * SSD_HEADS
LANES = 128
HALO = 8
D_IN_P = 10 * GROUP_W + SSD_XBC + N_DIR * LANES
COL_S5, COL_MQ, COL_MK, COL_MV, COL_MO, COL_RQ, COL_RK, COL_RV, COL_RG, COL_DZ = range(10)
COL_XBC = 10 * GROUP_W // SSD_XBC
COL_GATE = (10 * GROUP_W + SSD_XBC) // LANES
GATE_I, GATE_F, GATE_DT = 0, MLSTM_HEADS, 2 * MLSTM_HEADS
CPS = 4
ADA_TN = 1024
IN_PROJ_TM, IN_PROJ_TN = 1024, 1280
OUT_PROJ_TM = 512
NORM_ROWS = 32
MLP_TM, MLP_TF = 512, 1024
CONV_ROWS = 512
V7X_VMEM_BYTES = 64 * 1024 * 1024
VMEM_LIMIT = V7X_VMEM_BYTES // 8 * 7

F32 = jnp.float32
BF16 = jnp.bfloat16


def _bdot(a, b):
    return jnp.dot(a.astype(BF16), b.astype(BF16), preferred_element_type=F32)


def _bdot_nt(a, b):
    return lax.dot_general(a.astype(BF16), b.astype(BF16), (((1,), (1,)), ((), ())),
                           preferred_element_type=F32)


def _bdot_tn(a, b):
    return lax.dot_general(a.astype(BF16), b.astype(BF16), (((0,), (0,)), ((), ())),
                           preferred_element_type=F32)


def _split3(x):
    x1 = x.astype(BF16)
    r1 = x - x1.astype(F32)
    x2 = r1.astype(BF16)
    x3 = (r1 - x2.astype(F32)).astype(BF16)
    return x1, x2, x3


def _dot01_left(m01, x):
    return sum(jnp.dot(m01, p, preferred_element_type=F32) for p in _split3(x))


def _dot01_right(x, m01):
    return sum(jnp.dot(p, m01, preferred_element_type=F32) for p in _split3(x))


def _iota2():
    row = lax.broadcasted_iota(jnp.int32, (CHUNK, CHUNK), 0)
    col = lax.broadcasted_iota(jnp.int32, (CHUNK, CHUNK), 1)
    return row, col


def _dir_mask(di, row, col):
    return row >= col if di == 0 else row <= col


def _to01(mask):
    return jnp.where(mask, 1.0, 0.0).astype(BF16)


def _cparams(n_axes):
    return pltpu.CompilerParams(dimension_semantics=("arbitrary",) * n_axes,
                                vmem_limit_bytes=VMEM_LIMIT)


def _rms(x, g):
    return x * lax.rsqrt(jnp.mean(x * x, axis=-1, keepdims=True) + EPS) * g


def _head_norm(y, heads, dh):
    parts = []
    for h in range(heads):
        seg = y[:, h * dh:(h + 1) * dh]
        parts.append(seg * lax.rsqrt(jnp.mean(seg * seg, axis=-1, keepdims=True) + EPS))
    return jnp.concatenate(parts, axis=1)


class _Path:
    def __init__(self, n_seq, seq_len, mod_base, mod_per_seq, rope, state_layers=1):
        self.n_seq, self.seq_len, self.state_layers = n_seq, seq_len, state_layers
        self.mod_base, self.mod_per_seq, self.rope = mod_base, mod_per_seq, rope
        self.rows = n_seq * seq_len
        self.nc = seq_len // CHUNK
        self.n_chunks = self.rows // CHUNK


def _ada_kernel(c_ref, w_ref, b_ref, o_ref):
    c = c_ref[...]
    a = c * jax.nn.sigmoid(c)
    o_ref[...] = _bdot(a, w_ref[...]) + b_ref[...]


def _ada(cvec, w_ada, b_ada, tn=ADA_TN):
    n = w_ada.shape[-1]
    return pl.pallas_call(
        _ada_kernel,
        grid=(DEPTH, n // tn),
        in_specs=[pl.BlockSpec((8, D_MODEL), lambda l, j: (0, 0)),
                  pl.BlockSpec((None, D_MODEL, tn), lambda l, j: (l, 0, j)),
                  pl.BlockSpec((None, 1, tn), lambda l, j: (l, 0, j))],
        out_specs=pl.BlockSpec((None, 8, tn), lambda l, j: (l, 0, j)),
        out_shape=jax.ShapeDtypeStruct((DEPTH, 8, n), F32),
        compiler_params=_cparams(2),
        name="ada",
    )(cvec, w_ada, b_ada.reshape(DEPTH, 1, n))


def _mod_spec(layer, path, tm, which):
    def index(i, *_):
        r = path.mod_base + path.mod_per_seq * ((i * tm) // path.seq_len)
        return ((layer * 8 + r) * 6 + which, 0, 0)
    return pl.BlockSpec((None, 1, D_MODEL), index)


def _in_proj_kernel(x_ref, g_ref, sc_ref, sh_ref, w_ref, o_ref, h_scr):
    @pl.when(pl.program_id(1) == 0)
    def _():
        g, sc1, sh = g_ref[...], 1.0 + sc_ref[...], sh_ref[...]
        for b in range(x_ref.shape[0] // NORM_ROWS):
            rs = slice(b * NORM_ROWS, (b + 1) * NORM_ROWS)
            h_scr[rs, :] = (_rms(x_ref[rs, :], g) * sc1 + sh).astype(BF16)
    o_ref[...] = jnp.dot(h_scr[...], w_ref[...], preferred_element_type=F32)


def _in_proj(x, mod3, norm1, w_in_p, layer, path, tm=IN_PROJ_TM, tn=IN_PROJ_TN):
    rows = x.shape[0]
    return pl.pallas_call(
        _in_proj_kernel,
        grid=(rows // tm, D_IN_P // tn),
        in_specs=[pl.BlockSpec((tm, D_MODEL), lambda i, j: (i, 0)),
                  pl.BlockSpec((None, 1, D_MODEL), lambda i, j: (layer, 0, 0)),
                  _mod_spec(layer, path, tm, 1),
                  _mod_spec(layer, path, tm, 0),
                  pl.BlockSpec((None, D_MODEL, tn), lambda i, j: (layer, 0, j))],
        out_specs=pl.BlockSpec((tm, tn), lambda i, j: (i, j)),
        out_shape=jax.ShapeDtypeStruct((rows, D_IN_P), F32),
        scratch_shapes=[pltpu.VMEM((tm, D_MODEL), BF16)],
        compiler_params=_cparams(2),
        name="in_proj",
    )(x, norm1.reshape(DEPTH, 1, D_MODEL), mod3, mod3, w_in_p)


def _chunk_of(di, n, nc):
    return n if di == 0 else nc - 1 - n


def _seq_row_spec(path, di, width, col_block):
    nc = path.nc
    return pl.BlockSpec((CHUNK, width), lambda s, n: (s * nc + _chunk_of(di, n, nc), col_block))


def _seq_gate_spec(path, di):
    return _seq_row_spec(path, di, LANES, COL_GATE + di)


def _seq_snap_spec(path, di, shape):
    nc = path.nc
    zeros = (0,) * len(shape)
    return pl.BlockSpec((None,) + shape, lambda s, n: (s * nc + _chunk_of(di, n, nc),) + zeros)


def _seq_state_spec(shape):
    zeros = (0,) * (len(shape) + 1)
    return pl.BlockSpec((None, N_DIR) + shape, lambda s, n: (s,) + zeros)


def _final_state_spec(shape, layer):
    zeros = (0,) * (len(shape) + 1)
    return pl.BlockSpec((None, None, N_DIR) + shape, lambda s, n: (s, layer) + zeros)


def _final_state_shape(path, shape):
    return jax.ShapeDtypeStruct((path.n_seq, path.state_layers, N_DIR) + shape, F32)


def _full_spec(shape):
    zeros = (0,) * len(shape)
    return pl.BlockSpec(shape, lambda *_: zeros)


def _out_row_spec(width, col_block):
    return pl.BlockSpec((CPS * CHUNK, width), lambda i: (i, col_block))


def _out_snap_spec(shape):
    zeros = (0,) * len(shape)
    return pl.BlockSpec((CPS,) + shape, lambda i: (i,) + zeros)


class _Part(NamedTuple):
    kernel: Callable
    in_specs: list
    out_specs: list
    out_shape: list
    scratch: list
    args: list
    carry: tuple = ()


def _fused_kernel(*refs, layout, rounds):
    n_in = sum(ni + nc for _, ni, nc, _, _ in layout)
    n_out = sum(no for _, _, _, no, _ in layout)
    ins, outs, scr = refs[:n_in], refs[n_in:n_in + n_out], refs[n_in + n_out:]
    for r, chunks in enumerate(rounds or (None,)):
        i = o = s = 0
        for body, ni, nc, no, ns in layout:
            kw = {} if chunks is None else dict(chunks=chunks, first_round=r == 0)
            body(*ins[i:i + ni], *outs[o:o + no], *scr[s:s + ns], **kw)
            i, o, s = i + ni + nc, o + no, s + ns


def _fused_call(parts, grid, name, rounds=None):
    layout = tuple((p.kernel, len(p.in_specs), len(p.carry), len(p.out_specs), len(p.scratch))
                   for p in parts)
    in_specs, args, aliases, n_out = [], [], {}, 0
    for p in parts:
        in_specs += p.in_specs
        args += p.args
        first_carried_out = n_out + len(p.out_specs) - len(p.carry)
        for c, arr in enumerate(p.carry):
            aliases[len(args)] = first_carried_out + c
            in_specs.append(pl.BlockSpec(memory_space=pl.ANY))
            args.append(arr)
        n_out += len(p.out_specs)
    outs = pl.pallas_call(
        functools.partial(_fused_kernel, layout=layout, rounds=rounds),
        grid=grid,
        in_specs=in_specs,
        out_specs=tuple(s for p in parts for s in p.out_specs),
        out_shape=tuple(s for p in parts for s in p.out_shape),
        scratch_shapes=[s for p in parts for s in p.scratch],
        input_output_aliases=aliases,
        compiler_params=_cparams(len(grid)),
        name=name,
    )(*args)
    split, o = [], 0
    for p in parts:
        split.append(outs[o:o + len(p.out_specs)])
        o += len(p.out_specs)
    return split


def _s5_kernel(uf_ref, ub_ref, bre_ref, bim_ref, cre_ref, cim_ref, pwr_ref, pwi_ref, h0r_ref, h0i_ref,
               yf_ref, yb_ref, hnr_ref, hni_ref,
               ut, yt, hlr, hli, st_r, st_i, *, nc):
    n = pl.program_id(1)

    @pl.when(n == 0)
    def _():
        st_r[...] = jnp.broadcast_to(h0r_ref[...], st_r.shape)
        st_i[...] = jnp.broadcast_to(h0i_ref[...], st_i.shape)

    u_refs = (uf_ref, ub_ref)
    y_refs = (yf_ref, yb_ref)
    seg = lax.broadcasted_iota(jnp.int32, (S5_SEG, S5_BW), 0)
    pos = (seg, S5_SEG - 1 - seg)

    def step_rows(di, kk):
        k = kk if di == 0 else S5_SLEN - 1 - kk
        return slice(k * S5_SEG, (k + 1) * S5_SEG)

    def lanes_of(cb):
        return slice(cb * S5_BW, (cb + 1) * S5_BW)

    for cb0 in range(0, S5_CB, S5_ILV):
        chains = [(di, cb) for cb in range(cb0, cb0 + S5_ILV) for di in range(N_DIR)]
        for di, cb in chains:
            ut[di, cb] = u_refs[di][:, cb * LANES:(cb + 1) * LANES]
            ucb = jnp.concatenate([ut[di, cb, pl.ds(k, S5_SEG, stride=S5_SLEN), :] for k in range(S5_SLEN)],
                                  axis=0).astype(BF16)
            hlr[di, cb] = jnp.dot(ucb, bre_ref[di, cb], preferred_element_type=F32)
            hli[di, cb] = jnp.dot(ucb, bim_ref[di, cb], preferred_element_type=F32)
        hr = {c: jnp.zeros((S5_SEG, S5_BW), F32) for c in chains}
        hi = {c: jnp.zeros((S5_SEG, S5_BW), F32) for c in chains}
        for kk in range(S5_SLEN):
            for c in chains:
                di, cb = c
                rows = step_rows(di, kk)
                lr = pwr_ref[di, 0, :, lanes_of(cb)]
                li = pwi_ref[di, 0, :, lanes_of(cb)]
                hr[c], hi[c] = (lr * hr[c] - li * hi[c] + hlr[di, cb, rows, :],
                                lr * hi[c] + li * hr[c] + hli[di, cb, rows, :])
                hlr[di, cb, rows, :] = hr[c]
                hli[di, cb, rows, :] = hi[c]
        cr, ci = {}, {}
        for c in chains:
            di, cb = c
            one = 1 if di == 0 else S5_SEG - 1
            first = pos[di] == 0
            cr[c] = jnp.where(first, pltpu.roll(st_r[di, :, lanes_of(cb)], one, 0),
                              pltpu.roll(hr[c], one, 0))
            ci[c] = jnp.where(first, pltpu.roll(st_i[di, :, lanes_of(cb)], one, 0),
                              pltpu.roll(hi[c], one, 0))
        for lvl in range(S5_SEG.bit_length() - 1):
            dist = 1 << lvl
            for c in chains:
                di, cb = c
                mr = pwr_ref[di, S5_SLEN - 1 + lvl, :, lanes_of(cb)]
                mi = pwi_ref[di, S5_SLEN - 1 + lvl, :, lanes_of(cb)]
                shift = dist if di == 0 else S5_SEG - dist
                sr = jnp.where(pos[di] >= dist, pltpu.roll(cr[c], shift, 0), 0.0)
                si = jnp.where(pos[di] >= dist, pltpu.roll(ci[c], shift, 0), 0.0)
                cr[c], ci[c] = cr[c] + mr * sr - mi * si, ci[c] + mr * si + mi * sr
        for c in chains:
            di, cb = c
            l16r = pwr_ref[di, S5_SLEN - 1, :, lanes_of(cb)]
            l16i = pwi_ref[di, S5_SLEN - 1, :, lanes_of(cb)]
            st_r[di, :, lanes_of(cb)] = hr[c] + l16r * cr[c] - l16i * ci[c]
            st_i[di, :, lanes_of(cb)] = hi[c] + l16r * ci[c] + l16i * cr[c]
        for kk in range(S5_SLEN):
            for c in chains:
                di, cb = c
                pr = pwr_ref[di, kk, :, lanes_of(cb)]
                pi = pwi_ref[di, kk, :, lanes_of(cb)]
                rows = step_rows(di, kk)
                hlr[di, cb, rows, :] = hlr[di, cb, rows, :] + pr * cr[c] - pi * ci[c]
                hli[di, cb, rows, :] = hli[di, cb, rows, :] + pr * ci[c] + pi * cr[c]
        for di, cb in chains:
            yt[di, cb] = (jnp.dot(hlr[di, cb].astype(BF16), cre_ref[di, cb], preferred_element_type=F32)
                          + jnp.dot(hli[di, cb].astype(BF16), cim_ref[di, cb], preferred_element_type=F32))
            per = S5_SLEN // S5_SEG
            for r in range(CHUNK // S5_SEG):
                start = (r % per) * S5_SEG * S5_SEG + r // per
                y_refs[di][r * S5_SEG:(r + 1) * S5_SEG, cb * LANES:(cb + 1) * LANES] = (
                    yt[di, cb, pl.ds(start, S5_SEG, stride=S5_SEG), :])

    @pl.when(n == nc - 1)
    def _():
        for di in range(N_DIR):
            last = S5_SEG - 1 if di == 0 else 0
            hnr_ref[di] = st_r[di, last:last + 1, :]
            hni_ref[di] = st_i[di, last:last + 1, :]


def _s5_params(lam_re, lam_im, log_dt, b_re, b_im, c_re, c_im):
    dt = jnp.exp(log_dt)[..., None]
    mag = jnp.exp(lam_re * dt)
    lbr, lbi = mag * jnp.cos(lam_im * dt), mag * jnp.sin(lam_im * dt)
    den = lam_re * lam_re + lam_im * lam_im
    fr = ((lbr - 1.0) * lam_re + lbi * lam_im) / den
    fi = (lbi * lam_re - (lbr - 1.0) * lam_im) / den
    bbr = fr[..., None] * b_re - fi[..., None] * b_im
    bbi = fr[..., None] * b_im + fi[..., None] * b_re
    gpb = S5_GROUPS // S5_CB
    eye = jnp.eye(gpb, dtype=F32)

    def pack_b(b):
        b = b.reshape(N_DIR, S5_CB, gpb, S5_P, S5_CH)
        m = jnp.einsum('dbgpc,gh->dbgchp', b, eye)
        return m.reshape(N_DIR, S5_CB, gpb * S5_CH, gpb * S5_P).astype(BF16)

    def pack_c(c):
        c = c.reshape(N_DIR, S5_CB, gpb, S5_CH, S5_P)
        m = jnp.einsum('dbgcp,gh->dbgphc', c, eye)
        return m.reshape(N_DIR, S5_CB, gpb * S5_P, gpb * S5_CH).astype(BF16)

    pr, pi = [lbr], [lbi]
    for _ in range(S5_SLEN - 1):
        pr, pi = pr + [pr[-1] * lbr - pi[-1] * lbi], pi + [pr[-1] * lbi + pi[-1] * lbr]
    for _ in range(S5_SEG.bit_length() - 2):
        pr, pi = pr + [pr[-1] * pr[-1] - pi[-1] * pi[-1]], pi + [2.0 * pr[-1] * pi[-1]]
    tab = lambda p: jnp.broadcast_to(jnp.stack(p, axis=1).reshape(N_DIR, len(p), 1, S5_ST),
                                     (N_DIR, len(p), S5_SEG, S5_ST))
    return pack_b(bbr), pack_b(bbi), pack_c(c_re), pack_c(-c_im), tab(pr), tab(pi)


def _s5_part(proj, prm, h0r, h0i, path, layer, carry):
    nc = path.nc
    bre, bim, cre, cim, pwr, pwi = prm
    yshape = jax.ShapeDtypeStruct((path.rows, GROUP_W), F32)
    stshape = _final_state_shape(path, (1, S5_ST))
    final = _final_state_spec((1, S5_ST), layer)
    yspec = lambda di: pl.BlockSpec((CHUNK, GROUP_W), lambda s, n: (s * nc + _chunk_of(di, n, nc), 0))
    dir_scr = lambda rows: pltpu.VMEM((N_DIR, S5_CB, rows, S5_BW), F32)
    return _Part(
        functools.partial(_s5_kernel, nc=nc),
        in_specs=[_seq_row_spec(path, 0, GROUP_W, COL_S5), _seq_row_spec(path, 1, GROUP_W, COL_S5),
                  _full_spec(bre.shape), _full_spec(bim.shape), _full_spec(cre.shape),
                  _full_spec(cim.shape), _full_spec(pwr.shape), _full_spec(pwi.shape),
                  _seq_state_spec((1, S5_ST)), _seq_state_spec((1, S5_ST))],
        out_specs=[yspec(0), yspec(1), final, final],
        out_shape=[yshape, yshape, stshape, stshape],
        scratch=[pltpu.VMEM((N_DIR, S5_CB, CHUNK, LANES), F32)] * 2
                + [dir_scr(CHUNK)] * 2 + [pltpu.VMEM((N_DIR, S5_SEG, S5_ST), F32)] * 2,
        args=[proj, proj, bre, bim, cre, cim, pwr, pwi, h0r, h0i],
        carry=carry)


def _gate_act(g):
    lane = lax.broadcasted_iota(jnp.int32, g.shape, 1)
    return jnp.where(lane < GATE_F, g, jnp.where(lane < GATE_DT, jax.nn.log_sigmoid(g), 0.0))


def _mlstm_state_kernel(kf_ref, vf_ref, gf_ref, kb_ref, vb_ref, gb_ref, bias_ref, c0_ref, n0_ref, m0_ref,
                        cf_ref, mf_ref, cb_ref, mb_ref, cn_ref, nn_ref, mn_ref, caug, mscr, *, nc):
    n = pl.program_id(1)
    dh = MLSTM_DH

    @pl.when(n == 0)
    def _():
        for di in range(N_DIR):
            for h in range(MLSTM_HEADS):
                caug[di, h, 0:dh, :] = c0_ref[di, h].T
                caug[di, h, dh:2 * dh, :] = jnp.broadcast_to(n0_ref[di, h], (dh, dh))
                mscr[di, h] = m0_ref[di, h]

    row, col = _iota2()
    ones = jnp.ones((CHUNK, dh), F32)
    dirs = ((kf_ref, vf_ref, gf_ref, cf_ref, mf_ref), (kb_ref, vb_ref, gb_ref, cb_ref, mb_ref))
    for di, (k_ref, v_ref, g_ref, c_out, m_out) in enumerate(dirs):
        p = _gate_act(g_ref[...] + bias_ref[di])
        bcol = _dot01_left(_to01(_dir_mask(di, row, col)), p)
        btot = jnp.sum(p, axis=0, keepdims=True)
        for h in range(MLSTM_HEADS):
            sl = slice(h * dh, (h + 1) * dh)
            k = k_ref[:, sl] * (dh ** -0.5)
            v = v_ref[:, sl]
            bc = bcol[:, GATE_F + h:GATE_F + h + 1]
            bt = btot[:, GATE_F + h:GATE_F + h + 1]
            igc = p[:, h:h + 1]
            mrow = mscr[di, h]
            mm = mrow[:, 0:1]
            cst = caug[di, h]
            c_out[h] = cst.astype(BF16)
            m_out[h] = mrow
            gcol = bt - bc + igc
            m_new = jnp.maximum(bt + mm, jnp.max(gcol, axis=0, keepdims=True))
            w_s = jnp.exp(gcol - m_new)
            decay_old = jnp.exp(bt + mm - m_new)
            vaug = jnp.concatenate([v, ones], axis=1)
            caug[di, h] = decay_old * cst + _bdot_tn(vaug, w_s * k)
            mscr[di, h] = jnp.broadcast_to(m_new, (1, LANES))

    @pl.when(n == nc - 1)
    def _():
        for di in range(N_DIR):
            for h in range(MLSTM_HEADS):
                cn_ref[di, h] = caug[di, h, 0:dh, :].T
                nn_ref[di, h] = caug[di, h, dh:dh + 1, :]
                mn_ref[di, h] = mscr[di, h]


def _mlstm_out_kernel(q_ref, k_ref, v_ref, o_ref, g0_ref, g1_ref, bias_ref, norm_ref,
                      cf_ref, mf_ref, cb_ref, mb_ref, y_ref, *, chunks, first_round):
    dh = MLSTM_DH
    row, col = _iota2()
    masks = [_dir_mask(di, row, col) for di in range(N_DIR)]
    masks_t = [masks[1 - di] for di in range(N_DIR)]
    m01 = [_to01(m) for m in masks]
    mt01 = [_to01(m) for m in masks_t]
    g_refs = (g0_ref, g1_ref)
    c_refs = (cf_ref, cb_ref)
    m_refs = (mf_ref, mb_ref)
    for ci in chunks:
        rs = slice(ci * CHUNK, (ci + 1) * CHUNK)
        ps, bcols, brows = [], [], []
        for di in range(N_DIR):
            p = _gate_act(g_refs[di][rs, :] + bias_ref[di])
            ps.append(p)
            bcols.append(_dot01_left(m01[di], p))
            brows.append(_dot01_right(p.T, mt01[di]))
        outs = []
        for h in range(MLSTM_HEADS):
            sl = slice(h * dh, (h + 1) * dh)
            q = q_ref[rs, sl].astype(BF16)
            k = (k_ref[rs, sl] * (dh ** -0.5)).astype(BF16)
            v = v_ref[rs, sl].astype(BF16)
            qk = _bdot_nt(k, q)
            wsum = None
            extra = None
            for di in range(N_DIR):
                fl = GATE_F + h
                a_s = ps[di][:, h:h + 1] - bcols[di][:, fl:fl + 1]
                b_t = brows[di][fl:fl + 1, :]
                mm = m_refs[di][ci, h]
                am = jnp.where(masks_t[di], a_s, -jnp.inf)
                big = jnp.maximum(mm, jnp.max(am, axis=0, keepdims=True))
                sc = qk * jnp.exp(am - big)
                w_inter = jnp.exp(mm - big)
                qc = _bdot_nt(c_refs[di][ci, h], q)
                den = jnp.sum(sc, axis=0, keepdims=True) + w_inter * qc[dh:dh + 1, :]
                r = 1.0 / jnp.maximum(jnp.abs(den), jnp.exp(-(b_t + big)))
                wsum = sc * r if wsum is None else wsum + sc * r
                e = qc[0:dh, :] * (w_inter * r)
                extra = e if extra is None else extra + e
            ht = _bdot_tn(v, wsum) + extra
            ht = ht * lax.rsqrt(jnp.mean(ht * ht, axis=0, keepdims=True) + EPS)
            outs.append(ht.T)
        hs = jnp.concatenate(outs, axis=1) * norm_ref[...]
        y_ref[rs, :] = (jax.nn.sigmoid(o_ref[rs, :]) * hs).astype(BF16)


def _mlstm_state_part(proj, bias, c0, n0, m0, path, layer, carry):
    nc = path.nc
    hh, dh = MLSTM_HEADS, MLSTM_DH
    csnap = jax.ShapeDtypeStruct((path.n_chunks, hh, 2 * dh, dh), BF16)
    msnap = jax.ShapeDtypeStruct((path.n_chunks, hh, 1, LANES), F32)
    seq_in = lambda di: [_seq_row_spec(path, di, GROUP_W, COL_MK), _seq_row_spec(path, di, GROUP_W, COL_MV),
                         _seq_gate_spec(path, di)]
    snap_out = lambda di: [_seq_snap_spec(path, di, (hh, 2 * dh, dh)), _seq_snap_spec(path, di, (hh, 1, LANES))]
    finals = [(hh, dh, dh), (hh, 1, dh), (hh, 1, LANES)]
    return _Part(
        functools.partial(_mlstm_state_kernel, nc=nc),
        in_specs=seq_in(0) + seq_in(1) + [_full_spec((N_DIR, 1, LANES)), _seq_state_spec((hh, dh, dh)),
                                         _seq_state_spec((hh, 1, dh)), _seq_state_spec((hh, 1, LANES))],
        out_specs=snap_out(0) + snap_out(1) + [_final_state_spec(s, layer) for s in finals],
        out_shape=[csnap, msnap, csnap, msnap] + [_final_state_shape(path, s) for s in finals],
        scratch=[pltpu.VMEM((N_DIR, hh, 2 * dh, dh), F32), pltpu.VMEM((N_DIR, hh, 1, LANES), F32)],
        args=[proj, proj, proj, proj, proj, proj, bias, c0, n0, m0],
        carry=carry)


def _mlstm_out_part(proj, bias, norm, snaps, path):
    hh, dh = MLSTM_HEADS, MLSTM_DH
    return _Part(
        _mlstm_out_kernel,
        in_specs=[_out_row_spec(GROUP_W, COL_MQ), _out_row_spec(GROUP_W, COL_MK),
                  _out_row_spec(GROUP_W, COL_MV), _out_row_spec(GROUP_W, COL_MO),
                  _out_row_spec(LANES, COL_GATE), _out_row_spec(LANES, COL_GATE + 1),
                  _full_spec((N_DIR, 1, LANES)), _full_spec((1, GROUP_W)),
                  _out_snap_spec((hh, 2 * dh, dh)), _out_snap_spec((hh, 1, LANES)),
                  _out_snap_spec((hh, 2 * dh, dh)), _out_snap_spec((hh, 1, LANES))],
        out_specs=[_out_row_spec(GROUP_W, 0)],
        out_shape=[jax.ShapeDtypeStruct((path.rows, GROUP_W), BF16)],
        scratch=[],
        args=[proj, proj, proj, proj, proj, proj, bias, norm] + list(snaps))


def _rope(a, cos, sin):
    return a * cos + pltpu.roll(a, RET_DH // 2, 1) * sin


def _ret_state_kernel(*refs, nc, rope):
    if rope:
        (kf_ref, vf_ref, kb_ref, vb_ref, lg_ref, s0_ref, cosf_ref, sinf_ref, cosb_ref, sinb_ref,
         sf_ref, sb_ref, sn_ref, sscr) = refs
        tabs = ((cosf_ref, sinf_ref), (cosb_ref, sinb_ref))
    else:
        kf_ref, vf_ref, kb_ref, vb_ref, lg_ref, s0_ref, sf_ref, sb_ref, sn_ref, sscr = refs
    n = pl.program_id(1)
    dh = RET_DH

    @pl.when(n == 0)
    def _():
        sscr[...] = s0_ref[...]

    t = lax.broadcasted_iota(jnp.int32, (CHUNK, dh), 0).astype(F32)
    for di, (k_ref, v_ref, s_out) in enumerate(((kf_ref, vf_ref, sf_ref), (kb_ref, vb_ref, sb_ref))):
        left = CHUNK - 1.0 - t if di == 0 else t
        for h in range(RET_HEADS):
            sl = slice(h * dh, (h + 1) * dh)
            k = k_ref[:, sl]
            if rope:
                k = _rope(k, tabs[di][0][...], tabs[di][1][...])
            k = k * (dh ** -0.5)
            lg = -jnp.exp(lg_ref[di, :, sl])
            st = sscr[di, h]
            s_out[h] = st.astype(BF16)
            sscr[di, h] = jnp.exp(CHUNK * lg) * st + _bdot((k * jnp.exp(left * lg)).T, v_ref[:, sl])

    @pl.when(n == nc - 1)
    def _():
        sn_ref[...] = sscr[...]


def _ret_out_kernel(*refs, rope, chunks, first_round):
    if rope:
        (q_ref, k_ref, v_ref, g_ref, lg_ref, norm_ref, sf_ref, sb_ref, cos_ref, sin_ref,
         y_ref, dsum, qdec) = refs
    else:
        q_ref, k_ref, v_ref, g_ref, lg_ref, norm_ref, sf_ref, sb_ref, y_ref, dsum, qdec = refs
    dh = RET_DH

    def build_tables():
        row, col = _iota2()
        absdiff = jnp.abs(row - col).astype(F32)
        t = row.astype(F32)
        for h in range(RET_HEADS):
            sl = slice(h * dh, (h + 1) * dh)
            lgf = -jnp.exp(lg_ref[0, :, sl])
            lgb = -jnp.exp(lg_ref[1, :, sl])
            both = jnp.exp(absdiff * jnp.where(row >= col, lgf, lgb))
            dsum[h] = jnp.where(row == col, 2.0, both)
            qdec[0, h] = jnp.exp((t + 1.0) * lgf)
            qdec[1, h] = jnp.exp((CHUNK - t) * lgb)

    if first_round:
        pl.when(pl.program_id(0) == 0)(build_tables)

    s_refs = (sf_ref, sb_ref)
    for ci in chunks:
        rs = slice(ci * CHUNK, (ci + 1) * CHUNK)
        outs = []
        for h in range(RET_HEADS):
            sl = slice(h * dh, (h + 1) * dh)
            q = q_ref[rs, sl]
            k = k_ref[rs, sl]
            if rope:
                cos, sin = cos_ref[rs, :], sin_ref[rs, :]
                q = _rope(q, cos, sin)
                k = _rope(k, cos, sin)
            q = q.astype(BF16)
            k = (k * (dh ** -0.5)).astype(BF16)
            y = _bdot(_bdot_nt(q, k) * dsum[h], v_ref[rs, sl])
            for di in range(N_DIR):
                y = y + qdec[di, h] * jnp.dot(q, s_refs[di][ci, h], preferred_element_type=F32)
            outs.append(y)
        ys = _head_norm(jnp.concatenate(outs, axis=1), RET_HEADS, dh) * norm_ref[...]
        gate = g_ref[rs, :]
        y_ref[rs, :] = (ys * (gate * jax.nn.sigmoid(gate))).astype(BF16)


def _rope_tables(seq_len):
    n_rows = seq_len // GRID_W
    rows = jnp.repeat(jnp.arange(n_rows, dtype=F32), GRID_W)
    cols = jnp.tile(jnp.arange(GRID_W, dtype=F32), n_rows)
    quarter = RET_DH // 4
    freqs = ROPE_BASE ** (-jnp.arange(quarter, dtype=F32) / quarter)
    ang = jnp.concatenate([rows[:, None] * freqs, cols[:, None] * freqs], axis=-1)
    cos, sin = jnp.cos(ang), jnp.sin(ang)
    return jnp.concatenate([cos, cos], axis=-1), jnp.concatenate([-sin, sin], axis=-1)


def _ret_state_part(proj, lg, s0, rope_tabs, path, layer, carry):
    nc = path.nc
    hh, dh = RET_HEADS, RET_DH
    snap = jax.ShapeDtypeStruct((path.n_chunks, hh, dh, dh), BF16)
    seq_in = lambda di: [_seq_row_spec(path, di, GROUP_W, COL_RK), _seq_row_spec(path, di, GROUP_W, COL_RV)]
    in_specs = seq_in(0) + seq_in(1) + [_full_spec((N_DIR, 1, GROUP_W)), _seq_state_spec((hh, dh, dh))]
    args = [proj, proj, proj, proj, lg, s0]
    if path.rope:
        for di in range(N_DIR):
            tab = pl.BlockSpec((CHUNK, dh), lambda s, n, di=di: (_chunk_of(di, n, nc), 0))
            in_specs += [tab, tab]
            args += list(rope_tabs)
    return _Part(
        functools.partial(_ret_state_kernel, nc=nc, rope=path.rope),
        in_specs=in_specs,
        out_specs=[_seq_snap_spec(path, 0, (hh, dh, dh)), _seq_snap_spec(path, 1, (hh, dh, dh)),
                   _final_state_spec((hh, dh, dh), layer)],
        out_shape=[snap, snap, _final_state_shape(path, (hh, dh, dh))],
        scratch=[pltpu.VMEM((N_DIR, hh, dh, dh), F32)],
        args=args,
        carry=carry)


def _ret_out_part(proj, lg, norm, snaps, rope_tabs, path):
    hh, dh = RET_HEADS, RET_DH
    in_specs = [_out_row_spec(GROUP_W, COL_RQ), _out_row_spec(GROUP_W, COL_RK),
                _out_row_spec(GROUP_W, COL_RV), _out_row_spec(GROUP_W, COL_RG),
                _full_spec((N_DIR, 1, GROUP_W)), _full_spec((1, GROUP_W)),
                _out_snap_spec((hh, dh, dh)), _out_snap_spec((hh, dh, dh))]
    args = [proj, proj, proj, proj, lg, norm] + list(snaps)
    if path.rope:
        per_seq = path.nc // CPS
        tab = pl.BlockSpec((CPS * CHUNK, dh), lambda i: (i % per_seq, 0))
        in_specs += [tab, tab]
        args += list(rope_tabs)
    return _Part(
        functools.partial(_ret_out_kernel, rope=path.rope),
        in_specs=in_specs,
        out_specs=[_out_row_spec(GROUP_W, 0)],
        out_shape=[jax.ShapeDtypeStruct((path.rows, GROUP_W), BF16)],
        scratch=[pltpu.VMEM((hh, CHUNK, CHUNK), F32), pltpu.VMEM((N_DIR, hh, CHUNK, dh), F32)],
        args=args)


def _conv_kernel(x_ref, xp_ref, xn_ref, cw_ref, cb_ref, o_ref, ext, *, bps, tr):
    ib = pl.program_id(0) % bps
    ext[0:HALO, :] = jnp.where(ib > 0, xp_ref[...], 0.0)
    ext[HALO:HALO + tr, :] = x_ref[...]
    ext[HALO + tr:, :] = jnp.where(ib < bps - 1, xn_ref[...], 0.0)
    pad = (SSD_CONV - 1) // 2
    acc = cb_ref[...] + cw_ref[0:1, :] * ext[HALO - pad:HALO - pad + tr, :]
    for j in range(1, SSD_CONV):
        acc = acc + cw_ref[j:j + 1, :] * ext[HALO - pad + j:HALO - pad + j + tr, :]
    o_ref[...] = acc * jax.nn.sigmoid(acc)


def _ssd_conv(proj, conv_w, conv_b, path):
    tr = min(CONV_ROWS, path.seq_len)
    per = tr // HALO
    n_halo = path.rows // HALO
    bps = path.seq_len // tr
    return pl.pallas_call(
        functools.partial(_conv_kernel, bps=bps, tr=tr),
        grid=(path.rows // tr,),
        in_specs=[pl.BlockSpec((tr, SSD_XBC), lambda i: (i, COL_XBC)),
                  pl.BlockSpec((HALO, SSD_XBC), lambda i: (jnp.maximum(i * per - 1, 0), COL_XBC)),
                  pl.BlockSpec((HALO, SSD_XBC), lambda i: (jnp.minimum((i + 1) * per, n_halo - 1), COL_XBC)),
                  _full_spec((SSD_CONV, SSD_XBC)), _full_spec((1, SSD_XBC))],
        out_specs=pl.BlockSpec((tr, SSD_XBC), lambda i: (i, 0)),
        out_shape=jax.ShapeDtypeStruct((path.rows, SSD_XBC), F32),
        scratch_shapes=[pltpu.VMEM((tr + 2 * HALO, SSD_XBC), F32)],
        compiler_params=_cparams(1),
        name="ssd_conv",
    )(proj, proj, proj, conv_w, conv_b)


def _ssd_rates(g, bias, alog):
    lane = lax.broadcasted_iota(jnp.int32, g.shape, 1)
    dt_lanes = (lane >= GATE_DT) & (lane < GATE_DT + SSD_HEADS)
    dt = jnp.where(dt_lanes, jax.nn.softplus(g + bias), 0.0)
    return dt, dt * -jnp.exp(alog)


def _ssd_state_kernel(xf_ref, bf_ref, gf_ref, xb_ref, bb_ref, gb_ref, bias_ref, alog_ref, h0_ref,
                      hf_ref, hb_ref, hn_ref, hscr, *, nc):
    n = pl.program_id(1)

    @pl.when(n == 0)
    def _():
        for di in range(N_DIR):
            for gi in range(SSD_GROUPS):
                hscr[di, gi] = h0_ref[di, gi * SSD_HPG:(gi + 1) * SSD_HPG].reshape(SSD_GW, SSD_N).T

    row, col = _iota2()
    dirs = ((xf_ref, bf_ref, gf_ref, hf_ref), (xb_ref, bb_ref, gb_ref, hb_ref))
    for di, (x_ref, b_ref, g_ref, h_out) in enumerate(dirs):
        dt, a = _ssd_rates(g_ref[...], bias_ref[di], alog_ref[di])
        ccol = _dot01_left(_to01(_dir_mask(di, row, col)), a)
        ctot = jnp.sum(a, axis=0, keepdims=True)
        for gi in range(SSD_GROUPS):
            hst = hscr[di, gi]
            h_out[gi] = hst.astype(BF16)
            wx_parts, tot_parts = [], []
            for r in range(SSD_HPG):
                hd = gi * SSD_HPG + r
                ln = GATE_DT + hd
                tot = ctot[:, ln:ln + 1]
                w = jnp.exp(tot - ccol[:, ln:ln + 1]) * dt[:, ln:ln + 1]
                wx_parts.append(w * x_ref[:, hd * SSD_P:(hd + 1) * SSD_P])
                tot_parts.append(jnp.broadcast_to(jnp.exp(tot), (1, SSD_P)))
            wx = jnp.concatenate(wx_parts, axis=1)
            bm = b_ref[:, gi * SSD_N:(gi + 1) * SSD_N]
            hscr[di, gi] = jnp.concatenate(tot_parts, axis=1) * hst + _bdot(bm.T, wx)

    @pl.when(n == nc - 1)
    def _():
        for di in range(N_DIR):
            for gi in range(SSD_GROUPS):
                hn_ref[di, gi * SSD_HPG:(gi + 1) * SSD_HPG] = hscr[di, gi].T.reshape(SSD_HPG, SSD_P, SSD_N)


def _ssd_out_kernel(z_ref, xbc_ref, g0_ref, g1_ref, bias_ref, alog_ref, dsk_ref, norm_ref,
                    hf_ref, hb_ref, y_ref, *, chunks, first_round):
    row, col = _iota2()
    masks = [_dir_mask(di, row, col) for di in range(N_DIR)]
    m01 = [_to01(m) for m in masks]
    mt01 = [_to01(masks[1 - di]) for di in range(N_DIR)]
    g_refs = (g0_ref, g1_ref)
    h_refs = (hf_ref, hb_ref)
    for ci in chunks:
        rs = slice(ci * CHUNK, (ci + 1) * CHUNK)
        dtts, ccols, crows = [], [], []
        for di in range(N_DIR):
            dt, a = _ssd_rates(g_refs[di][rs, :], bias_ref[di], alog_ref[di])
            dtts.append(dt.T)
            ccols.append(_dot01_left(m01[di], a))
            crows.append(_dot01_right(a.T, mt01[di]))
        xs = xbc_ref[rs, 0:GROUP_W]
        y_parts = []
        for gi in range(SSD_GROUPS):
            bm = xbc_ref[rs, GROUP_W + gi * SSD_N:GROUP_W + (gi + 1) * SSD_N].astype(BF16)
            c0 = GROUP_W + (SSD_GROUPS + gi) * SSD_N
            cm = xbc_ref[rs, c0:c0 + SSD_N].astype(BF16)
            cb = _bdot_nt(cm, bm)
            inter = [jnp.dot(cm, h_refs[di][ci, gi], preferred_element_type=F32) for di in range(N_DIR)]
            for r in range(SSD_HPG):
                hd = gi * SSD_HPG + r
                ln = GATE_DT + hd
                hs = slice(r * SSD_P, (r + 1) * SSD_P)
                wmat = None
                y = None
                for di in range(N_DIR):
                    cc = ccols[di][:, ln:ln + 1]
                    decay = jnp.exp(jnp.where(masks[di], cc - crows[di][ln:ln + 1, :], -jnp.inf))
                    wd = decay * dtts[di][ln:ln + 1, :]
                    wmat = wd if wmat is None else wmat + wd
                    e = jnp.exp(cc) * inter[di][:, hs]
                    y = e if y is None else y + e
                y_parts.append(y + _bdot(cb * wmat, xs[:, hd * SSD_P:(hd + 1) * SSD_P]))
        z = z_ref[rs, :]
        yy = (dsk_ref[...] * xs + jnp.concatenate(y_parts, axis=1)) * (z * jax.nn.sigmoid(z))
        y_ref[rs, :] = _rms(yy, norm_ref[...]).astype(BF16)


SSD_SNAP = (SSD_GROUPS, SSD_N, SSD_GW)


def _ssd_state_part(proj, xbc, bias, alog, h0, path, layer, carry):
    nc = path.nc
    st_shape = (SSD_HEADS, SSD_P, SSD_N)
    snap_shape = SSD_SNAP
    snap = jax.ShapeDtypeStruct((path.n_chunks,) + snap_shape, BF16)

    def seq_in(di):
        return [pl.BlockSpec((CHUNK, GROUP_W), lambda s, n: (s * nc + _chunk_of(di, n, nc), 0)),
                pl.BlockSpec((CHUNK, SSD_GROUPS * SSD_N),
                             lambda s, n: (s * nc + _chunk_of(di, n, nc), GROUP_W // (SSD_GROUPS * SSD_N))),
                _seq_gate_spec(path, di)]

    return _Part(
        functools.partial(_ssd_state_kernel, nc=nc),
        in_specs=seq_in(0) + seq_in(1) + [_full_spec((N_DIR, 1, LANES)), _full_spec((N_DIR, 1, LANES)),
                                         _seq_state_spec(st_shape)],
        out_specs=[_seq_snap_spec(path, 0, snap_shape), _seq_snap_spec(path, 1, snap_shape),
                   _final_state_spec(st_shape, layer)],
        out_shape=[snap, snap, _final_state_shape(path, st_shape)],
        scratch=[pltpu.VMEM((N_DIR,) + snap_shape, F32)],
        args=[xbc, xbc, proj, xbc, xbc, proj, bias, alog, h0],
        carry=carry)


def _ssd_out_part(proj, xbc, bias, alog, dskip, norm, snaps, path):
    return _Part(
        _ssd_out_kernel,
        in_specs=[_out_row_spec(GROUP_W, COL_DZ), _out_row_spec(SSD_XBC, 0),
                  _out_row_spec(LANES, COL_GATE), _out_row_spec(LANES, COL_GATE + 1),
                  _full_spec((N_DIR, 1, LANES)), _full_spec((N_DIR, 1, LANES)),
                  _full_spec((1, GROUP_W)), _full_spec((1, GROUP_W)),
                  _out_snap_spec(SSD_SNAP), _out_snap_spec(SSD_SNAP)],
        out_specs=[_out_row_spec(GROUP_W, 0)],
        out_shape=[jax.ShapeDtypeStruct((path.rows, GROUP_W), BF16)],
        scratch=[],
        args=[proj, xbc, proj, proj, bias, alog, dskip, norm] + list(snaps))


def _out_proj_kernel(u_ref, yf_ref, yb_ref, dsk_ref, wglu_ref, yb2_ref, yc_ref, yd_ref, w_ref, x_ref,
                     g1_ref, g_ref, sc_ref, sh_ref, x1_ref, h2_ref):
    t = jax.nn.gelu(dsk_ref[...] * u_ref[...] + yf_ref[...] + yb_ref[...])
    ya = (t * jax.nn.sigmoid(_bdot(t, wglu_ref[...]))).astype(BF16)
    acc = jnp.dot(ya, w_ref[0], preferred_element_type=F32)
    for i, y_ref in enumerate((yb2_ref, yc_ref, yd_ref)):
        acc = acc + jnp.dot(y_ref[...], w_ref[i + 1], preferred_element_type=F32)
    x1 = x_ref[...] + g1_ref[...] * acc
    x1_ref[...] = x1
    h2_ref[...] = (_rms(x1, g_ref[...]) * (1.0 + sc_ref[...]) + sh_ref[...]).astype(BF16)


def _out_proj(proj, s5_y, s5_post, ys, x, mod3, norm2, w_out_p, layer, path, tm=OUT_PROJ_TM):
    rows = x.shape[0]
    yspec = pl.BlockSpec((tm, GROUP_W), lambda i: (i, 0))
    xspec = pl.BlockSpec((tm, D_MODEL), lambda i: (i, 0))
    dskip, wglu = s5_post
    return pl.pallas_call(
        _out_proj_kernel,
        grid=(rows // tm,),
        in_specs=[pl.BlockSpec((tm, GROUP_W), lambda i: (i, COL_S5)), yspec, yspec,
                  _full_spec((1, GROUP_W)), _full_spec((GROUP_W, GROUP_W)),
                  yspec, yspec, yspec,
                  pl.BlockSpec((None, 4, GROUP_W, D_MODEL), lambda i: (layer, 0, 0, 0)),
                  xspec,
                  _mod_spec(layer, path, tm, 2),
                  pl.BlockSpec((None, 1, D_MODEL), lambda i: (layer, 0, 0)),
                  _mod_spec(layer, path, tm, 4),
                  _mod_spec(layer, path, tm, 3)],
        out_specs=(xspec, xspec),
        out_shape=(jax.ShapeDtypeStruct((rows, D_MODEL), F32),
                   jax.ShapeDtypeStruct((rows, D_MODEL), BF16)),
        compiler_params=_cparams(1),
        name="out_proj",
    )(proj, s5_y[0], s5_y[1], dskip, wglu, *ys, w_out_p, x, mod3, norm2.reshape(DEPTH, 1, D_MODEL),
      mod3, mod3)


def _mlp_kernel(h2_ref, x1_ref, w1_ref, w2_ref, g2_ref, fn_ref, o_ref, *, nk, final):
    k = pl.program_id(1)

    @pl.when(k == 0)
    def _():
        o_ref[...] = x1_ref[...]

    a = jnp.dot(h2_ref[...], w1_ref[...], preferred_element_type=F32)
    a = jnp.square(jnp.maximum(a, 0.0))
    o_ref[...] += g2_ref[...] * jnp.dot(a.astype(BF16), w2_ref[...], preferred_element_type=F32)

    if final:
        @pl.when(k == nk - 1)
        def _():
            o_ref[...] = _rms(o_ref[...], fn_ref[...])


def _mlp(h2, x1, mod3, w1, w2, final_norm, layer, path, final, tm=MLP_TM, tf=MLP_TF):
    rows = x1.shape[0]
    nk = D_FF // tf
    xspec = pl.BlockSpec((tm, D_MODEL), lambda i, k: (i, 0))
    return pl.pallas_call(
        functools.partial(_mlp_kernel, nk=nk, final=final),
        grid=(rows // tm, nk),
        in_specs=[xspec, xspec,
                  pl.BlockSpec((None, D_MODEL, tf), lambda i, k: (layer, 0, k)),
                  pl.BlockSpec((None, tf, D_MODEL), lambda i, k: (layer, k, 0)),
                  _mod_spec(layer, path, tm, 5),
                  pl.BlockSpec((1, D_MODEL), lambda i, k: (0, 0))],
        out_specs=xspec,
        out_shape=jax.ShapeDtypeStruct((rows, D_MODEL), F32),
        compiler_params=_cparams(2),
        name="mlp",
    )(h2, x1, w1, w2, mod3, final_norm.reshape(1, D_MODEL))


def _permute_w_in(w_in):
    o_mi = 5 * GROUP_W
    o_mf = o_mi + N_DIR * MLSTM_HEADS
    o_r = o_mf + N_DIR * MLSTM_HEADS
    o_dt = D_IN - N_DIR * SSD_HEADS
    zeros = jnp.zeros(w_in.shape[:2] + (LANES - 2 * MLSTM_HEADS - SSD_HEADS,), w_in.dtype)
    parts = [w_in[..., :o_mi], w_in[..., o_r:o_dt]]
    for di in range(N_DIR):
        parts += [w_in[..., o_mi + di * MLSTM_HEADS:o_mi + (di + 1) * MLSTM_HEADS],
                  w_in[..., o_mf + di * MLSTM_HEADS:o_mf + (di + 1) * MLSTM_HEADS],
                  w_in[..., o_dt + di * SSD_HEADS:o_dt + (di + 1) * SSD_HEADS], zeros]
    return jnp.concatenate(parts, axis=-1).astype(BF16)


def _gate_rows(i_part, f_part, dt_part):
    pad = jnp.zeros(i_part.shape[:2] + (LANES - 2 * MLSTM_HEADS - SSD_HEADS,), F32)
    return jnp.concatenate([i_part, f_part, dt_part, pad], axis=-1)[:, :, None, :]


def kernel(x_prompt, x_sample, c, state_s5_re, state_s5_im, state_mlstm_c, state_mlstm_n, state_mlstm_m, state_ret, state_ssd, c_ctx, norm1, w_ada, b_ada, w_in, s5_lambda_re, s5_lambda_im, s5_log_dt, s5_b_re, s5_b_im, s5_c_re, s5_c_im, s5_d, s5_w_glu, mlstm_i_bias, mlstm_f_bias, mlstm_norm, ret_log_rate, ret_norm, ssd_conv_w, ssd_conv_b, ssd_dt_bias, ssd_a_log, ssd_d, ssd_norm, w_out, norm2, w_mlp1, w_mlp2, final_norm):
    bp, seq = x_prompt.shape[0], x_prompt.shape[1]
    bd, dseq = x_sample.shape[0], x_sample.shape[1]
    ctx = _Path(bp, seq, 0, 0, False, state_layers=DEPTH)
    lat = _Path(bd, dseq, 1, 1, True)

    cvec = jnp.concatenate([c_ctx[None], c, jnp.zeros((8 - 1 - bd, D_MODEL), F32)], axis=0)
    mod3 = _ada(cvec, w_ada, b_ada).reshape(DEPTH * 8 * 6, 1, D_MODEL)

    w_in_p = _permute_w_in(w_in)
    w_out_p = w_out.astype(BF16).reshape(DEPTH, 4, GROUP_W, D_MODEL)
    w1 = w_mlp1.astype(BF16)
    w2 = w_mlp2.astype(BF16)
    wglu = s5_w_glu.astype(BF16)
    zeros4 = jnp.zeros((DEPTH, N_DIR, MLSTM_HEADS), F32)
    gate_bias = _gate_rows(mlstm_i_bias, mlstm_f_bias, ssd_dt_bias)
    alog_rows = _gate_rows(zeros4, zeros4, ssd_a_log)
    ret_lg = jnp.repeat(ret_log_rate, RET_DH, axis=-1)[:, :, None, :]
    ssd_d_row = jnp.repeat(ssd_d, SSD_P, axis=-1)[:, None, :]
    rope_tabs = _rope_tables(dseq)

    def states(path, layer):
        if path is ctx:
            n = path.n_seq
            z = lambda *s: jnp.zeros((n, N_DIR) + s, F32)
            return (z(1, S5_ST), z(1, S5_ST), z(MLSTM_HEADS, MLSTM_DH, MLSTM_DH),
                    z(MLSTM_HEADS, 1, MLSTM_DH), z(MLSTM_HEADS, 1, LANES),
                    z(RET_HEADS, RET_DH, RET_DH), z(SSD_HEADS, SSD_P, SSD_N))
        return (state_s5_re[:, layer].reshape(bd, N_DIR, 1, S5_ST),
                state_s5_im[:, layer].reshape(bd, N_DIR, 1, S5_ST),
                state_mlstm_c[:, layer],
                state_mlstm_n[:, layer][..., None, :],
                jnp.broadcast_to(state_mlstm_m[:, layer][..., None, None],
                                 (bd, N_DIR, MLSTM_HEADS, 1, LANES)),
                state_ret[:, layer], state_ssd[:, layer])

    xs = {ctx: x_prompt.reshape(ctx.rows, D_MODEL), lat: x_sample.reshape(lat.rows, D_MODEL)}
    final_shapes = [(1, S5_ST), (1, S5_ST), (MLSTM_HEADS, MLSTM_DH, MLSTM_DH), (MLSTM_HEADS, 1, MLSTM_DH),
                    (MLSTM_HEADS, 1, LANES), (RET_HEADS, RET_DH, RET_DH), (SSD_HEADS, SSD_P, SSD_N)]
    acc = {ctx: [jnp.zeros((bp, DEPTH, N_DIR) + s, F32) for s in final_shapes], lat: None}
    for layer in range(DEPTH):
        s5_prm = _s5_params(s5_lambda_re[layer], s5_lambda_im[layer], s5_log_dt[layer],
                            s5_b_re[layer], s5_b_im[layer], s5_c_re[layer], s5_c_im[layer])
        s5_post = (s5_d[layer][None], wglu[layer])
        for path in (ctx, lat):
            x = xs[path]
            st = states(path, layer)
            proj = _in_proj(x, mod3, norm1, w_in_p, layer, path)
            xbc = _ssd_conv(proj, ssd_conv_w[layer], ssd_conv_b[layer][None], path)
            a = acc[path]
            carry = (lambda lo, hi: tuple(a[lo:hi])) if a else (lambda lo, hi: ())
            slot = layer if a else 0
            seq_out = _fused_call(
                [_s5_part(proj, s5_prm, st[0], st[1], path, slot, carry(0, 2)),
                 _mlstm_state_part(proj, gate_bias[layer], st[2], st[3], st[4], path, slot, carry(2, 5)),
                 _ret_state_part(proj, ret_lg[layer], st[5], rope_tabs, path, slot, carry(5, 6)),
                 _ssd_state_part(proj, xbc, gate_bias[layer], alog_rows[layer], st[6], path, slot,
                                 carry(6, 7))],
                (path.n_seq, path.nc), "mix_seq")
            (s5f, s5b, s5r, s5i), m_out, r_out, d_out = seq_out
            if a:
                acc[path] = [s5r, s5i, *m_out[4:], r_out[2], d_out[2]]
            (yb,), (yc,), (yd,) = _fused_call(
                [_mlstm_out_part(proj, gate_bias[layer], mlstm_norm[layer][None], m_out[:4], path),
                 _ret_out_part(proj, ret_lg[layer], ret_norm[layer][None], r_out[:2], rope_tabs, path),
                 _ssd_out_part(proj, xbc, gate_bias[layer], alog_rows[layer], ssd_d_row[layer],
                               ssd_norm[layer][None], d_out[:2], path)],
                (path.n_chunks // CPS,), "mix_out", rounds=tuple((ci,) for ci in range(CPS)))
            x1, h2 = _out_proj(proj, (s5f, s5b), s5_post, (yb, yc, yd), x, mod3, norm2, w_out_p,
                               layer, path)
            xs[path] = _mlp(h2, x1, mod3, w1, w2, final_norm, layer, path, layer == DEPTH - 1)
    y_prompt = xs[ctx].reshape(bp, seq, D_MODEL)
    y_sample = xs[lat].reshape(bd, dseq, D_MODEL)
    s5r, s5i, mc, mn, mm, rs, dh = acc[ctx]
    return (y_prompt, y_sample,
            s5r.reshape(bp, DEPTH, N_DIR, S5_GROUPS, S5_P), s5i.reshape(bp, DEPTH, N_DIR, S5_GROUPS, S5_P),
            mc, mn[..., 0, :], mm[..., 0, 0], rs, dh)
```
